```python
import math
import jax
import jax.numpy as jnp
from jax import lax
import numpy as np

D_MODEL = 4096
BATCH = 1
SEQ = 16384
DEPTH = 2

HEAD_DIM = 128
BLOCK = 128
SUPER_BLOCK = 1024
EPS = 1e-6
A_HEADS = D_MODEL // 512
A_KV_HEADS = max(A_HEADS // 4, 1)
A_GROUP = A_HEADS // A_KV_HEADS
A_WIDTH = A_HEADS * HEAD_DIM
A_KV_WIDTH = A_KV_HEADS * HEAD_DIM
WINDOW = 128
NUM_BUCKETS = 32
MAX_DISTANCE = 128
B_HEADS = D_MODEL // 1024
B_Q_LORA = D_MODEL // 4
B_KV_LORA = 512
B_NOPE_DIM = 128
B_ROPE_DIM = 64
B_V_DIM = 128
B_QK_DIM = B_NOPE_DIM + B_ROPE_DIM
B_WIDTH = B_HEADS * B_V_DIM
ROPE_THETA = 10000.0
C_HEADS = D_MODEL // 1024
C_WIDTH = C_HEADS * HEAD_DIM
IN_SIZES = (A_WIDTH, A_KV_WIDTH, A_KV_WIDTH, A_WIDTH,
            B_Q_LORA, B_KV_LORA, B_ROPE_DIM, B_WIDTH,
            C_WIDTH, C_WIDTH, C_WIDTH, C_WIDTH,
            D_MODEL, D_MODEL, D_MODEL)
IN_COLS = sum(IN_SIZES)

kernel_name = "hybrid_swa_mla_stickbreaking_block"


def rms_norm(x, g):
    xf = x.astype(jnp.float32)
    y = xf * lax.rsqrt(jnp.mean(xf * xf, axis=-1, keepdims=True) + EPS)
    return (y * g.astype(jnp.float32)).astype(x.dtype)


def t5_bucket(rel):
    n = jnp.maximum(rel, 0)
    max_exact = NUM_BUCKETS // 2
    logn = jnp.log(jnp.maximum(n, 1).astype(jnp.float32) / max_exact)
    large = max_exact + (logn / math.log(MAX_DISTANCE / max_exact) * (NUM_BUCKETS - max_exact)).astype(jnp.int32)
    large = jnp.minimum(large, NUM_BUCKETS - 1)
    return jnp.where(n < max_exact, n, large)


def swa_bias_and_mask(rel_bias, n_blocks):
    t = jnp.arange(BLOCK)[:, None]
    s = jnp.arange(2 * BLOCK)[None, :]
    rel = BLOCK + t - s
    bias = rel_bias[t5_bucket(rel)]
    bias = bias.reshape(BLOCK, 2 * BLOCK, A_KV_HEADS, A_GROUP).transpose(2, 3, 0, 1)
    key_abs = jnp.arange(n_blocks)[:, None, None] * BLOCK + s[None] - BLOCK
    mask = (rel >= 0)[None] & (rel < WINDOW)[None] & (key_abs >= 0)
    return bias, mask


def swa_sink_attention(q, k, v, sinks, bias, mask):
    B, S, HQ, D = q.shape
    nb = S // BLOCK
    qb = q.reshape(B, nb, BLOCK, A_KV_HEADS, A_GROUP, D)

    def band(t):
        tb = t.reshape(B, nb, BLOCK, A_KV_HEADS, D)
        prev = jnp.concatenate([jnp.zeros_like(tb[:, :1]), tb[:, :-1]], axis=1)
        return jnp.concatenate([prev, tb], axis=2)

    kk, vv = band(k), band(v)
    s = jnp.einsum('bnqhgd,bnkhd->bnhgqk', qb, kk).astype(jnp.float32) * (D ** -0.5)
    s = s + bias.astype(jnp.float32)
    s = jnp.where(mask[None, :, None, None], s, -jnp.inf)
    sink = sinks.astype(jnp.float32).reshape(A_KV_HEADS, A_GROUP)[:, :, None, None]
    sink = jnp.broadcast_to(sink, s.shape[:-1] + (1,))
    p = jax.nn.softmax(jnp.concatenate([s, sink], axis=-1), axis=-1)[..., :-1]
    out = jnp.einsum('bnhgqk,bnkhd->bnqhgd', p.astype(v.dtype), vv)
    return out.reshape(B, S, HQ, D)


def rope_tables(seq):
    pos = jnp.arange(seq, dtype=jnp.float32)
    inv = ROPE_THETA ** (-jnp.arange(0, B_ROPE_DIM, 2, dtype=jnp.float32) / B_ROPE_DIM)
    ang = pos[:, None] * inv[None, :]
    return jnp.cos(ang), jnp.sin(ang)


def apply_rope(x, cos, sin):
    half = x.shape[-1] // 2
    x1, x2 = x[..., :half], x[..., half:]
    c, s = cos.astype(x.dtype), sin.astype(x.dtype)
    return jnp.concatenate([x1 * c - x2 * s, x2 * c + x1 * s], axis=-1)


def to_blocks(t, start, end):
    B = t.shape[0]
    n = (end - start) // BLOCK
    return t[:, start:end].reshape(B, n, BLOCK, *t.shape[2:]).transpose(1, 0, 2, 3, 4)


def from_blocks(t):
    n, B = t.shape[0], t.shape[1]
    return t.transpose(1, 0, 2, 3, 4).reshape(B, n * BLOCK, *t.shape[3:])


def mla_attention(q_nope, q_rope, k_nope, k_rope, v):
    S = q_nope.shape[1]
    scale = B_QK_DIM ** -0.5
    outs = []
    for start in range(0, S, SUPER_BLOCK):
        end = min(start + SUPER_BLOCK, S)
        kn, kr, vv = k_nope[:, :end], k_rope[:, :end], v[:, :end]
        key_pos = jnp.arange(end)
        n = (end - start) // BLOCK

        def one_block(args, kn=kn, kr=kr, vv=vv, key_pos=key_pos, start=start):
            i, qn_b, qr_b = args
            s = (jnp.einsum('bqhd,bkhd->bhqk', qn_b, kn)
                 + jnp.einsum('bqhd,bkd->bhqk', qr_b, kr)).astype(jnp.float32) * scale
            q_pos = start + i * BLOCK + jnp.arange(BLOCK)
            s = jnp.where(key_pos[None, :] <= q_pos[:, None], s, -jnp.inf)
            p = jax.nn.softmax(s, axis=-1)
            return jnp.einsum('bhqk,bkhd->bqhd', p.astype(vv.dtype), vv)

        out = lax.map(one_block, (jnp.arange(n), to_blocks(q_nope, start, end), to_blocks(q_rope, start, end)))
        outs.append(from_blocks(out))
    return jnp.concatenate(outs, axis=1)


def stick_breaking_attention(q, k, v):
    B, S, H, D = q.shape
    scale = D ** -0.5
    outs = []
    for start in range(0, S, SUPER_BLOCK):
        end = min(start + SUPER_BLOCK, S)
        kk, vv = k[:, :end], v[:, :end]
        key_pos = jnp.arange(end)
        n = (end - start) // BLOCK
        n_kb = end // BLOCK

        def one_block(args, kk=kk, vv=vv, key_pos=key_pos, start=start, n_kb=n_kb):
            i, q_blk = args
            z = jnp.einsum('bqhd,bkhd->bhqk', q_blk, kk).astype(jnp.float32) * scale
            q_pos = start + i * BLOCK + jnp.arange(BLOCK)
            past = key_pos[None, :] < q_pos[:, None]
            log_keep = jnp.where(past, jax.nn.log_sigmoid(-z), 0.0)
            lk = log_keep.reshape(z.shape[:-1] + (n_kb, BLOCK))
            within = lax.cumsum(lk, axis=lk.ndim - 1, reverse=True) - lk
            tot = jnp.sum(lk, axis=-1)
            offset = lax.cumsum(tot, axis=tot.ndim - 1, reverse=True) - tot
            log_between = (within + offset[..., None]).reshape(z.shape)
            w = jnp.where(past, jnp.exp(jax.nn.log_sigmoid(z) + log_between), 0.0)
            return jnp.einsum('bhqk,bkhd->bqhd', w.astype(vv.dtype), vv)

        out = lax.map(one_block, (jnp.arange(n), to_blocks(q, start, end)))
        outs.append(from_blocks(out))
    return jnp.concatenate(outs, axis=1)


def setup_inputs(seed: int = 0) -> dict:
    key = jax.random.key(seed)
    ks = jax.random.split(key, 16)
    f32 = jnp.float32
    nrm = lambda k, shape, scale: jax.random.normal(k, shape, f32) * scale
    gain = lambda k, shape: 1.0 + 0.02 * jax.random.normal(k, shape, f32)
    return {
        "x": nrm(ks[0], (BATCH, SEQ, D_MODEL), 1.0),
        "norm_g": gain(ks[1], (DEPTH, D_MODEL)),
        "w_in": nrm(ks[2], (DEPTH, D_MODEL, IN_COLS), D_MODEL ** -0.5),
        "attn_sinks": nrm(ks[3], (DEPTH, A_HEADS), 0.5),
        "rel_bias": nrm(ks[4], (NUM_BUCKETS, A_HEADS), 0.5),
        "g_q_lora": gain(ks[5], (DEPTH, B_Q_LORA)),
        "w_q_up": nrm(ks[6], (DEPTH, B_Q_LORA, B_HEADS * B_QK_DIM), B_Q_LORA ** -0.5),
        "g_kv_lora": gain(ks[7], (DEPTH, B_KV_LORA)),
        "w_kv_up": nrm(ks[8], (DEPTH, B_KV_LORA, B_HEADS * (B_NOPE_DIM + B_V_DIM)), B_KV_LORA ** -0.5),
        "w_proj_a": nrm(ks[9], (DEPTH, A_WIDTH, D_MODEL), A_WIDTH ** -0.5),
        "w_proj_b": nrm(ks[10], (DEPTH, B_WIDTH, D_MODEL), B_WIDTH ** -0.5),
        "w_proj_c": nrm(ks[11], (DEPTH, C_WIDTH, D_MODEL), C_WIDTH ** -0.5),
        "w_out": nrm(ks[12], (DEPTH, D_MODEL, D_MODEL), D_MODEL ** -0.5),
        "final_g": gain(ks[13], (D_MODEL,)),
    }


def reference(x, norm_g, w_in, attn_sinks, rel_bias, g_q_lora, w_q_up, g_kv_lora, w_kv_up,
              w_proj_a, w_proj_b, w_proj_c, w_out, final_g):
    B, S, _ = x.shape
    nb = S // BLOCK
    split_points = np.cumsum(IN_SIZES)[:-1].tolist()
    swa_bias, swa_mask = swa_bias_and_mask(rel_bias, nb)
    cos, sin = rope_tables(S)

    for l in range(DEPTH):
        h = rms_norm(x, norm_g[l])
        p = jnp.einsum('bsd,de->bse', h, w_in[l])
        (qa, ka, va, za, cq, ckv, kr, zb, qc, kc, vc, zc, ga, gb, gc) = jnp.split(p, split_points, axis=-1)

        ya = swa_sink_attention(qa.reshape(B, S, A_HEADS, HEAD_DIM),
                                ka.reshape(B, S, A_KV_HEADS, HEAD_DIM),
                                va.reshape(B, S, A_KV_HEADS, HEAD_DIM),
                                attn_sinks[l], swa_bias, swa_mask)
        ya = ya.reshape(B, S, A_WIDTH) * jax.nn.silu(za)

        q = (rms_norm(cq, g_q_lora[l]) @ w_q_up[l]).reshape(B, S, B_HEADS, B_QK_DIM)
        q_nope = q[..., :B_NOPE_DIM]
        q_rope = apply_rope(q[..., B_NOPE_DIM:], cos[:, None, :], sin[:, None, :])
        kv = (rms_norm(ckv, g_kv_lora[l]) @ w_kv_up[l]).reshape(B, S, B_HEADS, B_NOPE_DIM + B_V_DIM)
        k_nope, vb = kv[..., :B_NOPE_DIM], kv[..., B_NOPE_DIM:]
        k_rope = apply_rope(kr, cos, sin)
        yb = mla_attention(q_nope, q_rope, k_nope, k_rope, vb)
        yb = yb.reshape(B, S, B_WIDTH) * jax.nn.silu(zb)

        yc = stick_breaking_attention(qc.reshape(B, S, C_HEADS, HEAD_DIM),
                                      kc.reshape(B, S, C_HEADS, HEAD_DIM),
                                      vc.reshape(B, S, C_HEADS, HEAD_DIM))
        yc = yc.reshape(B, S, C_WIDTH) * jax.nn.silu(zc)

        merged = (jax.nn.sigmoid(ga) * (ya @ w_proj_a[l])
                  + jax.nn.sigmoid(gb) * (yb @ w_proj_b[l])
                  + jax.nn.sigmoid(gc) * (yc @ w_proj_c[l]))
        x = x + merged @ w_out[l]

    return rms_norm(x, final_g)
```

```python
import functools
import math

import jax
import jax.numpy as jnp
from jax import lax
from jax.experimental import pallas as pl
from jax.experimental.pallas import tpu as pltpu

D_MODEL = 4096
HEAD_DIM = 128
BLOCK = 128
EPS = 1e-6
A_HEADS = 8
A_KV_HEADS = 2
A_GROUP = A_HEADS // A_KV_HEADS
A_WIDTH = A_HEADS * HEAD_DIM
A_KV_WIDTH = A_KV_HEADS * HEAD_DIM
NUM_BUCKETS = 32
MAX_DISTANCE = 128
B_HEADS = 4
B_Q_LORA = 1024
B_KV_LORA = 512
B_NOPE = 128
B_ROPE = 64
B_V = 128
B_QK = B_NOPE + B_ROPE
B_QK_PAD = 256
B_WIDTH = B_HEADS * B_V
ROPE_THETA = 10000.0
C_HEADS = 4
C_WIDTH = C_HEADS * HEAD_DIM

MXU_DTYPE = jnp.bfloat16
VMEM_LIMIT_BYTES = 56 * 1024 * 1024

_IN_SIZES = (A_WIDTH, A_KV_WIDTH, A_KV_WIDTH, A_WIDTH, B_Q_LORA, B_KV_LORA, B_ROPE, B_WIDTH,
             C_WIDTH, C_WIDTH, C_WIDTH, C_WIDTH, D_MODEL, D_MODEL, D_MODEL)
_IN_NAMES = ("qa", "ka", "va", "za", "cq", "ckv", "kr", "zb", "qc", "kc", "vc", "zc", "ga", "gb", "gc")
_IN_OFF = {}
_o = 0
for _n, _s in zip(_IN_NAMES, _IN_SIZES):
    _IN_OFF[_n] = (_o, _s)
    _o += _s

_P16_COLS = (("qa", A_WIDTH), ("ka", A_KV_WIDTH), ("va", A_KV_WIDTH),
             ("qc", C_WIDTH), ("kc", C_WIDTH), ("vc", C_WIDTH))
_PF_COLS = (("za", A_WIDTH), ("cq", B_Q_LORA), ("ckv", B_KV_LORA), ("kr_a", 128), ("kr_b", 128),
            ("zb", B_WIDTH), ("zc", C_WIDTH))


def _offsets(cols):
    out, o = {}, 0
    for n, s in cols:
        out[n] = o
        o += s
    return out, o


_P16_OFF, P16_WIDTH = _offsets(_P16_COLS)
_PF_OFF, PF_WIDTH = _offsets(_PF_COLS)
GATE_OFF = P16_WIDTH + PF_WIDTH


def _cparams(*sem):
    return pltpu.CompilerParams(dimension_semantics=sem, vmem_limit_bytes=VMEM_LIMIT_BYTES)


def _dot(a, b):
    return jnp.dot(a, b, preferred_element_type=jnp.float32)


def _dot_nt(a, b):
    return lax.dot_general(a, b, (((1,), (1,)), ((), ())), preferred_element_type=jnp.float32)


def _silu(z):
    return z * jax.nn.sigmoid(z)


def _rmsnorm_kernel(x_ref, g_ref, o_ref):
    x = x_ref[...]
    y = x * lax.rsqrt(jnp.mean(x * x, axis=-1, keepdims=True) + EPS)
    o_ref[...] = (y * g_ref[...]).astype(o_ref.dtype)


def _rmsnorm(x, g, out_dtype, tm=256):
    s, d = x.shape
    return pl.pallas_call(
        _rmsnorm_kernel,
        grid=(s // tm,),
        in_specs=[pl.BlockSpec((tm, d), lambda i: (i, 0)),
                  pl.BlockSpec((1, d), lambda i: (0, 0))],
        out_specs=pl.BlockSpec((tm, d), lambda i: (i, 0)),
        out_shape=jax.ShapeDtypeStruct((s, d), out_dtype),
        compiler_params=_cparams("parallel"),
        name="rmsnorm",
    )(x, g.reshape(1, d))


def _matmul_kernel(a_ref, w_ref, o_ref):
    o_ref[...] = _dot(a_ref[...], w_ref[...]).astype(o_ref.dtype)


def _matmul_residual_kernel(a_ref, w_ref, r_ref, o_ref):
    o_ref[...] = r_ref[...] + _dot(a_ref[...], w_ref[...])


def _matmul(a, w, col_off, n_cols, out_dtype, tm, tn, residual=None, name="matmul"):
    s, k = a.shape
    assert s % tm == 0 and n_cols % tn == 0 and col_off % tn == 0
    jo = col_off // tn
    in_specs = [pl.BlockSpec((tm, k), lambda i, j: (i, 0)),
                pl.BlockSpec((k, tn), lambda i, j: (0, j + jo))]
    args = [a, w]
    body = _matmul_kernel
    if residual is not None:
        in_specs.append(pl.BlockSpec((tm, tn), lambda i, j: (i, j)))
        args.append(residual)
        body = _matmul_residual_kernel
    return pl.pallas_call(
        body,
        grid=(s // tm, n_cols // tn),
        in_specs=in_specs,
        out_specs=pl.BlockSpec((tm, tn), lambda i, j: (i, j)),
        out_shape=jax.ShapeDtypeStruct((s, n_cols), out_dtype),
        compiler_params=_cparams("parallel", "arbitrary"),
        name=name,
    )(*args)


def _merge_kernel(h_ref, wga_ref, wgb_ref, wgc_ref, ya_ref, yb_ref, yc_ref,
                  wa_ref, wb_ref, wc_ref, o_ref):
    h = h_ref[...]

    def branch(wg_ref, y_ref, w_ref):
        return jax.nn.sigmoid(_dot(h, wg_ref[...])) * _dot(y_ref[...], w_ref[...])

    merged = (branch(wga_ref, ya_ref, wa_ref) + branch(wgb_ref, yb_ref, wb_ref)
              + branch(wgc_ref, yc_ref, wc_ref))
    o_ref[...] = merged.astype(o_ref.dtype)


def _merge(h, w_all, ya, yb, yc, wa, wb, wc, tm=1024, tn=256):
    s = h.shape[0]
    nj = D_MODEL // tn
    g0 = GATE_OFF // tn

    def gate_spec(k):
        return pl.BlockSpec((D_MODEL, tn), lambda i, j: (0, g0 + k * nj + j))

    def row_spec(width):
        return pl.BlockSpec((tm, width), lambda i, j: (i, 0))

    def w_spec(width):
        return pl.BlockSpec((width, tn), lambda i, j: (0, j))

    return pl.pallas_call(
        _merge_kernel,
        grid=(s // tm, nj),
        in_specs=[row_spec(D_MODEL), gate_spec(0), gate_spec(1), gate_spec(2),
                  row_spec(A_WIDTH), row_spec(B_WIDTH), row_spec(C_WIDTH),
                  w_spec(A_WIDTH), w_spec(B_WIDTH), w_spec(C_WIDTH)],
        out_specs=pl.BlockSpec((tm, tn), lambda i, j: (i, j)),
        out_shape=jax.ShapeDtypeStruct((s, D_MODEL), MXU_DTYPE),
        compiler_params=_cparams("parallel", "arbitrary"),
        name="gated_merge",
    )(h, w_all, w_all, w_all, ya, yb, yc, wa, wb, wc)


def _t5_bucket(rel):
    n = jnp.maximum(rel, 0)
    max_exact = NUM_BUCKETS // 2
    logn = jnp.log(jnp.maximum(n, 1).astype(jnp.float32) / max_exact)
    large = max_exact + (logn / math.log(MAX_DISTANCE / max_exact)
                         * (NUM_BUCKETS - max_exact)).astype(jnp.int32)
    large = jnp.minimum(large, NUM_BUCKETS - 1)
    return jnp.where(n < max_exact, n, large)


def _swa_bias_kernel(rel_bias_ref, bucket_ref, o_ref):
    bucket = bucket_ref[...]
    for h in range(A_HEADS):
        acc = jnp.zeros(bucket.shape, jnp.float32)
        for b in range(NUM_BUCKETS):
            acc = jnp.where(bucket == b, rel_bias_ref[b, h], acc)
        o_ref[h] = acc


def _swa_bias(rel_bias):
    t = jnp.arange(BLOCK)[:, None]
    s = jnp.arange(2 * BLOCK)[None, :]
    bucket = _t5_bucket(BLOCK + t - s).astype(jnp.int32)
    return pl.pallas_call(
        _swa_bias_kernel,
        in_specs=[pl.BlockSpec(memory_space=pltpu.SMEM),
                  pl.BlockSpec((BLOCK, 2 * BLOCK), lambda: (0, 0))],
        out_specs=pl.BlockSpec((A_HEADS, BLOCK, 2 * BLOCK), lambda: (0, 0, 0)),
        out_shape=jax.ShapeDtypeStruct((A_HEADS, BLOCK, 2 * BLOCK), jnp.float32),
        name="swa_bias_table",
    )(rel_bias, bucket)


def _swa_kernel(sinks_ref, q_ref, kc_ref, kp_ref, vc_ref, vp_ref, bias_ref, z_ref, o_ref):
    n = pl.program_id(0)
    t = lax.broadcasted_iota(jnp.int32, (BLOCK, BLOCK), 0)
    s = lax.broadcasted_iota(jnp.int32, (BLOCK, BLOCK), 1)
    prev_ok = s > t + jnp.where(n > 0, 0, BLOCK)
    cur_ok = s <= t
    scale = HEAD_DIM ** -0.5
    neg_inf = jnp.float32(-jnp.inf)
    for hq in range(A_HEADS):
        hk = hq // A_GROUP
        q = q_ref[:, hq * HEAD_DIM:(hq + 1) * HEAD_DIM]
        kp = kp_ref[:, hk * HEAD_DIM:(hk + 1) * HEAD_DIM]
        kc = kc_ref[:, hk * HEAD_DIM:(hk + 1) * HEAD_DIM]
        vp = vp_ref[:, hk * HEAD_DIM:(hk + 1) * HEAD_DIM]
        vc = vc_ref[:, hk * HEAD_DIM:(hk + 1) * HEAD_DIM]
        sp = _dot_nt(q, kp) * scale + bias_ref[hq, :, 0:BLOCK]
        sc = _dot_nt(q, kc) * scale + bias_ref[hq, :, BLOCK:2 * BLOCK]
        sp = jnp.where(prev_ok, sp, neg_inf)
        sc = jnp.where(cur_ok, sc, neg_inf)
        sink = sinks_ref[hq]
        m = jnp.maximum(jnp.max(sp, axis=-1, keepdims=True), jnp.max(sc, axis=-1, keepdims=True))
        m = jnp.maximum(m, sink)
        ep = jnp.exp(sp - m)
        ec = jnp.exp(sc - m)
        denom = (jnp.sum(ep, axis=-1, keepdims=True) + jnp.sum(ec, axis=-1, keepdims=True)
                 + jnp.exp(sink - m))
        out = (_dot((ep / denom).astype(MXU_DTYPE), vp)
               + _dot((ec / denom).astype(MXU_DTYPE), vc))
        z = z_ref[:, hq * HEAD_DIM:(hq + 1) * HEAD_DIM]
        o_ref[:, hq * HEAD_DIM:(hq + 1) * HEAD_DIM] = (out * _silu(z)).astype(o_ref.dtype)


def _swa(p16, pf, sinks, bias):
    s = p16.shape[0]
    qb = _P16_OFF["qa"] // A_WIDTH
    kb = _P16_OFF["ka"] // A_KV_WIDTH
    vb = _P16_OFF["va"] // A_KV_WIDTH
    zb = _PF_OFF["za"] // A_WIDTH
    prev = lambda n: jnp.maximum(n - 1, 0)
    return pl.pallas_call(
        _swa_kernel,
        grid=(s // BLOCK,),
        in_specs=[pl.BlockSpec(memory_space=pltpu.SMEM),
                  pl.BlockSpec((BLOCK, A_WIDTH), lambda n: (n, qb)),
                  pl.BlockSpec((BLOCK, A_KV_WIDTH), lambda n: (n, kb)),
                  pl.BlockSpec((BLOCK, A_KV_WIDTH), lambda n: (prev(n), kb)),
                  pl.BlockSpec((BLOCK, A_KV_WIDTH), lambda n: (n, vb)),
                  pl.BlockSpec((BLOCK, A_KV_WIDTH), lambda n: (prev(n), vb)),
                  pl.BlockSpec((A_HEADS, BLOCK, 2 * BLOCK), lambda n: (0, 0, 0)),
                  pl.BlockSpec((BLOCK, A_WIDTH), lambda n: (n, zb))],
        out_specs=pl.BlockSpec((BLOCK, A_WIDTH), lambda n: (n, 0)),
        out_shape=jax.ShapeDtypeStruct((s, A_WIDTH), MXU_DTYPE),
        compiler_params=_cparams("parallel"),
        name="swa_attention",
    )(sinks, p16, p16, p16, p16, p16, bias, pf)


def _rms_cast(x, g):
    y = x * lax.rsqrt(jnp.mean(x * x, axis=-1, keepdims=True) + EPS)
    return (y * g).astype(MXU_DTYPE)


def _mla_pre_kernel(cq_ref, ckv_ref, kr_ref, gq_ref, gkv_ref, wq_ref, wkv_ref, rope_ref,
                    qf_ref, kf_ref, v_ref):
    tc = rope_ref[:, 0:128]
    ts = rope_ref[:, 128:256]
    qraw = _dot(_rms_cast(cq_ref[...], gq_ref[...]), wq_ref[...])
    for h in range(B_HEADS):
        b = 3 * 128 * h
        qf_ref[:, h * B_QK_PAD:h * B_QK_PAD + B_NOPE] = qraw[:, b:b + 128].astype(qf_ref.dtype)
        rot = qraw[:, b + 128:b + 256] * tc + qraw[:, b + 256:b + 384] * ts
        qf_ref[:, h * B_QK_PAD + B_NOPE:(h + 1) * B_QK_PAD] = rot.astype(qf_ref.dtype)
    kvraw = _dot(_rms_cast(ckv_ref[...], gkv_ref[...]), wkv_ref[...])
    kr = kr_ref[...]
    krot = (kr[:, 0:128] * tc + kr[:, 128:256] * ts).astype(kf_ref.dtype)
    for h in range(B_HEADS):
        kf_ref[:, h * B_QK_PAD:h * B_QK_PAD + B_NOPE] = (
            kvraw[:, h * B_NOPE:(h + 1) * B_NOPE].astype(kf_ref.dtype))
        kf_ref[:, h * B_QK_PAD + B_NOPE:(h + 1) * B_QK_PAD] = krot
    v_ref[...] = kvraw[:, B_HEADS * B_NOPE:].astype(v_ref.dtype)


def _mla_pre(pf, gq, gkv, wq, wkv, rope, tm=512):
    s = pf.shape[0]
    row = lambda width, off: pl.BlockSpec((tm, width), lambda i: (i, off // width))
    full = lambda a: pl.BlockSpec(a.shape, lambda i: (0, 0))
    gq = gq.reshape(1, B_Q_LORA)
    gkv = gkv.reshape(1, B_KV_LORA)
    out_row = lambda width: pl.BlockSpec((tm, width), lambda i: (i, 0))
    return pl.pallas_call(
        _mla_pre_kernel,
        grid=(s // tm,),
        in_specs=[row(B_Q_LORA, _PF_OFF["cq"]), row(B_KV_LORA, _PF_OFF["ckv"]),
                  row(256, _PF_OFF["kr_a"]), full(gq), full(gkv), full(wq), full(wkv),
                  pl.BlockSpec((tm, 256), lambda i: (i, 0))],
        out_specs=[out_row(B_HEADS * B_QK_PAD), out_row(B_HEADS * B_QK_PAD), out_row(B_WIDTH)],
        out_shape=[jax.ShapeDtypeStruct((s, B_HEADS * B_QK_PAD), MXU_DTYPE),
                   jax.ShapeDtypeStruct((s, B_HEADS * B_QK_PAD), MXU_DTYPE),
                   jax.ShapeDtypeStruct((s, B_WIDTH), MXU_DTYPE)],
        compiler_params=_cparams("parallel"),
        name="mla_pre",
    )(pf, pf, pf, gq, gkv, wq, wkv, rope)


def _mla_kernel(q_ref, k_ref, v_ref, z_ref, o_ref, *, tq, tk):
    i = pl.program_id(1)
    q = q_ref[...]
    scale = B_QK ** -0.5
    row = i * tq + lax.broadcasted_iota(jnp.int32, (tq, tk), 0)
    col = lax.broadcasted_iota(jnp.int32, (tq, tk), 1)

    def step(kb, carry, masked):
        m, l, acc = carry
        start = pl.multiple_of(kb * tk, tk)
        k = k_ref[pl.ds(start, tk), :]
        v = v_ref[pl.ds(start, tk), :]
        s = _dot_nt(q, k) * scale
        if masked:
            s = jnp.where(col + start <= row, s, -jnp.inf)
        m_new = jnp.maximum(m, jnp.max(s, axis=-1, keepdims=True))
        alpha = jnp.exp(m - m_new)
        p = jnp.exp(s - m_new)
        l = alpha * l + jnp.sum(p, axis=-1, keepdims=True)
        acc = alpha * acc + _dot(p.astype(MXU_DTYPE), v)
        return m_new, l, acc

    carry = (jnp.full((tq, 1), -jnp.inf, jnp.float32), jnp.zeros((tq, 1), jnp.float32),
             jnp.zeros((tq, B_V), jnp.float32))
    n_full = i * (tq // tk)
    carry = lax.fori_loop(0, n_full, lambda kb, c: step(kb, c, False), carry)
    for d in range(tq // tk):
        carry = step(n_full + d, carry, True)
    _, l, acc = carry
    o_ref[...] = ((acc / l) * _silu(z_ref[...])).astype(o_ref.dtype)


def _mla(qf, kf, v, pf, tq=512, tk=512):
    s = qf.shape[0]
    zb = _PF_OFF["zb"] // B_V
    return pl.pallas_call(
        functools.partial(_mla_kernel, tq=tq, tk=tk),
        grid=(B_HEADS, s // tq),
        in_specs=[pl.BlockSpec((tq, B_QK_PAD), lambda h, i: (i, h)),
                  pl.BlockSpec((s, B_QK_PAD), lambda h, i: (0, h)),
                  pl.BlockSpec((s, B_V), lambda h, i: (0, h)),
                  pl.BlockSpec((tq, B_V), lambda h, i: (i, zb + h))],
        out_specs=pl.BlockSpec((tq, B_V), lambda h, i: (i, h)),
        out_shape=jax.ShapeDtypeStruct((s, B_WIDTH), MXU_DTYPE),
        compiler_params=_cparams("parallel", "arbitrary"),
        name="mla_attention",
    )(qf, kf, v, pf)


def _sb_kernel(q_ref, k_ref, v_ref, z_ref, u_ref, o_ref, *, tq, tk):
    i = pl.program_id(1)
    q = q_ref[...]
    u = u_ref[...]
    scale = HEAD_DIM ** -0.5
    row = i * tq + lax.broadcasted_iota(jnp.int32, (tq, tk), 0)
    col = lax.broadcasted_iota(jnp.int32, (tq, tk), 1)

    def step(kb, carry, masked):
        off, acc = carry
        start = pl.multiple_of(kb * tk, tk)
        k = k_ref[pl.ds(start, tk), :]
        v = v_ref[pl.ds(start, tk), :]
        z = _dot_nt(q, k) * scale
        l1p = jnp.log1p(jnp.exp(-jnp.abs(z)))
        log_keep = -jnp.maximum(z, 0.0) - l1p
        log_beta = jnp.minimum(z, 0.0) - l1p
        if masked:
            past = col + start < row
            log_keep = jnp.where(past, log_keep, 0.0)
        hi = log_keep.astype(MXU_DTYPE)
        lo = (log_keep - hi.astype(jnp.float32)).astype(MXU_DTYPE)
        within = _dot(hi, u) + _dot(lo, u)
        w = jnp.exp(log_beta + (within + off))
        if masked:
            w = jnp.where(past, w, 0.0)
        acc = acc + _dot(w.astype(MXU_DTYPE), v)
        off = off + jnp.sum(log_keep, axis=-1, keepdims=True)
        return off, acc

    carry = (jnp.zeros((tq, 1), jnp.float32), jnp.zeros((tq, HEAD_DIM), jnp.float32))
    n_full = i * (tq // tk)
    for d in reversed(range(tq // tk)):
        carry = step(n_full + d, carry, True)
    carry = lax.fori_loop(0, n_full, lambda j, c: step(n_full - 1 - j, c, False), carry)
    _, acc = carry
    o_ref[...] = (acc * _silu(z_ref[...])).astype(o_ref.dtype)


def _sb(p16, pf, tq=512, tk=256):
    s = p16.shape[0]
    qb = _P16_OFF["qc"] // HEAD_DIM
    kb = _P16_OFF["kc"] // HEAD_DIM
    vb = _P16_OFF["vc"] // HEAD_DIM
    zb = _PF_OFF["zc"] // HEAD_DIM
    j = lax.broadcasted_iota(jnp.int32, (tk, tk), 0)
    c = lax.broadcasted_iota(jnp.int32, (tk, tk), 1)
    u = (j > c).astype(MXU_DTYPE)
    return pl.pallas_call(
        functools.partial(_sb_kernel, tq=tq, tk=tk),
        grid=(C_HEADS, s // tq),
        in_specs=[pl.BlockSpec((tq, HEAD_DIM), lambda h, i: (i, qb + h)),
                  pl.BlockSpec((s, HEAD_DIM), lambda h, i: (0, kb + h)),
                  pl.BlockSpec((s, HEAD_DIM), lambda h, i: (0, vb + h)),
                  pl.BlockSpec((tq, HEAD_DIM), lambda h, i: (i, zb + h)),
                  pl.BlockSpec((tk, tk), lambda h, i: (0, 0))],
        out_specs=pl.BlockSpec((tq, HEAD_DIM), lambda h, i: (i, h)),
        out_shape=jax.ShapeDtypeStruct((s, C_WIDTH), MXU_DTYPE),
        compiler_params=_cparams("parallel", "arbitrary"),
        name="stick_breaking_attention",
    )(p16, p16, p16, pf, u)


def _swap_halves(w):
    half = w.shape[-1] // 2
    return jnp.concatenate([w[..., half:], w[..., :half]], axis=-1)


def _pad_cols(w, width):
    return jnp.pad(w, ((0, 0), (0, width - w.shape[-1])))


def _layout_w_in(w_in):
    col = lambda n: w_in[:, _IN_OFF[n][0]:_IN_OFF[n][0] + _IN_OFF[n][1]]
    kr = col("kr")
    parts = [col(n) for n, _ in _P16_COLS]
    parts += [col("za"), col("cq"), col("ckv"), _pad_cols(kr, 128), _pad_cols(_swap_halves(kr), 128),
              col("zb"), col("zc")]
    parts += [col("ga"), col("gb"), col("gc")]
    return jnp.concatenate(parts, axis=1).astype(MXU_DTYPE)


def _layout_w_q_up(w):
    parts = []
    for h in range(B_HEADS):
        nope = w[:, h * B_QK:h * B_QK + B_NOPE]
        rope = w[:, h * B_QK + B_NOPE:(h + 1) * B_QK]
        parts += [nope, _pad_cols(rope, 128), _pad_cols(_swap_halves(rope), 128)]
    return jnp.concatenate(parts, axis=1).astype(MXU_DTYPE)


def _layout_w_kv_up(w):
    per = B_NOPE + B_V
    ks = [w[:, h * per:h * per + B_NOPE] for h in range(B_HEADS)]
    vs = [w[:, h * per + B_NOPE:(h + 1) * per] for h in range(B_HEADS)]
    return jnp.concatenate(ks + vs, axis=1).astype(MXU_DTYPE)


def _rope_table(seq):
    pos = jnp.arange(seq, dtype=jnp.float32)
    inv = ROPE_THETA ** (-jnp.arange(0, B_ROPE, 2, dtype=jnp.float32) / B_ROPE)
    ang = pos[:, None] * inv[None, :]
    cos, sin = jnp.cos(ang), jnp.sin(ang)
    zeros = jnp.zeros((seq, 128 - B_ROPE), jnp.float32)
    return jnp.concatenate([cos, cos, zeros, -sin, sin, zeros], axis=1)


def kernel(x, norm_g, w_in, attn_sinks, rel_bias, g_q_lora, w_q_up, g_kv_lora, w_kv_up,
           w_proj_a, w_proj_b, w_proj_c, w_out, final_g):
    batch, seq, d = x.shape
    assert batch == 1 and d == D_MODEL and seq % 1024 == 0
    depth = w_in.shape[0]
    xs = x.reshape(seq, d)
    bias = _swa_bias(rel_bias)
    rope = _rope_table(seq)

    for l in range(depth):
        w_all = _layout_w_in(w_in[l])
        h = _rmsnorm(xs, norm_g[l], MXU_DTYPE)
        p16 = _matmul(h, w_all, 0, P16_WIDTH, MXU_DTYPE, tm=1024, tn=768, name="in_proj_bf16")
        pf = _matmul(h, w_all, P16_WIDTH, PF_WIDTH, jnp.float32, tm=1024, tn=768, name="in_proj_f32")

        ya = _swa(p16, pf, attn_sinks[l], bias)
        qf, kf, vb = _mla_pre(pf, g_q_lora[l], g_kv_lora[l], _layout_w_q_up(w_q_up[l]),
                              _layout_w_kv_up(w_kv_up[l]), rope)
        yb = _mla(qf, kf, vb, pf)
        yc = _sb(p16, pf)

        merged = _merge(h, w_all, ya, yb, yc, w_proj_a[l].astype(MXU_DTYPE),
                        w_proj_b[l].astype(MXU_DTYPE), w_proj_c[l].astype(MXU_DTYPE))
        xs = _matmul(merged, w_out[l].astype(MXU_DTYPE), 0, D_MODEL, jnp.float32,
                     tm=1024, tn=512, residual=xs, name="out_proj")

    out = _rmsnorm(xs, final_g, jnp.float32)
    return out.reshape(batch, seq, d)
```

```python
import functools
import math

import jax
import jax.numpy as jnp
from jax import lax
from jax.experimental import pallas as pl
from jax.experimental.pallas import tpu as pltpu

D_MODEL = 4096
HEAD_DIM = 128
LANES = 128
BLOCK = 128
EPS = 1e-6
A_HEADS = 8
A_KV_HEADS = 2
A_GROUP = A_HEADS // A_KV_HEADS
A_WIDTH = A_HEADS * HEAD_DIM
A_KV_WIDTH = A_KV_HEADS * HEAD_DIM
NUM_BUCKETS = 32
MAX_DISTANCE = 128
B_HEADS = 4
B_Q_LORA = 1024
B_KV_LORA = 512
B_NOPE = 128
B_ROPE = 64
B_V = 128
B_QK = B_NOPE + B_ROPE
B_QK_PAD = 256
B_WIDTH = B_HEADS * B_V
ROPE_THETA = 10000.0
C_HEADS = 4
C_WIDTH = C_HEADS * HEAD_DIM

MXU_DTYPE = jnp.bfloat16
LOG2E = math.log2(math.e)
VMEM_LIMIT_BYTES = 56 * 1024 * 1024

_IN_SIZES = (A_WIDTH, A_KV_WIDTH, A_KV_WIDTH, A_WIDTH, B_Q_LORA, B_KV_LORA, B_ROPE, B_WIDTH,
             C_WIDTH, C_WIDTH, C_WIDTH, C_WIDTH, D_MODEL, D_MODEL, D_MODEL)
_IN_NAMES = ("qa", "ka", "va", "za", "cq", "ckv", "kr", "zb", "qc", "kc", "vc", "zc", "ga", "gb", "gc")
_IN_OFF = {}
_o = 0
for _n, _s in zip(_IN_NAMES, _IN_SIZES):
    _IN_OFF[_n] = (_o, _s)
    _o += _s

_P16_COLS = (("qa", A_WIDTH), ("ka", A_KV_WIDTH), ("va", A_KV_WIDTH),
             ("qc", C_WIDTH), ("kc", C_WIDTH), ("vc", C_WIDTH))
_PF_COLS = (("za", A_WIDTH), ("cq", B_Q_LORA), ("ckv", B_KV_LORA), ("kr_a", 128), ("kr_b", 128),
            ("zb", B_WIDTH), ("zc", C_WIDTH))


def _offsets(cols):
    out, o = {}, 0
    for n, s in cols:
        out[n] = o
        o += s
    return out, o


_P16_OFF, P16_WIDTH = _offsets(_P16_COLS)
_PF_OFF, PF_WIDTH = _offsets(_PF_COLS)
GATE_OFF = P16_WIDTH + PF_WIDTH


def _cparams(*sem):
    return pltpu.CompilerParams(dimension_semantics=sem, vmem_limit_bytes=VMEM_LIMIT_BYTES)


def _dot(a, b):
    return jnp.dot(a, b, preferred_element_type=jnp.float32)


def _dot_nt(a, b):
    return lax.dot_general(a, b, (((1,), (1,)), ((), ())), preferred_element_type=jnp.float32)


def _silu(z):
    return z * jax.nn.sigmoid(z)


def _rmsnorm_kernel(x_ref, g_ref, o_ref):
    x = x_ref[...]
    y = x * lax.rsqrt(jnp.mean(x * x, axis=-1, keepdims=True) + EPS)
    o_ref[...] = (y * g_ref[...]).astype(o_ref.dtype)


def _rmsnorm(x, g, out_dtype, tm=256):
    s, d = x.shape
    return pl.pallas_call(
        _rmsnorm_kernel,
        grid=(s // tm,),
        in_specs=[pl.BlockSpec((tm, d), lambda i: (i, 0)),
                  pl.BlockSpec((1, d), lambda i: (0, 0))],
        out_specs=pl.BlockSpec((tm, d), lambda i: (i, 0)),
        out_shape=jax.ShapeDtypeStruct((s, d), out_dtype),
        compiler_params=_cparams("parallel"),
        name="rmsnorm",
    )(x, g.reshape(1, d))


def _matmul_kernel(a_ref, w_ref, o_ref):
    o_ref[...] = _dot(a_ref[...], w_ref[...]).astype(o_ref.dtype)


def _matmul_residual_kernel(a_ref, w_ref, r_ref, o_ref):
    o_ref[...] = r_ref[...] + _dot(a_ref[...], w_ref[...])


def _matmul(a, w, col_off, n_cols, out_dtype, tm, tn, residual=None, name="matmul"):
    s, k = a.shape
    assert s % tm == 0 and n_cols % tn == 0 and col_off % tn == 0
    jo = col_off // tn
    in_specs = [pl.BlockSpec((tm, k), lambda i, j: (i, 0)),
                pl.BlockSpec((k, tn), lambda i, j: (0, j + jo))]
    args = [a, w]
    body = _matmul_kernel
    if residual is not None:
        in_specs.append(pl.BlockSpec((tm, tn), lambda i, j: (i, j)))
        args.append(residual)
        body = _matmul_residual_kernel
    return pl.pallas_call(
        body,
        grid=(s // tm, n_cols // tn),
        in_specs=in_specs,
        out_specs=pl.BlockSpec((tm, tn), lambda i, j: (i, j)),
        out_shape=jax.ShapeDtypeStruct((s, n_cols), out_dtype),
        compiler_params=_cparams("parallel", "arbitrary"),
        name=name,
    )(*args)


def _merge_kernel(h_ref, wga_ref, wgb_ref, wgc_ref, ya_ref, yb_ref, yc_ref,
                  wa_ref, wb_ref, wc_ref, o_ref):
    h = h_ref[...]

    def branch(wg_ref, y_ref, w_ref):
        return jax.nn.sigmoid(_dot(h, wg_ref[...])) * _dot(y_ref[...], w_ref[...])

    merged = (branch(wga_ref, ya_ref, wa_ref) + branch(wgb_ref, yb_ref, wb_ref)
              + branch(wgc_ref, yc_ref, wc_ref))
    o_ref[...] = merged.astype(o_ref.dtype)


def _merge(h, w_all, ya, yb, yc, wa, wb, wc, tm=1024, tn=256):
    s = h.shape[0]
    nj = D_MODEL // tn
    g0 = GATE_OFF // tn

    def gate_spec(k):
        return pl.BlockSpec((D_MODEL, tn), lambda i, j: (0, g0 + k * nj + j))

    def row_spec(width):
        return pl.BlockSpec((tm, width), lambda i, j: (i, 0))

    def w_spec(width):
        return pl.BlockSpec((width, tn), lambda i, j: (0, j))

    return pl.pallas_call(
        _merge_kernel,
        grid=(s // tm, nj),
        in_specs=[row_spec(D_MODEL), gate_spec(0), gate_spec(1), gate_spec(2),
                  row_spec(A_WIDTH), row_spec(B_WIDTH), row_spec(C_WIDTH),
                  w_spec(A_WIDTH), w_spec(B_WIDTH), w_spec(C_WIDTH)],
        out_specs=pl.BlockSpec((tm, tn), lambda i, j: (i, j)),
        out_shape=jax.ShapeDtypeStruct((s, D_MODEL), MXU_DTYPE),
        compiler_params=_cparams("parallel", "arbitrary"),
        name="gated_merge",
    )(h, w_all, w_all, w_all, ya, yb, yc, wa, wb, wc)


def _t5_bucket(rel):
    n = jnp.maximum(rel, 0)
    max_exact = NUM_BUCKETS // 2
    logn = jnp.log(jnp.maximum(n, 1).astype(jnp.float32) / max_exact)
    large = max_exact + (logn / math.log(MAX_DISTANCE / max_exact)
                         * (NUM_BUCKETS - max_exact)).astype(jnp.int32)
    large = jnp.minimum(large, NUM_BUCKETS - 1)
    return jnp.where(n < max_exact, n, large)


def _swa_bias_kernel(rel_bias_ref, bucket_ref, o_ref):
    bucket = bucket_ref[...]
    for h in range(A_HEADS):
        acc = jnp.zeros(bucket.shape, jnp.float32)
        for b in range(NUM_BUCKETS):
            acc = jnp.where(bucket == b, rel_bias_ref[b, h], acc)
        o_ref[h] = acc


def _swa_bias(rel_bias):
    t = jnp.arange(BLOCK)[:, None]
    s = jnp.arange(2 * BLOCK)[None, :]
    bucket = _t5_bucket(BLOCK + t - s).astype(jnp.int32)
    return pl.pallas_call(
        _swa_bias_kernel,
        in_specs=[pl.BlockSpec(memory_space=pltpu.SMEM),
                  pl.BlockSpec((BLOCK, 2 * BLOCK), lambda: (0, 0))],
        out_specs=pl.BlockSpec((A_HEADS, BLOCK, 2 * BLOCK), lambda: (0, 0, 0)),
        out_shape=jax.ShapeDtypeStruct((A_HEADS, BLOCK, 2 * BLOCK), jnp.float32),
        name="swa_bias_table",
    )(rel_bias, bucket)


def _swa_kernel(sinks_ref, q_ref, kc_ref, kp_ref, vc_ref, vp_ref, bias_ref, z_ref, o_ref):
    n = pl.program_id(0)
    t = lax.broadcasted_iota(jnp.int32, (BLOCK, BLOCK), 0)
    s = lax.broadcasted_iota(jnp.int32, (BLOCK, BLOCK), 1)
    prev_ok = s > t + jnp.where(n > 0, 0, BLOCK)
    cur_ok = s <= t
    scale = HEAD_DIM ** -0.5
    neg_inf = jnp.float32(-jnp.inf)
    for hq in range(A_HEADS):
        hk = hq // A_GROUP
        q = q_ref[:, hq * HEAD_DIM:(hq + 1) * HEAD_DIM]
        kp = kp_ref[:, hk * HEAD_DIM:(hk + 1) * HEAD_DIM]
        kc = kc_ref[:, hk * HEAD_DIM:(hk + 1) * HEAD_DIM]
        vp = vp_ref[:, hk * HEAD_DIM:(hk + 1) * HEAD_DIM]
        vc = vc_ref[:, hk * HEAD_DIM:(hk + 1) * HEAD_DIM]
        sp = _dot_nt(q, kp) * scale + bias_ref[hq, :, 0:BLOCK]
        sc = _dot_nt(q, kc) * scale + bias_ref[hq, :, BLOCK:2 * BLOCK]
        sp = jnp.where(prev_ok, sp, neg_inf)
        sc = jnp.where(cur_ok, sc, neg_inf)
        sink = sinks_ref[hq]
        m = jnp.maximum(jnp.max(sp, axis=-1, keepdims=True), jnp.max(sc, axis=-1, keepdims=True))
        m = jnp.maximum(m, sink)
        ep = jnp.exp(sp - m)
        ec = jnp.exp(sc - m)
        denom = (jnp.sum(ep, axis=-1, keepdims=True) + jnp.sum(ec, axis=-1, keepdims=True)
                 + jnp.exp(sink - m))
        out = (_dot((ep / denom).astype(MXU_DTYPE), vp)
               + _dot((ec / denom).astype(MXU_DTYPE), vc))
        z = z_ref[:, hq * HEAD_DIM:(hq + 1) * HEAD_DIM]
        o_ref[:, hq * HEAD_DIM:(hq + 1) * HEAD_DIM] = (out * _silu(z)).astype(o_ref.dtype)


def _swa(p16, pf, sinks, bias):
    s = p16.shape[0]
    qb = _P16_OFF["qa"] // A_WIDTH
    kb = _P16_OFF["ka"] // A_KV_WIDTH
    vb = _P16_OFF["va"] // A_KV_WIDTH
    zb = _PF_OFF["za"] // A_WIDTH
    prev = lambda n: jnp.maximum(n - 1, 0)
    return pl.pallas_call(
        _swa_kernel,
        grid=(s // BLOCK,),
        in_specs=[pl.BlockSpec(memory_space=pltpu.SMEM),
                  pl.BlockSpec((BLOCK, A_WIDTH), lambda n: (n, qb)),
                  pl.BlockSpec((BLOCK, A_KV_WIDTH), lambda n: (n, kb)),
                  pl.BlockSpec((BLOCK, A_KV_WIDTH), lambda n: (prev(n), kb)),
                  pl.BlockSpec((BLOCK, A_KV_WIDTH), lambda n: (n, vb)),
                  pl.BlockSpec((BLOCK, A_KV_WIDTH), lambda n: (prev(n), vb)),
                  pl.BlockSpec((A_HEADS, BLOCK, 2 * BLOCK), lambda n: (0, 0, 0)),
                  pl.BlockSpec((BLOCK, A_WIDTH), lambda n: (n, zb))],
        out_specs=pl.BlockSpec((BLOCK, A_WIDTH), lambda n: (n, 0)),
        out_shape=jax.ShapeDtypeStruct((s, A_WIDTH), MXU_DTYPE),
        compiler_params=_cparams("parallel"),
        name="swa_attention",
    )(sinks, p16, p16, p16, p16, p16, bias, pf)


def _rms_cast(x, g):
    y = x * lax.rsqrt(jnp.mean(x * x, axis=-1, keepdims=True) + EPS)
    return (y * g).astype(MXU_DTYPE)


def _mla_pre_kernel(cq_ref, ckv_ref, kr_ref, gq_ref, gkv_ref, wq_ref, wkv_ref, rope_ref,
                    qf_ref, kf_ref, v_ref):
    tc = rope_ref[:, 0:128]
    ts = rope_ref[:, 128:256]
    qraw = _dot(_rms_cast(cq_ref[...], gq_ref[...]), wq_ref[...])
    for h in range(B_HEADS):
        b = 3 * 128 * h
        qf_ref[:, h * B_QK_PAD:h * B_QK_PAD + B_NOPE] = qraw[:, b:b + 128].astype(qf_ref.dtype)
        rot = qraw[:, b + 128:b + 256] * tc + qraw[:, b + 256:b + 384] * ts
        qf_ref[:, h * B_QK_PAD + B_NOPE:(h + 1) * B_QK_PAD] = rot.astype(qf_ref.dtype)
    kvraw = _dot(_rms_cast(ckv_ref[...], gkv_ref[...]), wkv_ref[...])
    kr = kr_ref[...]
    krot = (kr[:, 0:128] * tc + kr[:, 128:256] * ts).astype(kf_ref.dtype)
    for h in range(B_HEADS):
        kf_ref[:, h * B_QK_PAD:h * B_QK_PAD + B_NOPE] = (
            kvraw[:, h * B_NOPE:(h + 1) * B_NOPE].astype(kf_ref.dtype))
        kf_ref[:, h * B_QK_PAD + B_NOPE:(h + 1) * B_QK_PAD] = krot
    v_ref[...] = kvraw[:, B_HEADS * B_NOPE:].astype(v_ref.dtype)


def _mla_pre(pf, gq, gkv, wq, wkv, rope, tm=512):
    s = pf.shape[0]
    row = lambda width, off: pl.BlockSpec((tm, width), lambda i: (i, off // width))
    full = lambda a: pl.BlockSpec(a.shape, lambda i: (0, 0))
    gq = gq.reshape(1, B_Q_LORA)
    gkv = gkv.reshape(1, B_KV_LORA)
    out_row = lambda width: pl.BlockSpec((tm, width), lambda i: (i, 0))
    return pl.pallas_call(
        _mla_pre_kernel,
        grid=(s // tm,),
        in_specs=[row(B_Q_LORA, _PF_OFF["cq"]), row(B_KV_LORA, _PF_OFF["ckv"]),
                  row(256, _PF_OFF["kr_a"]), full(gq), full(gkv), full(wq), full(wkv),
                  pl.BlockSpec((tm, 256), lambda i: (i, 0))],
        out_specs=[out_row(B_HEADS * B_QK_PAD), out_row(B_HEADS * B_QK_PAD), out_row(B_WIDTH)],
        out_shape=[jax.ShapeDtypeStruct((s, B_HEADS * B_QK_PAD), MXU_DTYPE),
                   jax.ShapeDtypeStruct((s, B_HEADS * B_QK_PAD), MXU_DTYPE),
                   jax.ShapeDtypeStruct((s, B_WIDTH), MXU_DTYPE)],
        compiler_params=_cparams("parallel"),
        name="mla_pre",
    )(pf, pf, pf, gq, gkv, wq, wkv, rope)


def _mla_kernel(q_ref, k_ref, v_ref, z_ref, o_ref,
                y_scr, p_scr, alpha_scr, m_scr, l_scr, acc_scr, *, t):
    i = pl.program_id(1)
    n = i + 1
    q = q_ref[...]
    c = (B_QK ** -0.5) * LOG2E

    def key_rows(pos):
        kb = jnp.where(pos == 0, i, pos - 1)
        return pl.ds(pl.multiple_of(kb * t, t), t)

    def stage1(pos, slot):
        y_scr[slot] = _dot_nt(q, k_ref[key_rows(pos), :]) * c

    def stage2(slot, diagonal=False):
        y = y_scr[slot]
        if diagonal:
            row = lax.broadcasted_iota(jnp.int32, (t, t), 0)
            col = lax.broadcasted_iota(jnp.int32, (t, t), 1)
            y = jnp.where(col <= row, y, -jnp.inf)
        m = m_scr[...]
        m_new = jnp.maximum(m, jnp.max(y, axis=-1, keepdims=True))
        alpha = jnp.exp2(m - m_new)
        p = jnp.exp2(y - jnp.concatenate([m_new] * (t // LANES), axis=-1))
        l_scr[...] = alpha * l_scr[...] + jnp.sum(p, axis=-1, keepdims=True)
        m_scr[...] = m_new
        alpha_scr[slot] = alpha
        p_scr[slot] = p.astype(MXU_DTYPE)

    def stage3(pos, slot):
        acc_scr[...] = alpha_scr[slot] * acc_scr[...] + _dot(p_scr[slot], v_ref[key_rows(pos), :])

    m_scr[...] = jnp.full((t, LANES), -jnp.inf, jnp.float32)
    l_scr[...] = jnp.zeros((t, LANES), jnp.float32)
    acc_scr[...] = jnp.zeros((t, B_V), jnp.float32)

    stage1(0, 0)

    @pl.when(n > 1)
    def _():
        stage1(1, 1)

    stage2(0, diagonal=True)

    def steady(tt, s_even, s_odd):
        stage3(tt - 2, s_even)
        stage2(s_odd)
        stage1(tt, s_even)

    def body(u, carry):
        tt = 2 + 2 * u
        steady(tt, 0, 1)
        steady(tt + 1, 1, 0)
        return carry

    pairs = jnp.maximum(n - 2, 0) // 2
    lax.fori_loop(0, pairs, body, 0)
    t_next = 2 + 2 * pairs

    @pl.when(t_next < n)
    def _():
        steady(t_next, 0, 1)

    @pl.when(n >= 2)
    def _():
        stage3(n - 2, n % 2)
        stage2((n - 1) % 2)

    stage3(n - 1, (n - 1) % 2)
    o_ref[...] = ((acc_scr[...] / l_scr[...]) * _silu(z_ref[...])).astype(o_ref.dtype)


def _mla(qf, kf, v, pf, t=512):
    s = qf.shape[0]
    zb = _PF_OFF["zb"] // B_V
    return pl.pallas_call(
        functools.partial(_mla_kernel, t=t),
        grid=(B_HEADS, s // t),
        in_specs=[pl.BlockSpec((t, B_QK_PAD), lambda h, i: (i, h)),
                  pl.BlockSpec((s, B_QK_PAD), lambda h, i: (0, h)),
                  pl.BlockSpec((s, B_V), lambda h, i: (0, h)),
                  pl.BlockSpec((t, B_V), lambda h, i: (i, zb + h))],
        out_specs=pl.BlockSpec((t, B_V), lambda h, i: (i, h)),
        out_shape=jax.ShapeDtypeStruct((s, B_WIDTH), MXU_DTYPE),
        scratch_shapes=[pltpu.VMEM((2, t, t), jnp.float32),
                        pltpu.VMEM((2, t, t), MXU_DTYPE),
                        pltpu.VMEM((2, t, LANES), jnp.float32),
                        pltpu.VMEM((t, LANES), jnp.float32),
                        pltpu.VMEM((t, LANES), jnp.float32),
                        pltpu.VMEM((t, B_V), jnp.float32)],
        compiler_params=_cparams("parallel", "arbitrary"),
        name="mla_attention",
    )(qf, kf, v, pf)


def _sb_kernel(q_ref, k_ref, v_ref, z_ref, uu_ref, o_ref,
               s_scr, hl_scr, lb_scr, off_scr, off_run_scr, w_scr, acc_scr, *, tq, tk):
    assert tq == 2 * tk
    i = pl.program_id(1)
    q = q_ref[...]
    scale = HEAD_DIM ** -0.5
    n = 2 * i + 2

    def key_rows(pos):
        return pl.ds(pl.multiple_of((n - 1 - pos) * tk, tk), tk)

    def stage1(pos, slot):
        s_scr[slot] = _dot_nt(q, k_ref[key_rows(pos), :])

    def stage2(slot, d=None):
        y = s_scr[slot] * (scale * LOG2E)
        l1p = jnp.log2(1.0 + jnp.exp2(-jnp.abs(y)))
        drop = jnp.maximum(y, 0.0) + l1p
        log_beta = y - drop
        if d is not None:
            row = lax.broadcasted_iota(jnp.int32, (tq, tk), 0)
            col = lax.broadcasted_iota(jnp.int32, (tq, tk), 1) + d * tk
            past = col < row
            drop = jnp.where(past, drop, 0.0)
            log_beta = jnp.where(past, log_beta, -jnp.inf)
        hi = drop.astype(MXU_DTYPE)
        lo = (drop - hi.astype(jnp.float32)).astype(MXU_DTYPE)
        hl_scr[slot] = jnp.concatenate([hi, lo], axis=-1)
        lb_scr[slot] = log_beta
        off = off_run_scr[...]
        off_scr[slot] = off
        off_run_scr[...] = off + jnp.sum(drop, axis=-1, keepdims=True)

    def stage3(slot):
        within = _dot(hl_scr[slot], uu_ref[...])
        off = jnp.concatenate([off_scr[slot]] * (tk // LANES), axis=-1)
        w_scr[slot] = jnp.exp2(lb_scr[slot] - (within + off)).astype(MXU_DTYPE)

    def stage4(pos, slot):
        acc_scr[...] += _dot(w_scr[slot], v_ref[key_rows(pos), :])

    off_run_scr[...] = jnp.zeros((tq, LANES), jnp.float32)
    acc_scr[...] = jnp.zeros((tq, HEAD_DIM), jnp.float32)

    stage1(0, 0)

    stage2(0, d=1)
    stage1(1, 1)

    stage3(0)
    stage2(1, d=0)

    @pl.when(n > 2)
    def _():
        stage1(2, 0)

    def steady(tt, s_t):
        stage4(tt - 3, 1 - s_t)
        stage3(s_t)
        stage2(1 - s_t)
        stage1(tt, s_t)

    def body(u, carry):
        tt = 3 + 2 * u
        steady(tt, 1)
        steady(tt + 1, 0)
        return carry

    pairs = (n - 3) // 2
    lax.fori_loop(0, pairs, body, 0)
    t_next = 3 + 2 * jnp.maximum(pairs, 0)

    @pl.when(t_next < n)
    def _():
        steady(t_next, 1)

    @pl.when(n >= 3)
    def _():
        stage4(n - 3, (n - 3) % 2)
        stage3(n % 2)
        stage2((n - 1) % 2)

    stage4(n - 2, n % 2)
    stage3((n - 1) % 2)
    stage4(n - 1, (n - 1) % 2)
    o_ref[...] = (acc_scr[...] * _silu(z_ref[...])).astype(o_ref.dtype)


def _sb(p16, pf, tq=512, tk=256):
    s = p16.shape[0]
    qb = _P16_OFF["qc"] // HEAD_DIM
    kb = _P16_OFF["kc"] // HEAD_DIM
    vb = _P16_OFF["vc"] // HEAD_DIM
    zb = _PF_OFF["zc"] // HEAD_DIM
    j = lax.broadcasted_iota(jnp.int32, (tk, tk), 0)
    c = lax.broadcasted_iota(jnp.int32, (tk, tk), 1)
    u = (j > c).astype(MXU_DTYPE)
    uu = jnp.concatenate([u, u], axis=0)
    return pl.pallas_call(
        functools.partial(_sb_kernel, tq=tq, tk=tk),
        grid=(C_HEADS, s // tq),
        in_specs=[pl.BlockSpec((tq, HEAD_DIM), lambda h, i: (i, qb + h)),
                  pl.BlockSpec((s, HEAD_DIM), lambda h, i: (0, kb + h)),
                  pl.BlockSpec((s, HEAD_DIM), lambda h, i: (0, vb + h)),
                  pl.BlockSpec((tq, HEAD_DIM), lambda h, i: (i, zb + h)),
                  pl.BlockSpec((2 * tk, tk), lambda h, i: (0, 0))],
        out_specs=pl.BlockSpec((tq, HEAD_DIM), lambda h, i: (i, h)),
        out_shape=jax.ShapeDtypeStruct((s, C_WIDTH), MXU_DTYPE),
        scratch_shapes=[pltpu.VMEM((2, tq, tk), jnp.float32),
                        pltpu.VMEM((2, tq, 2 * tk), MXU_DTYPE),
                        pltpu.VMEM((2, tq, tk), jnp.float32),
                        pltpu.VMEM((2, tq, LANES), jnp.float32),
                        pltpu.VMEM((tq, LANES), jnp.float32),
                        pltpu.VMEM((2, tq, tk), MXU_DTYPE),
                        pltpu.VMEM((tq, HEAD_DIM), jnp.float32)],
        compiler_params=_cparams("parallel", "arbitrary"),
        name="stick_breaking_attention",
    )(p16, p16, p16, pf, uu)


def _swap_halves(w):
    half = w.shape[-1] // 2
    return jnp.concatenate([w[..., half:], w[..., :half]], axis=-1)


def _pad_cols(w, width):
    return jnp.pad(w, ((0, 0), (0, width - w.shape[-1])))


def _layout_w_in(w_in):
    col = lambda n: w_in[:, _IN_OFF[n][0]:_IN_OFF[n][0] + _IN_OFF[n][1]]
    kr = col("kr")
    parts = [col(n) for n, _ in _P16_COLS]
    parts += [col("za"), col("cq"), col("ckv"), _pad_cols(kr, 128), _pad_cols(_swap_halves(kr), 128),
              col("zb"), col("zc")]
    parts += [col("ga"), col("gb"), col("gc")]
    return jnp.concatenate(parts, axis=1).astype(MXU_DTYPE)


def _layout_w_q_up(w):
    parts = []
    for h in range(B_HEADS):
        nope = w[:, h * B_QK:h * B_QK + B_NOPE]
        rope = w[:, h * B_QK + B_NOPE:(h + 1) * B_QK]
        parts += [nope, _pad_cols(rope, 128), _pad_cols(_swap_halves(rope), 128)]
    return jnp.concatenate(parts, axis=1).astype(MXU_DTYPE)


def _layout_w_kv_up(w):
    per = B_NOPE + B_V
    ks = [w[:, h * per:h * per + B_NOPE] for h in range(B_HEADS)]
    vs = [w[:, h * per + B_NOPE:(h + 1) * per] for h in range(B_HEADS)]
    return jnp.concatenate(ks + vs, axis=1).astype(MXU_DTYPE)


def _rope_table(seq):
    pos = jnp.arange(seq, dtype=jnp.float32)
    inv = ROPE_THETA ** (-jnp.arange(0, B_ROPE, 2, dtype=jnp.float32) / B_ROPE)
    ang = pos[:, None] * inv[None, :]
    cos, sin = jnp.cos(ang), jnp.sin(ang)
    zeros = jnp.zeros((seq, 128 - B_ROPE), jnp.float32)
    return jnp.concatenate([cos, cos, zeros, -sin, sin, zeros], axis=1)


def kernel(x, norm_g, w_in, attn_sinks, rel_bias, g_q_lora, w_q_up, g_kv_lora, w_kv_up,
           w_proj_a, w_proj_b, w_proj_c, w_out, final_g):
    batch, seq, d = x.shape
    assert batch == 1 and d == D_MODEL and seq % 1024 == 0
    depth = w_in.shape[0]
    xs = x.reshape(seq, d)
    bias = _swa_bias(rel_bias)
    rope = _rope_table(seq)

    for l in range(depth):
        w_all = _layout_w_in(w_in[l])
        h = _rmsnorm(xs, norm_g[l], MXU_DTYPE)
        p16 = _matmul(h, w_all, 0, P16_WIDTH, MXU_DTYPE, tm=1024, tn=768, name="in_proj_bf16")
        pf = _matmul(h, w_all, P16_WIDTH, PF_WIDTH, jnp.float32, tm=1024, tn=768, name="in_proj_f32")

        ya = _swa(p16, pf, attn_sinks[l], bias)
        qf, kf, vb = _mla_pre(pf, g_q_lora[l], g_kv_lora[l], _layout_w_q_up(w_q_up[l]),
                              _layout_w_kv_up(w_kv_up[l]), rope)
        yb = _mla(qf, kf, vb, pf)
        yc = _sb(p16, pf)

        merged = _merge(h, w_all, ya, yb, yc, w_proj_a[l].astype(MXU_DTYPE),
                        w_proj_b[l].astype(MXU_DTYPE), w_proj_c[l].astype(MXU_DTYPE))
        xs = _matmul(merged, w_out[l].astype(MXU_DTYPE), 0, D_MODEL, jnp.float32,
                     tm=1024, tn=512, residual=xs, name="out_proj")

    out = _rmsnorm(xs, final_g, jnp.float32)
    return out.reshape(batch, seq, d)
```

```python
import functools
import math

import jax
import jax.numpy as jnp
from jax import lax
from jax.experimental import pallas as pl
from jax.experimental.pallas import tpu as pltpu

D_MODEL = 4096
HEAD_DIM = 128
LANES = 128
BLOCK = 128
EPS = 1e-6
A_HEADS = 8
A_KV_HEADS = 2
A_GROUP = A_HEADS // A_KV_HEADS
A_WIDTH = A_HEADS * HEAD_DIM
A_KV_WIDTH = A_KV_HEADS * HEAD_DIM
NUM_BUCKETS = 32
MAX_DISTANCE = 128
B_HEADS = 4
B_Q_LORA = 1024
B_KV_LORA = 512
B_NOPE = 128
B_ROPE = 64
B_V = 128
B_QK = B_NOPE + B_ROPE
B_QK_PAD = 256
B_WIDTH = B_HEADS * B_V
ROPE_THETA = 10000.0
C_HEADS = 4
C_WIDTH = C_HEADS * HEAD_DIM

MXU_DTYPE = jnp.bfloat16
LOG2E = math.log2(math.e)
VMEM_LIMIT_BYTES = 56 * 1024 * 1024

_IN_SIZES = (A_WIDTH, A_KV_WIDTH, A_KV_WIDTH, A_WIDTH, B_Q_LORA, B_KV_LORA, B_ROPE, B_WIDTH,
             C_WIDTH, C_WIDTH, C_WIDTH, C_WIDTH, D_MODEL, D_MODEL, D_MODEL)
_IN_NAMES = ("qa", "ka", "va", "za", "cq", "ckv", "kr", "zb", "qc", "kc", "vc", "zc", "ga", "gb", "gc")
_IN_OFF = {}
_o = 0
for _n, _s in zip(_IN_NAMES, _IN_SIZES):
    _IN_OFF[_n] = (_o, _s)
    _o += _s

_P16_COLS = (("qa", A_WIDTH), ("ka", A_KV_WIDTH), ("va", A_KV_WIDTH),
             ("qc", C_WIDTH), ("kc", C_WIDTH), ("vc", C_WIDTH))
_PF_COLS = (("za", A_WIDTH), ("cq", B_Q_LORA), ("ckv", B_KV_LORA), ("kr_a", 128), ("kr_b", 128),
            ("zb", B_WIDTH), ("zc", C_WIDTH))


def _offsets(cols):
    out, o = {}, 0
    for n, s in cols:
        out[n] = o
        o += s
    return out, o


_P16_OFF, P16_WIDTH = _offsets(_P16_COLS)
_PF_OFF, PF_WIDTH = _offsets(_PF_COLS)
GATE_OFF = P16_WIDTH + PF_WIDTH


def _cparams(*sem):
    return pltpu.CompilerParams(dimension_semantics=sem, vmem_limit_bytes=VMEM_LIMIT_BYTES)


def _dot(a, b):
    return jnp.dot(a, b, preferred_element_type=jnp.float32)


def _dot_nt(a, b):
    return lax.dot_general(a, b, (((1,), (1,)), ((), ())), preferred_element_type=jnp.float32)


def _silu(z):
    return z * jax.nn.sigmoid(z)


def _rmsnorm_kernel(x_ref, g_ref, o_ref):
    x = x_ref[...]
    y = x * lax.rsqrt(jnp.mean(x * x, axis=-1, keepdims=True) + EPS)
    o_ref[...] = (y * g_ref[...]).astype(o_ref.dtype)


def _rmsnorm(x, g, out_dtype, tm=512):
    s, d = x.shape
    return pl.pallas_call(
        _rmsnorm_kernel,
        grid=(s // tm,),
        in_specs=[pl.BlockSpec((tm, d), lambda i: (i, 0)),
                  pl.BlockSpec((1, d), lambda i: (0, 0))],
        out_specs=pl.BlockSpec((tm, d), lambda i: (i, 0)),
        out_shape=jax.ShapeDtypeStruct((s, d), out_dtype),
        compiler_params=_cparams("parallel"),
        name="rmsnorm",
    )(x, g.reshape(1, d))


def _matmul_kernel(a_ref, w_ref, o_ref):
    o_ref[...] = _dot(a_ref[...], w_ref[...]).astype(o_ref.dtype)


def _matmul_residual_kernel(a_ref, w_ref, r_ref, o_ref):
    o_ref[...] = r_ref[...] + _dot(a_ref[...], w_ref[...])


def _matmul(a, w, col_off, n_cols, out_dtype, tm, tn, residual=None, name="matmul"):
    s, k = a.shape
    assert s % tm == 0 and n_cols % tn == 0 and col_off % tn == 0
    jo = col_off // tn
    in_specs = [pl.BlockSpec((tm, k), lambda i, j: (i, 0)),
                pl.BlockSpec((k, tn), lambda i, j: (0, j + jo))]
    args = [a, w]
    body = _matmul_kernel
    if residual is not None:
        in_specs.append(pl.BlockSpec((tm, tn), lambda i, j: (i, j)))
        args.append(residual)
        body = _matmul_residual_kernel
    return pl.pallas_call(
        body,
        grid=(s // tm, n_cols // tn),
        in_specs=in_specs,
        out_specs=pl.BlockSpec((tm, tn), lambda i, j: (i, j)),
        out_shape=jax.ShapeDtypeStruct((s, n_cols), out_dtype),
        compiler_params=_cparams("parallel", "arbitrary"),
        name=name,
    )(*args)


def _merge_kernel(h_ref, wga_ref, wgb_ref, wgc_ref, ya_ref, yb_ref, yc_ref,
                  wa_ref, wb_ref, wc_ref, o_ref):
    h = h_ref[...]

    def branch(wg_ref, y_ref, w_ref):
        return jax.nn.sigmoid(_dot(h, wg_ref[...])) * _dot(y_ref[...], w_ref[...])

    merged = (branch(wga_ref, ya_ref, wa_ref) + branch(wgb_ref, yb_ref, wb_ref)
              + branch(wgc_ref, yc_ref, wc_ref))
    o_ref[...] = merged.astype(o_ref.dtype)


def _merge(h, w_all, ya, yb, yc, wa, wb, wc, tm=1024, tn=256):
    s = h.shape[0]
    nj = D_MODEL // tn
    g0 = GATE_OFF // tn

    def gate_spec(k):
        return pl.BlockSpec((D_MODEL, tn), lambda i, j: (0, g0 + k * nj + j))

    def row_spec(width):
        return pl.BlockSpec((tm, width), lambda i, j: (i, 0))

    def w_spec(width):
        return pl.BlockSpec((width, tn), lambda i, j: (0, j))

    return pl.pallas_call(
        _merge_kernel,
        grid=(s // tm, nj),
        in_specs=[row_spec(D_MODEL), gate_spec(0), gate_spec(1), gate_spec(2),
                  row_spec(A_WIDTH), row_spec(B_WIDTH), row_spec(C_WIDTH),
                  w_spec(A_WIDTH), w_spec(B_WIDTH), w_spec(C_WIDTH)],
        out_specs=pl.BlockSpec((tm, tn), lambda i, j: (i, j)),
        out_shape=jax.ShapeDtypeStruct((s, D_MODEL), MXU_DTYPE),
        compiler_params=_cparams("parallel", "arbitrary"),
        name="gated_merge",
    )(h, w_all, w_all, w_all, ya, yb, yc, wa, wb, wc)


def _t5_bucket(rel):
    n = jnp.maximum(rel, 0)
    max_exact = NUM_BUCKETS // 2
    logn = jnp.log(jnp.maximum(n, 1).astype(jnp.float32) / max_exact)
    large = max_exact + (logn / math.log(MAX_DISTANCE / max_exact)
                         * (NUM_BUCKETS - max_exact)).astype(jnp.int32)
    large = jnp.minimum(large, NUM_BUCKETS - 1)
    return jnp.where(n < max_exact, n, large)


def _swa_bias_kernel(rel_bias_ref, bucket_ref, o_ref):
    bucket = bucket_ref[...]
    for h in range(A_HEADS):
        acc = jnp.zeros(bucket.shape, jnp.float32)
        for b in range(NUM_BUCKETS):
            acc = jnp.where(bucket == b, rel_bias_ref[b, h], acc)
        o_ref[h] = acc


def _swa_bias(rel_bias):
    t = jnp.arange(BLOCK)[:, None]
    s = jnp.arange(2 * BLOCK)[None, :]
    bucket = _t5_bucket(BLOCK + t - s).astype(jnp.int32)
    return pl.pallas_call(
        _swa_bias_kernel,
        in_specs=[pl.BlockSpec(memory_space=pltpu.SMEM),
                  pl.BlockSpec((BLOCK, 2 * BLOCK), lambda: (0, 0))],
        out_specs=pl.BlockSpec((A_HEADS, BLOCK, 2 * BLOCK), lambda: (0, 0, 0)),
        out_shape=jax.ShapeDtypeStruct((A_HEADS, BLOCK, 2 * BLOCK), jnp.float32),
        name="swa_bias_table",
    )(rel_bias, bucket)


def _swa_kernel(sinks_ref, q_ref, kc_ref, kp_ref, vc_ref, vp_ref, bias_ref, z_ref, o_ref):
    n = pl.program_id(0)
    rows = A_GROUP * BLOCK
    t = lax.broadcasted_iota(jnp.int32, (rows, 2 * BLOCK), 0) & (BLOCK - 1)
    s = lax.broadcasted_iota(jnp.int32, (rows, 2 * BLOCK), 1)
    ok = (s > jnp.maximum(t, jnp.where(n > 0, -1, BLOCK - 1))) & (s <= t + BLOCK)
    scale = HEAD_DIM ** -0.5

    def head_cols(h):
        return slice(h * HEAD_DIM, (h + 1) * HEAD_DIM)

    scores = []
    for hk in range(A_KV_HEADS):
        q = jnp.concatenate([q_ref[:, head_cols(hk * A_GROUP + g)] for g in range(A_GROUP)], axis=0)
        kk = jnp.concatenate([kp_ref[:, head_cols(hk)], kc_ref[:, head_cols(hk)]], axis=0)
        scores.append(_dot_nt(q, kk))
    probs = []
    for hk in range(A_KV_HEADS):
        sc = jnp.where(ok, scores[hk] * scale + bias_ref[hk], -jnp.inf)
        sink = jnp.concatenate([jnp.full((BLOCK, LANES), sinks_ref[hk * A_GROUP + g], jnp.float32)
                                for g in range(A_GROUP)], axis=0)
        m = jnp.maximum(jnp.max(sc, axis=-1, keepdims=True), sink)
        e = jnp.exp(sc - jnp.concatenate([m, m], axis=-1))
        inv = 1.0 / (jnp.sum(e, axis=-1, keepdims=True) + jnp.exp(sink - m))
        probs.append((e * jnp.concatenate([inv, inv], axis=-1)).astype(MXU_DTYPE))
    for hk in range(A_KV_HEADS):
        vv = jnp.concatenate([vp_ref[:, head_cols(hk)], vc_ref[:, head_cols(hk)]], axis=0)
        out = _dot(probs[hk], vv)
        for g in range(A_GROUP):
            cols = head_cols(hk * A_GROUP + g)
            o_ref[:, cols] = (out[g * BLOCK:(g + 1) * BLOCK] * _silu(z_ref[:, cols])).astype(o_ref.dtype)


def _swa(p16, pf, sinks, bias):
    s = p16.shape[0]
    qb = _P16_OFF["qa"] // A_WIDTH
    kb = _P16_OFF["ka"] // A_KV_WIDTH
    vb = _P16_OFF["va"] // A_KV_WIDTH
    zb = _PF_OFF["za"] // A_WIDTH
    prev = lambda n: jnp.maximum(n - 1, 0)
    return pl.pallas_call(
        _swa_kernel,
        grid=(s // BLOCK,),
        in_specs=[pl.BlockSpec(memory_space=pltpu.SMEM),
                  pl.BlockSpec((BLOCK, A_WIDTH), lambda n: (n, qb)),
                  pl.BlockSpec((BLOCK, A_KV_WIDTH), lambda n: (n, kb)),
                  pl.BlockSpec((BLOCK, A_KV_WIDTH), lambda n: (prev(n), kb)),
                  pl.BlockSpec((BLOCK, A_KV_WIDTH), lambda n: (n, vb)),
                  pl.BlockSpec((BLOCK, A_KV_WIDTH), lambda n: (prev(n), vb)),
                  pl.BlockSpec((A_KV_HEADS, A_GROUP * BLOCK, 2 * BLOCK), lambda n: (0, 0, 0)),
                  pl.BlockSpec((BLOCK, A_WIDTH), lambda n: (n, zb))],
        out_specs=pl.BlockSpec((BLOCK, A_WIDTH), lambda n: (n, 0)),
        out_shape=jax.ShapeDtypeStruct((s, A_WIDTH), MXU_DTYPE),
        compiler_params=_cparams("parallel"),
        name="swa_attention",
    )(sinks, p16, p16, p16, p16, p16, bias.reshape(A_KV_HEADS, A_GROUP * BLOCK, 2 * BLOCK), pf)


def _rms_cast(x, g):
    y = x * lax.rsqrt(jnp.mean(x * x, axis=-1, keepdims=True) + EPS)
    return (y * g).astype(MXU_DTYPE)


def _mla_pre_kernel(cq_ref, ckv_ref, kr_ref, gq_ref, gkv_ref, wq_ref, wkv_ref, rope_ref,
                    qf_ref, kf_ref, v_ref):
    tc = rope_ref[:, 0:128]
    ts = rope_ref[:, 128:256]
    qraw = _dot(_rms_cast(cq_ref[...], gq_ref[...]), wq_ref[...])
    for h in range(B_HEADS):
        b = 3 * 128 * h
        qf_ref[:, h * B_QK_PAD:h * B_QK_PAD + B_NOPE] = qraw[:, b:b + 128].astype(qf_ref.dtype)
        rot = qraw[:, b + 128:b + 256] * tc + qraw[:, b + 256:b + 384] * ts
        qf_ref[:, h * B_QK_PAD + B_NOPE:(h + 1) * B_QK_PAD] = rot.astype(qf_ref.dtype)
    kvraw = _dot(_rms_cast(ckv_ref[...], gkv_ref[...]), wkv_ref[...])
    kr = kr_ref[...]
    krot = (kr[:, 0:128] * tc + kr[:, 128:256] * ts).astype(kf_ref.dtype)
    for h in range(B_HEADS):
        kf_ref[:, h * B_QK_PAD:h * B_QK_PAD + B_NOPE] = (
            kvraw[:, h * B_NOPE:(h + 1) * B_NOPE].astype(kf_ref.dtype))
        kf_ref[:, h * B_QK_PAD + B_NOPE:(h + 1) * B_QK_PAD] = krot
    v_ref[...] = kvraw[:, B_HEADS * B_NOPE:].astype(v_ref.dtype)


def _mla_pre(pf, gq, gkv, wq, wkv, rope, tm=512):
    s = pf.shape[0]
    row = lambda width, off: pl.BlockSpec((tm, width), lambda i: (i, off // width))
    full = lambda a: pl.BlockSpec(a.shape, lambda i: (0, 0))
    gq = gq.reshape(1, B_Q_LORA)
    gkv = gkv.reshape(1, B_KV_LORA)
    out_row = lambda width: pl.BlockSpec((tm, width), lambda i: (i, 0))
    return pl.pallas_call(
        _mla_pre_kernel,
        grid=(s // tm,),
        in_specs=[row(B_Q_LORA, _PF_OFF["cq"]), row(B_KV_LORA, _PF_OFF["ckv"]),
                  row(256, _PF_OFF["kr_a"]), full(gq), full(gkv), full(wq), full(wkv),
                  pl.BlockSpec((tm, 256), lambda i: (i, 0))],
        out_specs=[out_row(B_HEADS * B_QK_PAD), out_row(B_HEADS * B_QK_PAD), out_row(B_WIDTH)],
        out_shape=[jax.ShapeDtypeStruct((s, B_HEADS * B_QK_PAD), MXU_DTYPE),
                   jax.ShapeDtypeStruct((s, B_HEADS * B_QK_PAD), MXU_DTYPE),
                   jax.ShapeDtypeStruct((s, B_WIDTH), MXU_DTYPE)],
        compiler_params=_cparams("parallel"),
        name="mla_pre",
    )(pf, pf, pf, gq, gkv, wq, wkv, rope)


MLA_UNROLL = 4


def _mla_kernel(q_ref, k_ref, v_ref, z_ref, o_ref,
                y_scr, p_scr, alpha_scr, m_scr, l_scr, acc_scr, *, t):
    i = pl.program_id(1)
    n = i + 1
    q = q_ref[...]
    c = (B_QK ** -0.5) * LOG2E

    def key_rows(pos):
        kb = jnp.where(pos == 0, i, pos - 1)
        return pl.ds(pl.multiple_of(kb * t, t), t)

    def stage1(pos, slot):
        y_scr[slot] = _dot_nt(q, k_ref[key_rows(pos), :]) * c

    def stage2(slot, diagonal=False):
        y = y_scr[slot]
        if diagonal:
            row = lax.broadcasted_iota(jnp.int32, (t, t), 0)
            col = lax.broadcasted_iota(jnp.int32, (t, t), 1)
            y = jnp.where(col <= row, y, -jnp.inf)
        m = m_scr[...]
        m_new = jnp.maximum(m, jnp.max(y, axis=-1, keepdims=True))
        alpha = jnp.exp2(m - m_new)
        p = jnp.exp2(y - jnp.concatenate([m_new] * (t // LANES), axis=-1))
        l_scr[...] = alpha * l_scr[...] + jnp.sum(p, axis=-1, keepdims=True)
        m_scr[...] = m_new
        alpha_scr[slot] = alpha
        p_scr[slot] = p.astype(MXU_DTYPE)

    def stage3(pos, slot):
        acc_scr[...] = alpha_scr[slot] * acc_scr[...] + _dot(p_scr[slot], v_ref[key_rows(pos), :])

    m_scr[...] = jnp.full((t, LANES), -jnp.inf, jnp.float32)
    l_scr[...] = jnp.zeros((t, LANES), jnp.float32)
    acc_scr[...] = jnp.zeros((t, B_V), jnp.float32)

    stage1(0, 0)

    @pl.when(n > 1)
    def _():
        stage1(1, 1)

    stage2(0, diagonal=True)

    def steady(tt, s_t):
        stage3(tt - 2, s_t)
        stage2(1 - s_t)
        stage1(tt, s_t)

    def body(u, carry):
        tt = 2 + MLA_UNROLL * u
        for k in range(MLA_UNROLL):
            steady(tt + k, k % 2)
        return carry

    groups = jnp.maximum(n - 2, 0) // MLA_UNROLL
    lax.fori_loop(0, groups, body, 0)
    t_next = 2 + MLA_UNROLL * groups

    for k in range(MLA_UNROLL - 1):
        @pl.when(t_next + k < n)
        def _():
            steady(t_next + k, k % 2)

    @pl.when(n >= 2)
    def _():
        stage3(n - 2, n % 2)
        stage2((n - 1) % 2)

    stage3(n - 1, (n - 1) % 2)
    o_ref[...] = ((acc_scr[...] / l_scr[...]) * _silu(z_ref[...])).astype(o_ref.dtype)


def _mla(qf, kf, v, pf, t=512):
    s = qf.shape[0]
    zb = _PF_OFF["zb"] // B_V
    return pl.pallas_call(
        functools.partial(_mla_kernel, t=t),
        grid=(B_HEADS, s // t),
        in_specs=[pl.BlockSpec((t, B_QK_PAD), lambda h, i: (i, h)),
                  pl.BlockSpec((s, B_QK_PAD), lambda h, i: (0, h)),
                  pl.BlockSpec((s, B_V), lambda h, i: (0, h)),
                  pl.BlockSpec((t, B_V), lambda h, i: (i, zb + h))],
        out_specs=pl.BlockSpec((t, B_V), lambda h, i: (i, h)),
        out_shape=jax.ShapeDtypeStruct((s, B_WIDTH), MXU_DTYPE),
        scratch_shapes=[pltpu.VMEM((2, t, t), jnp.float32),
                        pltpu.VMEM((2, t, t), MXU_DTYPE),
                        pltpu.VMEM((2, t, LANES), jnp.float32),
                        pltpu.VMEM((t, LANES), jnp.float32),
                        pltpu.VMEM((t, LANES), jnp.float32),
                        pltpu.VMEM((t, B_V), jnp.float32)],
        compiler_params=_cparams("parallel", "arbitrary"),
        name="mla_attention",
    )(qf, kf, v, pf)


SB_UNROLL = 4


def _sb_kernel(q_ref, k_ref, v_ref, z_ref, uu_ref, o_ref,
               s_scr, hl_scr, lb_scr, off_scr, off_run_scr, w_scr, acc_scr, *, tq, tk):
    assert tq == 2 * tk
    i = pl.program_id(1)
    q = q_ref[...]
    scale = HEAD_DIM ** -0.5
    n = 2 * i + 2

    def key_rows(pos):
        return pl.ds(pl.multiple_of((n - 1 - pos) * tk, tk), tk)

    def stage1(pos, slot):
        s_scr[slot] = _dot_nt(q, k_ref[key_rows(pos), :])

    def stage2(slot, d=None):
        y = s_scr[slot] * (scale * LOG2E)
        l1p = jnp.log2(1.0 + jnp.exp2(-jnp.abs(y)))
        drop = jnp.maximum(y, 0.0) + l1p
        log_beta = y - drop
        if d is not None:
            row = lax.broadcasted_iota(jnp.int32, (tq, tk), 0)
            col = lax.broadcasted_iota(jnp.int32, (tq, tk), 1) + d * tk
            past = col < row
            drop = jnp.where(past, drop, 0.0)
            log_beta = jnp.where(past, log_beta, -jnp.inf)
        hi = drop.astype(MXU_DTYPE)
        lo = (drop - hi.astype(jnp.float32)).astype(MXU_DTYPE)
        hl_scr[slot] = jnp.concatenate([hi, lo], axis=-1)
        lb_scr[slot] = log_beta
        off = off_run_scr[...]
        off_scr[slot] = off
        off_run_scr[...] = off + jnp.sum(drop, axis=-1, keepdims=True)

    def stage3(slot):
        within = _dot(hl_scr[slot], uu_ref[...])
        off = jnp.concatenate([off_scr[slot]] * (tk // LANES), axis=-1)
        w_scr[slot] = jnp.exp2(lb_scr[slot] - (within + off)).astype(MXU_DTYPE)

    def stage4(pos, slot):
        acc_scr[...] += _dot(w_scr[slot], v_ref[key_rows(pos), :])

    off_run_scr[...] = jnp.zeros((tq, LANES), jnp.float32)
    acc_scr[...] = jnp.zeros((tq, HEAD_DIM), jnp.float32)

    stage1(0, 0)

    stage2(0, d=1)
    stage1(1, 1)

    stage3(0)
    stage2(1, d=0)

    @pl.when(n > 2)
    def _():
        stage1(2, 0)

    def steady(tt, s_t):
        stage4(tt - 3, 1 - s_t)
        stage3(s_t)
        stage2(1 - s_t)
        stage1(tt, s_t)

    def body(u, carry):
        tt = 3 + SB_UNROLL * u
        for k in range(SB_UNROLL):
            steady(tt + k, (1 + k) % 2)
        return carry

    groups = jnp.maximum(n - 3, 0) // SB_UNROLL
    lax.fori_loop(0, groups, body, 0)
    t_next = 3 + SB_UNROLL * groups

    for k in range(SB_UNROLL - 1):
        @pl.when(t_next + k < n)
        def _():
            steady(t_next + k, (1 + k) % 2)

    @pl.when(n >= 3)
    def _():
        stage4(n - 3, (n - 3) % 2)
        stage3(n % 2)
        stage2((n - 1) % 2)

    stage4(n - 2, n % 2)
    stage3((n - 1) % 2)
    stage4(n - 1, (n - 1) % 2)
    o_ref[...] = (acc_scr[...] * _silu(z_ref[...])).astype(o_ref.dtype)


def _sb(p16, pf, tq=512, tk=256):
    s = p16.shape[0]
    qb = _P16_OFF["qc"] // HEAD_DIM
    kb = _P16_OFF["kc"] // HEAD_DIM
    vb = _P16_OFF["vc"] // HEAD_DIM
    zb = _PF_OFF["zc"] // HEAD_DIM
    j = lax.broadcasted_iota(jnp.int32, (tk, tk), 0)
    c = lax.broadcasted_iota(jnp.int32, (tk, tk), 1)
    u = (j > c).astype(MXU_DTYPE)
    uu = jnp.concatenate([u, u], axis=0)
    return pl.pallas_call(
        functools.partial(_sb_kernel, tq=tq, tk=tk),
        grid=(C_HEADS, s // tq),
        in_specs=[pl.BlockSpec((tq, HEAD_DIM), lambda h, i: (i, qb + h)),
                  pl.BlockSpec((s, HEAD_DIM), lambda h, i: (0, kb + h)),
                  pl.BlockSpec((s, HEAD_DIM), lambda h, i: (0, vb + h)),
                  pl.BlockSpec((tq, HEAD_DIM), lambda h, i: (i, zb + h)),
                  pl.BlockSpec((2 * tk, tk), lambda h, i: (0, 0))],
        out_specs=pl.BlockSpec((tq, HEAD_DIM), lambda h, i: (i, h)),
        out_shape=jax.ShapeDtypeStruct((s, C_WIDTH), MXU_DTYPE),
        scratch_shapes=[pltpu.VMEM((2, tq, tk), jnp.float32),
                        pltpu.VMEM((2, tq, 2 * tk), MXU_DTYPE),
                        pltpu.VMEM((2, tq, tk), jnp.float32),
                        pltpu.VMEM((2, tq, LANES), jnp.float32),
                        pltpu.VMEM((tq, LANES), jnp.float32),
                        pltpu.VMEM((2, tq, tk), MXU_DTYPE),
                        pltpu.VMEM((tq, HEAD_DIM), jnp.float32)],
        compiler_params=_cparams("parallel", "arbitrary"),
        name="stick_breaking_attention",
    )(p16, p16, p16, pf, uu)


def _swap_halves(w):
    half = w.shape[-1] // 2
    return jnp.concatenate([w[..., half:], w[..., :half]], axis=-1)


def _pad_cols(w, width):
    return jnp.pad(w, ((0, 0), (0, width - w.shape[-1])))


def _layout_w_in(w_in):
    col = lambda n: w_in[:, _IN_OFF[n][0]:_IN_OFF[n][0] + _IN_OFF[n][1]]
    kr = col("kr")
    parts = [col(n) for n, _ in _P16_COLS]
    parts += [col("za"), col("cq"), col("ckv"), _pad_cols(kr, 128), _pad_cols(_swap_halves(kr), 128),
              col("zb"), col("zc")]
    parts += [col("ga"), col("gb"), col("gc")]
    return jnp.concatenate(parts, axis=1).astype(MXU_DTYPE)


def _layout_w_q_up(w):
    parts = []
    for h in range(B_HEADS):
        nope = w[:, h * B_QK:h * B_QK + B_NOPE]
        rope = w[:, h * B_QK + B_NOPE:(h + 1) * B_QK]
        parts += [nope, _pad_cols(rope, 128), _pad_cols(_swap_halves(rope), 128)]
    return jnp.concatenate(parts, axis=1).astype(MXU_DTYPE)


def _layout_w_kv_up(w):
    per = B_NOPE + B_V
    ks = [w[:, h * per:h * per + B_NOPE] for h in range(B_HEADS)]
    vs = [w[:, h * per + B_NOPE:(h + 1) * per] for h in range(B_HEADS)]
    return jnp.concatenate(ks + vs, axis=1).astype(MXU_DTYPE)


def _rope_table(seq):
    pos = jnp.arange(seq, dtype=jnp.float32)
    inv = ROPE_THETA ** (-jnp.arange(0, B_ROPE, 2, dtype=jnp.float32) / B_ROPE)
    ang = pos[:, None] * inv[None, :]
    cos, sin = jnp.cos(ang), jnp.sin(ang)
    zeros = jnp.zeros((seq, 128 - B_ROPE), jnp.float32)
    return jnp.concatenate([cos, cos, zeros, -sin, sin, zeros], axis=1)


def kernel(x, norm_g, w_in, attn_sinks, rel_bias, g_q_lora, w_q_up, g_kv_lora, w_kv_up,
           w_proj_a, w_proj_b, w_proj_c, w_out, final_g):
    batch, seq, d = x.shape
    assert batch == 1 and d == D_MODEL and seq % 1024 == 0
    depth = w_in.shape[0]
    xs = x.reshape(seq, d)
    bias = _swa_bias(rel_bias)
    rope = _rope_table(seq)

    for l in range(depth):
        w_all = _layout_w_in(w_in[l])
        h = _rmsnorm(xs, norm_g[l], MXU_DTYPE)
        p16 = _matmul(h, w_all, 0, P16_WIDTH, MXU_DTYPE, tm=1024, tn=768, name="in_proj_bf16")
        pf = _matmul(h, w_all, P16_WIDTH, PF_WIDTH, jnp.float32, tm=1024, tn=768, name="in_proj_f32")

        ya = _swa(p16, pf, attn_sinks[l], bias)
        qf, kf, vb = _mla_pre(pf, g_q_lora[l], g_kv_lora[l], _layout_w_q_up(w_q_up[l]),
                              _layout_w_kv_up(w_kv_up[l]), rope)
        yb = _mla(qf, kf, vb, pf)
        yc = _sb(p16, pf)

        merged = _merge(h, w_all, ya, yb, yc, w_proj_a[l].astype(MXU_DTYPE),
                        w_proj_b[l].astype(MXU_DTYPE), w_proj_c[l].astype(MXU_DTYPE))
        xs = _matmul(merged, w_out[l].astype(MXU_DTYPE), 0, D_MODEL, jnp.float32,
                     tm=1024, tn=512, residual=xs, name="out_proj")

    out = _rmsnorm(xs, final_g, jnp.float32)
    return out.reshape(batch, seq, d)
```

```python
import functools
import math

import jax
import jax.numpy as jnp
from jax import lax
from jax.experimental import pallas as pl
from jax.experimental.pallas import tpu as pltpu

D_MODEL = 4096
HEAD_DIM = 128
LANES = 128
BLOCK = 128
EPS = 1e-6
A_HEADS = 8
A_KV_HEADS = 2
A_GROUP = A_HEADS // A_KV_HEADS
A_WIDTH = A_HEADS * HEAD_DIM
A_KV_WIDTH = A_KV_HEADS * HEAD_DIM
NUM_BUCKETS = 32
MAX_DISTANCE = 128
B_HEADS = 4
B_Q_LORA = 1024
B_KV_LORA = 512
B_NOPE = 128
B_ROPE = 64
B_V = 128
B_QK = B_NOPE + B_ROPE
B_QK_PAD = 256
B_WIDTH = B_HEADS * B_V
ROPE_THETA = 10000.0
C_HEADS = 4
C_WIDTH = C_HEADS * HEAD_DIM

MXU_DTYPE = jnp.bfloat16
LOG2E = math.log2(math.e)
VMEM_LIMIT_BYTES = 56 * 1024 * 1024

_IN_SIZES = (A_WIDTH, A_KV_WIDTH, A_KV_WIDTH, A_WIDTH, B_Q_LORA, B_KV_LORA, B_ROPE, B_WIDTH,
             C_WIDTH, C_WIDTH, C_WIDTH, C_WIDTH, D_MODEL, D_MODEL, D_MODEL)
_IN_NAMES = ("qa", "ka", "va", "za", "cq", "ckv", "kr", "zb", "qc", "kc", "vc", "zc", "ga", "gb", "gc")
_IN_OFF = {}
_o = 0
for _n, _s in zip(_IN_NAMES, _IN_SIZES):
    _IN_OFF[_n] = (_o, _s)
    _o += _s

_P16_COLS = (("qa", A_WIDTH), ("ka", A_KV_WIDTH), ("va", A_KV_WIDTH),
             ("qc", C_WIDTH), ("kc", C_WIDTH), ("vc", C_WIDTH))
_PF_COLS = (("za", A_WIDTH), ("cq", B_Q_LORA), ("ckv", B_KV_LORA), ("kr_a", 128), ("kr_b", 128),
            ("zb", B_WIDTH), ("zc", C_WIDTH))


def _offsets(cols):
    out, o = {}, 0
    for n, s in cols:
        out[n] = o
        o += s
    return out, o


_P16_OFF, P16_WIDTH = _offsets(_P16_COLS)
_PF_OFF, PF_WIDTH = _offsets(_PF_COLS)
GATE_OFF = P16_WIDTH + PF_WIDTH


def _cparams(*sem):
    return pltpu.CompilerParams(dimension_semantics=sem, vmem_limit_bytes=VMEM_LIMIT_BYTES)


def _dot(a, b):
    return jnp.dot(a, b, preferred_element_type=jnp.float32)


def _dot_nt(a, b):
    return lax.dot_general(a, b, (((1,), (1,)), ((), ())), preferred_element_type=jnp.float32)


def _silu(z):
    return z * jax.nn.sigmoid(z)


def _rmsnorm_kernel(x_ref, g_ref, o_ref):
    x = x_ref[...]
    y = x * lax.rsqrt(jnp.mean(x * x, axis=-1, keepdims=True) + EPS)
    o_ref[...] = (y * g_ref[...]).astype(o_ref.dtype)


def _rmsnorm(x, g, out_dtype, tm=512):
    s, d = x.shape
    return pl.pallas_call(
        _rmsnorm_kernel,
        grid=(s // tm,),
        in_specs=[pl.BlockSpec((tm, d), lambda i: (i, 0)),
                  pl.BlockSpec((1, d), lambda i: (0, 0))],
        out_specs=pl.BlockSpec((tm, d), lambda i: (i, 0)),
        out_shape=jax.ShapeDtypeStruct((s, d), out_dtype),
        compiler_params=_cparams("parallel"),
        name="rmsnorm",
    )(x, g.reshape(1, d))


def _matmul_kernel(a_ref, w_ref, o_ref):
    o_ref[...] = _dot(a_ref[...], w_ref[...]).astype(o_ref.dtype)


def _matmul_residual_kernel(a_ref, w_ref, r_ref, o_ref):
    o_ref[...] = r_ref[...] + _dot(a_ref[...], w_ref[...])


def _matmul(a, w, col_off, n_cols, out_dtype, tm, tn, residual=None, name="matmul"):
    s, k = a.shape
    assert s % tm == 0 and n_cols % tn == 0 and col_off % tn == 0
    jo = col_off // tn
    in_specs = [pl.BlockSpec((tm, k), lambda i, j: (i, 0)),
                pl.BlockSpec((k, tn), lambda i, j: (0, j + jo))]
    args = [a, w]
    body = _matmul_kernel
    if residual is not None:
        in_specs.append(pl.BlockSpec((tm, tn), lambda i, j: (i, j)))
        args.append(residual)
        body = _matmul_residual_kernel
    return pl.pallas_call(
        body,
        grid=(s // tm, n_cols // tn),
        in_specs=in_specs,
        out_specs=pl.BlockSpec((tm, tn), lambda i, j: (i, j)),
        out_shape=jax.ShapeDtypeStruct((s, n_cols), out_dtype),
        compiler_params=_cparams("parallel", "arbitrary"),
        name=name,
    )(*args)


def _merge_kernel(h_ref, wga_ref, wgb_ref, wgc_ref, ya_ref, yb_ref, yc_ref,
                  wa_ref, wb_ref, wc_ref, o_ref):
    h = h_ref[...]

    def branch(wg_ref, y_ref, w_ref):
        return jax.nn.sigmoid(_dot(h, wg_ref[...])) * _dot(y_ref[...], w_ref[...])

    merged = (branch(wga_ref, ya_ref, wa_ref) + branch(wgb_ref, yb_ref, wb_ref)
              + branch(wgc_ref, yc_ref, wc_ref))
    o_ref[...] = merged.astype(o_ref.dtype)


def _merge(h, w_all, ya, yb, yc, wa, wb, wc, tm=1024, tn=256):
    s = h.shape[0]
    nj = D_MODEL // tn
    g0 = GATE_OFF // tn

    def gate_spec(k):
        return pl.BlockSpec((D_MODEL, tn), lambda i, j: (0, g0 + k * nj + j))

    def row_spec(width):
        return pl.BlockSpec((tm, width), lambda i, j: (i, 0))

    def w_spec(width):
        return pl.BlockSpec((width, tn), lambda i, j: (0, j))

    return pl.pallas_call(
        _merge_kernel,
        grid=(s // tm, nj),
        in_specs=[row_spec(D_MODEL), gate_spec(0), gate_spec(1), gate_spec(2),
                  row_spec(A_WIDTH), row_spec(B_WIDTH), row_spec(C_WIDTH),
                  w_spec(A_WIDTH), w_spec(B_WIDTH), w_spec(C_WIDTH)],
        out_specs=pl.BlockSpec((tm, tn), lambda i, j: (i, j)),
        out_shape=jax.ShapeDtypeStruct((s, D_MODEL), MXU_DTYPE),
        compiler_params=_cparams("parallel", "arbitrary"),
        name="gated_merge",
    )(h, w_all, w_all, w_all, ya, yb, yc, wa, wb, wc)


def _t5_bucket(rel):
    n = jnp.maximum(rel, 0)
    max_exact = NUM_BUCKETS // 2
    logn = jnp.log(jnp.maximum(n, 1).astype(jnp.float32) / max_exact)
    large = max_exact + (logn / math.log(MAX_DISTANCE / max_exact)
                         * (NUM_BUCKETS - max_exact)).astype(jnp.int32)
    large = jnp.minimum(large, NUM_BUCKETS - 1)
    return jnp.where(n < max_exact, n, large)


def _swa_bias_kernel(rel_bias_ref, bucket_ref, o_ref):
    bucket = bucket_ref[...]
    for h in range(A_HEADS):
        acc = jnp.zeros(bucket.shape, jnp.float32)
        for b in range(NUM_BUCKETS):
            acc = jnp.where(bucket == b, rel_bias_ref[b, h], acc)
        o_ref[h] = acc


def _swa_bias(rel_bias):
    t = jnp.arange(BLOCK)[:, None]
    s = jnp.arange(2 * BLOCK)[None, :]
    bucket = _t5_bucket(BLOCK + t - s).astype(jnp.int32)
    return pl.pallas_call(
        _swa_bias_kernel,
        in_specs=[pl.BlockSpec(memory_space=pltpu.SMEM),
                  pl.BlockSpec((BLOCK, 2 * BLOCK), lambda: (0, 0))],
        out_specs=pl.BlockSpec((A_HEADS, BLOCK, 2 * BLOCK), lambda: (0, 0, 0)),
        out_shape=jax.ShapeDtypeStruct((A_HEADS, BLOCK, 2 * BLOCK), jnp.float32),
        name="swa_bias_table",
    )(rel_bias, bucket)


def _swa_kernel(sinks_ref, q_ref, kc_ref, kp_ref, vc_ref, vp_ref, bias_ref, z_ref, o_ref):
    n = pl.program_id(0)
    rows = A_GROUP * BLOCK
    t = lax.broadcasted_iota(jnp.int32, (rows, 2 * BLOCK), 0) & (BLOCK - 1)
    s = lax.broadcasted_iota(jnp.int32, (rows, 2 * BLOCK), 1)
    ok = (s > jnp.maximum(t, jnp.where(n > 0, -1, BLOCK - 1))) & (s <= t + BLOCK)
    scale = HEAD_DIM ** -0.5

    def head_cols(h):
        return slice(h * HEAD_DIM, (h + 1) * HEAD_DIM)

    scores = []
    for hk in range(A_KV_HEADS):
        q = jnp.concatenate([q_ref[:, head_cols(hk * A_GROUP + g)] for g in range(A_GROUP)], axis=0)
        kk = jnp.concatenate([kp_ref[:, head_cols(hk)], kc_ref[:, head_cols(hk)]], axis=0)
        scores.append(_dot_nt(q, kk))
    probs = []
    for hk in range(A_KV_HEADS):
        sc = jnp.where(ok, scores[hk] * scale + bias_ref[hk], -jnp.inf)
        sink = jnp.concatenate([jnp.full((BLOCK, LANES), sinks_ref[hk * A_GROUP + g], jnp.float32)
                                for g in range(A_GROUP)], axis=0)
        m = jnp.maximum(jnp.max(sc, axis=-1, keepdims=True), sink)
        e = jnp.exp(sc - jnp.concatenate([m, m], axis=-1))
        inv = 1.0 / (jnp.sum(e, axis=-1, keepdims=True) + jnp.exp(sink - m))
        probs.append((e * jnp.concatenate([inv, inv], axis=-1)).astype(MXU_DTYPE))
    for hk in range(A_KV_HEADS):
        vv = jnp.concatenate([vp_ref[:, head_cols(hk)], vc_ref[:, head_cols(hk)]], axis=0)
        out = _dot(probs[hk], vv)
        for g in range(A_GROUP):
            cols = head_cols(hk * A_GROUP + g)
            o_ref[:, cols] = (out[g * BLOCK:(g + 1) * BLOCK] * _silu(z_ref[:, cols])).astype(o_ref.dtype)


def _swa(p16, pf, sinks, bias):
    s = p16.shape[0]
    qb = _P16_OFF["qa"] // A_WIDTH
    kb = _P16_OFF["ka"] // A_KV_WIDTH
    vb = _P16_OFF["va"] // A_KV_WIDTH
    zb = _PF_OFF["za"] // A_WIDTH
    prev = lambda n: jnp.maximum(n - 1, 0)
    return pl.pallas_call(
        _swa_kernel,
        grid=(s // BLOCK,),
        in_specs=[pl.BlockSpec(memory_space=pltpu.SMEM),
                  pl.BlockSpec((BLOCK, A_WIDTH), lambda n: (n, qb)),
                  pl.BlockSpec((BLOCK, A_KV_WIDTH), lambda n: (n, kb)),
                  pl.BlockSpec((BLOCK, A_KV_WIDTH), lambda n: (prev(n), kb)),
                  pl.BlockSpec((BLOCK, A_KV_WIDTH), lambda n: (n, vb)),
                  pl.BlockSpec((BLOCK, A_KV_WIDTH), lambda n: (prev(n), vb)),
                  pl.BlockSpec((A_KV_HEADS, A_GROUP * BLOCK, 2 * BLOCK), lambda n: (0, 0, 0)),
                  pl.BlockSpec((BLOCK, A_WIDTH), lambda n: (n, zb))],
        out_specs=pl.BlockSpec((BLOCK, A_WIDTH), lambda n: (n, 0)),
        out_shape=jax.ShapeDtypeStruct((s, A_WIDTH), MXU_DTYPE),
        compiler_params=_cparams("parallel"),
        name="swa_attention",
    )(sinks, p16, p16, p16, p16, p16, bias.reshape(A_KV_HEADS, A_GROUP * BLOCK, 2 * BLOCK), pf)


def _rms_cast(x, g):
    y = x * lax.rsqrt(jnp.mean(x * x, axis=-1, keepdims=True) + EPS)
    return (y * g).astype(MXU_DTYPE)


def _mla_pre_kernel(cq_ref, ckv_ref, kr_ref, gq_ref, gkv_ref, wq_ref, wkv_ref, rope_ref,
                    qf_ref, kf_ref, v_ref):
    tc = rope_ref[:, 0:128]
    ts = rope_ref[:, 128:256]
    qraw = _dot(_rms_cast(cq_ref[...], gq_ref[...]), wq_ref[...])
    for h in range(B_HEADS):
        b = 3 * 128 * h
        qf_ref[:, h * B_QK_PAD:h * B_QK_PAD + B_NOPE] = qraw[:, b:b + 128].astype(qf_ref.dtype)
        rot = qraw[:, b + 128:b + 256] * tc + qraw[:, b + 256:b + 384] * ts
        qf_ref[:, h * B_QK_PAD + B_NOPE:(h + 1) * B_QK_PAD] = rot.astype(qf_ref.dtype)
    kvraw = _dot(_rms_cast(ckv_ref[...], gkv_ref[...]), wkv_ref[...])
    kr = kr_ref[...]
    krot = (kr[:, 0:128] * tc + kr[:, 128:256] * ts).astype(kf_ref.dtype)
    for h in range(B_HEADS):
        kf_ref[:, h * B_QK_PAD:h * B_QK_PAD + B_NOPE] = (
            kvraw[:, h * B_NOPE:(h + 1) * B_NOPE].astype(kf_ref.dtype))
        kf_ref[:, h * B_QK_PAD + B_NOPE:(h + 1) * B_QK_PAD] = krot
    v_ref[...] = kvraw[:, B_HEADS * B_NOPE:].astype(v_ref.dtype)


def _mla_pre(pf, gq, gkv, wq, wkv, rope, tm=512):
    s = pf.shape[0]
    row = lambda width, off: pl.BlockSpec((tm, width), lambda i: (i, off // width))
    full = lambda a: pl.BlockSpec(a.shape, lambda i: (0, 0))
    gq = gq.reshape(1, B_Q_LORA)
    gkv = gkv.reshape(1, B_KV_LORA)
    out_row = lambda width: pl.BlockSpec((tm, width), lambda i: (i, 0))
    return pl.pallas_call(
        _mla_pre_kernel,
        grid=(s // tm,),
        in_specs=[row(B_Q_LORA, _PF_OFF["cq"]), row(B_KV_LORA, _PF_OFF["ckv"]),
                  row(256, _PF_OFF["kr_a"]), full(gq), full(gkv), full(wq), full(wkv),
                  pl.BlockSpec((tm, 256), lambda i: (i, 0))],
        out_specs=[out_row(B_HEADS * B_QK_PAD), out_row(B_HEADS * B_QK_PAD), out_row(B_WIDTH)],
        out_shape=[jax.ShapeDtypeStruct((s, B_HEADS * B_QK_PAD), MXU_DTYPE),
                   jax.ShapeDtypeStruct((s, B_HEADS * B_QK_PAD), MXU_DTYPE),
                   jax.ShapeDtypeStruct((s, B_WIDTH), MXU_DTYPE)],
        compiler_params=_cparams("parallel"),
        name="mla_pre",
    )(pf, pf, pf, gq, gkv, wq, wkv, rope)


def _run_pipeline(stages, n, n_static, unroll):
    depth = len(stages)
    t0 = n_static + 1
    assert unroll % 2 == 0 and depth >= 2 and t0 >= depth - 1

    def run(k, pos, slot, static_pos=None):
        stages[k](pos, slot, static_pos)

    for tt in range(t0):
        for k in reversed(range(depth)):
            pos = tt - k
            if pos < 0:
                continue
            if pos < n_static:
                run(k, pos, pos % 2, pos)
            else:
                pl.when(n > pos)(functools.partial(run, k, pos, pos % 2))

    def steady(tt, slot):
        for k in reversed(range(depth)):
            run(k, tt - k, (slot + k) % 2)

    def body(u, carry):
        for j in range(unroll):
            steady(t0 + unroll * u + j, (t0 + j) % 2)
        return carry

    trips = jnp.maximum(n - t0, 0) // unroll
    lax.fori_loop(0, trips, body, 0)
    t_next = t0 + unroll * trips
    for j in range(unroll - 1):
        pl.when(t_next + j < n)(functools.partial(steady, t_next + j, (t0 + j) % 2))

    def drain_first():
        for k in reversed(range(1, depth)):
            run(k, n - k, (n - k) % 2)

    pl.when(n >= t0)(drain_first)
    for e in range(1, depth - 1):
        for k in reversed(range(e + 1, depth)):
            run(k, n + e - k, (n + e - k) % 2)


MLA_UNROLL = 4


def _mla_kernel(q_ref, k_ref, v_ref, z_ref, o_ref,
                y_scr, p_scr, alpha_scr, m_scr, l_scr, acc_scr, *, tq, tk):
    n_diag = tq // tk
    i = pl.program_id(1)
    q = q_ref[...]
    c = (B_QK ** -0.5) * LOG2E

    def key_rows(pos):
        kb = jnp.where(pos < n_diag, n_diag * i + pos, pos - n_diag)
        return pl.ds(pl.multiple_of(kb * tk, tk), tk)

    def scores(pos, slot, static_pos):
        y_scr[slot] = _dot_nt(q, k_ref[key_rows(pos), :]) * c

    def softmax(pos, slot, static_pos):
        y = y_scr[slot]
        if static_pos is not None:
            row = lax.broadcasted_iota(jnp.int32, (tq, tk), 0)
            col = lax.broadcasted_iota(jnp.int32, (tq, tk), 1) + static_pos * tk
            y = jnp.where(col <= row, y, -jnp.inf)
        m = m_scr[...]
        m_new = jnp.maximum(m, jnp.max(y, axis=-1, keepdims=True))
        alpha = jnp.exp2(m - m_new)
        p = jnp.exp2(y - jnp.concatenate([m_new] * (tk // LANES), axis=-1))
        l_scr[...] = alpha * l_scr[...] + jnp.sum(p, axis=-1, keepdims=True)
        m_scr[...] = m_new
        alpha_scr[slot] = alpha
        p_scr[slot] = p.astype(MXU_DTYPE)

    def accumulate(pos, slot, static_pos):
        acc_scr[...] = alpha_scr[slot] * acc_scr[...] + _dot(p_scr[slot], v_ref[key_rows(pos), :])

    m_scr[...] = jnp.full((tq, LANES), -jnp.inf, jnp.float32)
    l_scr[...] = jnp.zeros((tq, LANES), jnp.float32)
    acc_scr[...] = jnp.zeros((tq, B_V), jnp.float32)
    _run_pipeline([scores, softmax, accumulate], n_diag * (i + 1), n_diag, MLA_UNROLL)
    o_ref[...] = ((acc_scr[...] / l_scr[...]) * _silu(z_ref[...])).astype(o_ref.dtype)


def _mla(qf, kf, v, pf, tq=1024, tk=512):
    s = qf.shape[0]
    zb = _PF_OFF["zb"] // B_V
    return pl.pallas_call(
        functools.partial(_mla_kernel, tq=tq, tk=tk),
        grid=(B_HEADS, s // tq),
        in_specs=[pl.BlockSpec((tq, B_QK_PAD), lambda h, i: (i, h)),
                  pl.BlockSpec((s, B_QK_PAD), lambda h, i: (0, h)),
                  pl.BlockSpec((s, B_V), lambda h, i: (0, h)),
                  pl.BlockSpec((tq, B_V), lambda h, i: (i, zb + h))],
        out_specs=pl.BlockSpec((tq, B_V), lambda h, i: (i, h)),
        out_shape=jax.ShapeDtypeStruct((s, B_WIDTH), MXU_DTYPE),
        scratch_shapes=[pltpu.VMEM((2, tq, tk), jnp.float32),
                        pltpu.VMEM((2, tq, tk), MXU_DTYPE),
                        pltpu.VMEM((2, tq, LANES), jnp.float32),
                        pltpu.VMEM((tq, LANES), jnp.float32),
                        pltpu.VMEM((tq, LANES), jnp.float32),
                        pltpu.VMEM((tq, B_V), jnp.float32)],
        compiler_params=_cparams("parallel", "arbitrary"),
        name="mla_attention",
    )(qf, kf, v, pf)


SB_UNROLL = 4


def _sb_kernel(q_ref, k_ref, v_ref, z_ref, uu_ref, o_ref,
               s_scr, hl_scr, lb_scr, off_scr, off_run_scr, w_scr, acc_scr, *, tq, tk):
    n_diag = tq // tk
    i = pl.program_id(1)
    q = q_ref[...]
    scale = HEAD_DIM ** -0.5
    n = n_diag * (i + 1)

    def key_rows(pos):
        return pl.ds(pl.multiple_of((n - 1 - pos) * tk, tk), tk)

    def score(pos, slot, static_pos):
        s_scr[slot] = _dot_nt(q, k_ref[key_rows(pos), :])

    def park(pos, slot, static_pos):
        y = s_scr[slot] * (scale * LOG2E)
        l1p = jnp.log2(1.0 + jnp.exp2(-jnp.abs(y)))
        drop = jnp.maximum(y, 0.0) + l1p
        log_beta = y - drop
        if static_pos is not None:
            row = lax.broadcasted_iota(jnp.int32, (tq, tk), 0)
            col = lax.broadcasted_iota(jnp.int32, (tq, tk), 1) + (n_diag - 1 - static_pos) * tk
            past = col < row
            drop = jnp.where(past, drop, 0.0)
            log_beta = jnp.where(past, log_beta, -jnp.inf)
        hi = drop.astype(MXU_DTYPE)
        lo = (drop - hi.astype(jnp.float32)).astype(MXU_DTYPE)
        hl_scr[slot] = jnp.concatenate([hi, lo], axis=-1)
        lb_scr[slot] = log_beta
        off = off_run_scr[...]
        off_scr[slot] = off
        off_run_scr[...] = off + jnp.sum(drop, axis=-1, keepdims=True)

    def weigh(pos, slot, static_pos):
        within = _dot(hl_scr[slot], uu_ref[...])
        off = jnp.concatenate([off_scr[slot]] * (tk // LANES), axis=-1)
        w_scr[slot] = jnp.exp2(lb_scr[slot] - (within + off)).astype(MXU_DTYPE)

    def accumulate(pos, slot, static_pos):
        acc_scr[...] += _dot(w_scr[slot], v_ref[key_rows(pos), :])

    off_run_scr[...] = jnp.zeros((tq, LANES), jnp.float32)
    acc_scr[...] = jnp.zeros((tq, HEAD_DIM), jnp.float32)
    _run_pipeline([score, park, weigh, accumulate], n, n_diag, SB_UNROLL)
    o_ref[...] = (acc_scr[...] * _silu(z_ref[...])).astype(o_ref.dtype)


def _sb(p16, pf, tq=1024, tk=256):
    s = p16.shape[0]
    qb = _P16_OFF["qc"] // HEAD_DIM
    kb = _P16_OFF["kc"] // HEAD_DIM
    vb = _P16_OFF["vc"] // HEAD_DIM
    zb = _PF_OFF["zc"] // HEAD_DIM
    j = lax.broadcasted_iota(jnp.int32, (tk, tk), 0)
    c = lax.broadcasted_iota(jnp.int32, (tk, tk), 1)
    u = (j > c).astype(MXU_DTYPE)
    uu = jnp.concatenate([u, u], axis=0)
    return pl.pallas_call(
        functools.partial(_sb_kernel, tq=tq, tk=tk),
        grid=(C_HEADS, s // tq),
        in_specs=[pl.BlockSpec((tq, HEAD_DIM), lambda h, i: (i, qb + h)),
                  pl.BlockSpec((s, HEAD_DIM), lambda h, i: (0, kb + h)),
                  pl.BlockSpec((s, HEAD_DIM), lambda h, i: (0, vb + h)),
                  pl.BlockSpec((tq, HEAD_DIM), lambda h, i: (i, zb + h)),
                  pl.BlockSpec((2 * tk, tk), lambda h, i: (0, 0))],
        out_specs=pl.BlockSpec((tq, HEAD_DIM), lambda h, i: (i, h)),
        out_shape=jax.ShapeDtypeStruct((s, C_WIDTH), MXU_DTYPE),
        scratch_shapes=[pltpu.VMEM((2, tq, tk), jnp.float32),
                        pltpu.VMEM((2, tq, 2 * tk), MXU_DTYPE),
                        pltpu.VMEM((2, tq, tk), jnp.float32),
                        pltpu.VMEM((2, tq, LANES), jnp.float32),
                        pltpu.VMEM((tq, LANES), jnp.float32),
                        pltpu.VMEM((2, tq, tk), MXU_DTYPE),
                        pltpu.VMEM((tq, HEAD_DIM), jnp.float32)],
        compiler_params=_cparams("parallel", "arbitrary"),
        name="stick_breaking_attention",
    )(p16, p16, p16, pf, uu)


def _swap_halves(w):
    half = w.shape[-1] // 2
    return jnp.concatenate([w[..., half:], w[..., :half]], axis=-1)


def _pad_cols(w, width):
    return jnp.pad(w, ((0, 0), (0, width - w.shape[-1])))


def _layout_w_in(w_in):
    col = lambda n: w_in[:, _IN_OFF[n][0]:_IN_OFF[n][0] + _IN_OFF[n][1]]
    kr = col("kr")
    parts = [col(n) for n, _ in _P16_COLS]
    parts += [col("za"), col("cq"), col("ckv"), _pad_cols(kr, 128), _pad_cols(_swap_halves(kr), 128),
              col("zb"), col("zc")]
    parts += [col("ga"), col("gb"), col("gc")]
    return jnp.concatenate(parts, axis=1).astype(MXU_DTYPE)


def _layout_w_q_up(w):
    parts = []
    for h in range(B_HEADS):
        nope = w[:, h * B_QK:h * B_QK + B_NOPE]
        rope = w[:, h * B_QK + B_NOPE:(h + 1) * B_QK]
        parts += [nope, _pad_cols(rope, 128), _pad_cols(_swap_halves(rope), 128)]
    return jnp.concatenate(parts, axis=1).astype(MXU_DTYPE)


def _layout_w_kv_up(w):
    per = B_NOPE + B_V
    ks = [w[:, h * per:h * per + B_NOPE] for h in range(B_HEADS)]
    vs = [w[:, h * per + B_NOPE:(h + 1) * per] for h in range(B_HEADS)]
    return jnp.concatenate(ks + vs, axis=1).astype(MXU_DTYPE)


def _rope_table(seq):
    pos = jnp.arange(seq, dtype=jnp.float32)
    inv = ROPE_THETA ** (-jnp.arange(0, B_ROPE, 2, dtype=jnp.float32) / B_ROPE)
    ang = pos[:, None] * inv[None, :]
    cos, sin = jnp.cos(ang), jnp.sin(ang)
    zeros = jnp.zeros((seq, 128 - B_ROPE), jnp.float32)
    return jnp.concatenate([cos, cos, zeros, -sin, sin, zeros], axis=1)


def kernel(x, norm_g, w_in, attn_sinks, rel_bias, g_q_lora, w_q_up, g_kv_lora, w_kv_up,
           w_proj_a, w_proj_b, w_proj_c, w_out, final_g):
    batch, seq, d = x.shape
    assert batch == 1 and d == D_MODEL and seq % 1024 == 0
    depth = w_in.shape[0]
    xs = x.reshape(seq, d)
    bias = _swa_bias(rel_bias)
    rope = _rope_table(seq)

    for l in range(depth):
        w_all = _layout_w_in(w_in[l])
        h = _rmsnorm(xs, norm_g[l], MXU_DTYPE)
        p16 = _matmul(h, w_all, 0, P16_WIDTH, MXU_DTYPE, tm=1024, tn=768, name="in_proj_bf16")
        pf = _matmul(h, w_all, P16_WIDTH, PF_WIDTH, jnp.float32, tm=1024, tn=768, name="in_proj_f32")

        ya = _swa(p16, pf, attn_sinks[l], bias)
        qf, kf, vb = _mla_pre(pf, g_q_lora[l], g_kv_lora[l], _layout_w_q_up(w_q_up[l]),
                              _layout_w_kv_up(w_kv_up[l]), rope)
        yb = _mla(qf, kf, vb, pf)
        yc = _sb(p16, pf)

        merged = _merge(h, w_all, ya, yb, yc, w_proj_a[l].astype(MXU_DTYPE),
                        w_proj_b[l].astype(MXU_DTYPE), w_proj_c[l].astype(MXU_DTYPE))
        xs = _matmul(merged, w_out[l].astype(MXU_DTYPE), 0, D_MODEL, jnp.float32,
                     tm=1024, tn=512, residual=xs, name="out_proj")

    out = _rmsnorm(xs, final_g, jnp.float32)
    return out.reshape(batch, seq, d)
```

```python
import functools
import math

import jax
import jax.numpy as jnp
from jax import lax
from jax.experimental import pallas as pl
from jax.experimental.pallas import tpu as pltpu

D_MODEL = 4096
HEAD_DIM = 128
LANES = 128
BLOCK = 128
EPS = 1e-6
A_HEADS = 8
A_KV_HEADS = 2
A_GROUP = A_HEADS // A_KV_HEADS
A_WIDTH = A_HEADS * HEAD_DIM
A_KV_WIDTH = A_KV_HEADS * HEAD_DIM
NUM_BUCKETS = 32
MAX_DISTANCE = 128
B_HEADS = 4
B_Q_LORA = 1024
B_KV_LORA = 512
B_NOPE = 128
B_ROPE = 64
B_V = 128
B_QK = B_NOPE + B_ROPE
B_QK_PAD = 256
B_WIDTH = B_HEADS * B_V
ROPE_THETA = 10000.0
C_HEADS = 4
C_WIDTH = C_HEADS * HEAD_DIM

MXU_DTYPE = jnp.bfloat16
LOG2E = math.log2(math.e)
VMEM_LIMIT_BYTES = 56 * 1024 * 1024

_IN_SIZES = (A_WIDTH, A_KV_WIDTH, A_KV_WIDTH, A_WIDTH, B_Q_LORA, B_KV_LORA, B_ROPE, B_WIDTH,
             C_WIDTH, C_WIDTH, C_WIDTH, C_WIDTH, D_MODEL, D_MODEL, D_MODEL)
_IN_NAMES = ("qa", "ka", "va", "za", "cq", "ckv", "kr", "zb", "qc", "kc", "vc", "zc", "ga", "gb", "gc")
_IN_OFF = {}
_o = 0
for _n, _s in zip(_IN_NAMES, _IN_SIZES):
    _IN_OFF[_n] = (_o, _s)
    _o += _s

_P16_COLS = (("qa", A_WIDTH), ("ka", A_KV_WIDTH), ("va", A_KV_WIDTH),
             ("qc", C_WIDTH), ("kc", C_WIDTH), ("vc", C_WIDTH))
_PF_COLS = (("za", A_WIDTH), ("cq", B_Q_LORA), ("ckv", B_KV_LORA), ("kr_a", 128), ("kr_b", 128),
            ("zb", B_WIDTH), ("zc", C_WIDTH))


def _offsets(cols):
    out, o = {}, 0
    for n, s in cols:
        out[n] = o
        o += s
    return out, o


_P16_OFF, P16_WIDTH = _offsets(_P16_COLS)
_PF_OFF, PF_WIDTH = _offsets(_PF_COLS)
GATE_OFF = P16_WIDTH + PF_WIDTH


def _cparams(*sem):
    return pltpu.CompilerParams(dimension_semantics=sem, vmem_limit_bytes=VMEM_LIMIT_BYTES)


def _dot(a, b):
    return jnp.dot(a, b, preferred_element_type=jnp.float32)


def _dot_nt(a, b):
    return lax.dot_general(a, b, (((1,), (1,)), ((), ())), preferred_element_type=jnp.float32)


def _silu(z):
    return z * jax.nn.sigmoid(z)


def _rmsnorm_kernel(x_ref, g_ref, o_ref):
    x = x_ref[...]
    y = x * lax.rsqrt(jnp.mean(x * x, axis=-1, keepdims=True) + EPS)
    o_ref[...] = (y * g_ref[...]).astype(o_ref.dtype)


def _rmsnorm(x, g, out_dtype, tm=512):
    s, d = x.shape
    return pl.pallas_call(
        _rmsnorm_kernel,
        grid=(s // tm,),
        in_specs=[pl.BlockSpec((tm, d), lambda i: (i, 0)),
                  pl.BlockSpec((1, d), lambda i: (0, 0))],
        out_specs=pl.BlockSpec((tm, d), lambda i: (i, 0)),
        out_shape=jax.ShapeDtypeStruct((s, d), out_dtype),
        compiler_params=_cparams("parallel"),
        name="rmsnorm",
    )(x, g.reshape(1, d))


def _matmul_kernel(a_ref, w_ref, o_ref):
    o_ref[...] = _dot(a_ref[...], w_ref[...]).astype(o_ref.dtype)


def _matmul_residual_kernel(a_ref, w_ref, r_ref, o_ref):
    o_ref[...] = r_ref[...] + _dot(a_ref[...], w_ref[...])


def _matmul(a, w, layer, col_off, n_cols, out_dtype, tm, tn, residual=None, name="matmul"):
    s, k = a.shape
    assert s % tm == 0 and n_cols % tn == 0 and col_off % tn == 0
    jo = col_off // tn
    in_specs = [pl.BlockSpec((tm, k), lambda i, j: (i, 0)),
                pl.BlockSpec((None, k, tn), lambda i, j: (layer, 0, j + jo))]
    args = [a, w]
    body = _matmul_kernel
    if residual is not None:
        in_specs.append(pl.BlockSpec((tm, tn), lambda i, j: (i, j)))
        args.append(residual)
        body = _matmul_residual_kernel
    return pl.pallas_call(
        body,
        grid=(s // tm, n_cols // tn),
        in_specs=in_specs,
        out_specs=pl.BlockSpec((tm, tn), lambda i, j: (i, j)),
        out_shape=jax.ShapeDtypeStruct((s, n_cols), out_dtype),
        compiler_params=_cparams("parallel", "arbitrary"),
        name=name,
    )(*args)


def _merge_kernel(h_ref, wga_ref, wgb_ref, wgc_ref, ya_ref, yb_ref, yc_ref,
                  wa_ref, wb_ref, wc_ref, o_ref):
    h = h_ref[...]

    def branch(wg_ref, y_ref, w_ref):
        return jax.nn.sigmoid(_dot(h, wg_ref[...])) * _dot(y_ref[...], w_ref[...])

    merged = (branch(wga_ref, ya_ref, wa_ref) + branch(wgb_ref, yb_ref, wb_ref)
              + branch(wgc_ref, yc_ref, wc_ref))
    o_ref[...] = merged.astype(o_ref.dtype)


def _merge(h, w_all, layer, ya, yb, yc, wa, wb, wc, tm=1024, tn=256):
    s = h.shape[0]
    nj = D_MODEL // tn
    g0 = GATE_OFF // tn

    def gate_spec(k):
        return pl.BlockSpec((None, D_MODEL, tn), lambda i, j: (layer, 0, g0 + k * nj + j))

    def row_spec(width):
        return pl.BlockSpec((tm, width), lambda i, j: (i, 0))

    def w_spec(width):
        return pl.BlockSpec((None, width, tn), lambda i, j: (layer, 0, j))

    return pl.pallas_call(
        _merge_kernel,
        grid=(s // tm, nj),
        in_specs=[row_spec(D_MODEL), gate_spec(0), gate_spec(1), gate_spec(2),
                  row_spec(A_WIDTH), row_spec(B_WIDTH), row_spec(C_WIDTH),
                  w_spec(A_WIDTH), w_spec(B_WIDTH), w_spec(C_WIDTH)],
        out_specs=pl.BlockSpec((tm, tn), lambda i, j: (i, j)),
        out_shape=jax.ShapeDtypeStruct((s, D_MODEL), MXU_DTYPE),
        compiler_params=_cparams("parallel", "arbitrary"),
        name="gated_merge",
    )(h, w_all, w_all, w_all, ya, yb, yc, wa, wb, wc)


def _t5_bucket(rel):
    n = jnp.maximum(rel, 0)
    max_exact = NUM_BUCKETS // 2
    logn = jnp.log(jnp.maximum(n, 1).astype(jnp.float32) / max_exact)
    large = max_exact + (logn / math.log(MAX_DISTANCE / max_exact)
                         * (NUM_BUCKETS - max_exact)).astype(jnp.int32)
    large = jnp.minimum(large, NUM_BUCKETS - 1)
    return jnp.where(n < max_exact, n, large)


def _swa_bias_kernel(rel_bias_ref, bucket_ref, o_ref):
    bucket = bucket_ref[...]
    for h in range(A_HEADS):
        acc = jnp.zeros(bucket.shape, jnp.float32)
        for b in range(NUM_BUCKETS):
            acc = jnp.where(bucket == b, rel_bias_ref[b, h], acc)
        o_ref[h] = acc


def _swa_bias(rel_bias):
    t = jnp.arange(BLOCK)[:, None]
    s = jnp.arange(2 * BLOCK)[None, :]
    bucket = _t5_bucket(BLOCK + t - s).astype(jnp.int32)
    return pl.pallas_call(
        _swa_bias_kernel,
        in_specs=[pl.BlockSpec(memory_space=pltpu.SMEM),
                  pl.BlockSpec((BLOCK, 2 * BLOCK), lambda: (0, 0))],
        out_specs=pl.BlockSpec((A_HEADS, BLOCK, 2 * BLOCK), lambda: (0, 0, 0)),
        out_shape=jax.ShapeDtypeStruct((A_HEADS, BLOCK, 2 * BLOCK), jnp.float32),
        name="swa_bias_table",
    )(rel_bias, bucket)


def _swa_kernel(sinks_ref, q_ref, kc_ref, kp_ref, vc_ref, vp_ref, bias_ref, z_ref, o_ref):
    n = pl.program_id(0)
    rows = A_GROUP * BLOCK
    t = lax.broadcasted_iota(jnp.int32, (rows, 2 * BLOCK), 0) & (BLOCK - 1)
    s = lax.broadcasted_iota(jnp.int32, (rows, 2 * BLOCK), 1)
    ok = (s > jnp.maximum(t, jnp.where(n > 0, -1, BLOCK - 1))) & (s <= t + BLOCK)
    scale = HEAD_DIM ** -0.5

    def head_cols(h):
        return slice(h * HEAD_DIM, (h + 1) * HEAD_DIM)

    scores = []
    for hk in range(A_KV_HEADS):
        q = jnp.concatenate([q_ref[:, head_cols(hk * A_GROUP + g)] for g in range(A_GROUP)], axis=0)
        kk = jnp.concatenate([kp_ref[:, head_cols(hk)], kc_ref[:, head_cols(hk)]], axis=0)
        scores.append(_dot_nt(q, kk))
    probs = []
    for hk in range(A_KV_HEADS):
        sc = jnp.where(ok, scores[hk] * scale + bias_ref[hk], -jnp.inf)
        sink = jnp.concatenate([jnp.full((BLOCK, LANES), sinks_ref[hk * A_GROUP + g], jnp.float32)
                                for g in range(A_GROUP)], axis=0)
        m = jnp.maximum(jnp.max(sc, axis=-1, keepdims=True), sink)
        e = jnp.exp(sc - jnp.concatenate([m, m], axis=-1))
        inv = 1.0 / (jnp.sum(e, axis=-1, keepdims=True) + jnp.exp(sink - m))
        probs.append((e * jnp.concatenate([inv, inv], axis=-1)).astype(MXU_DTYPE))
    for hk in range(A_KV_HEADS):
        vv = jnp.concatenate([vp_ref[:, head_cols(hk)], vc_ref[:, head_cols(hk)]], axis=0)
        out = _dot(probs[hk], vv)
        for g in range(A_GROUP):
            cols = head_cols(hk * A_GROUP + g)
            o_ref[:, cols] = (out[g * BLOCK:(g + 1) * BLOCK] * _silu(z_ref[:, cols])).astype(o_ref.dtype)


def _swa(p16, pf, sinks, bias):
    s = p16.shape[0]
    qb = _P16_OFF["qa"] // A_WIDTH
    kb = _P16_OFF["ka"] // A_KV_WIDTH
    vb = _P16_OFF["va"] // A_KV_WIDTH
    zb = _PF_OFF["za"] // A_WIDTH
    prev = lambda n: jnp.maximum(n - 1, 0)
    return pl.pallas_call(
        _swa_kernel,
        grid=(s // BLOCK,),
        in_specs=[pl.BlockSpec(memory_space=pltpu.SMEM),
                  pl.BlockSpec((BLOCK, A_WIDTH), lambda n: (n, qb)),
                  pl.BlockSpec((BLOCK, A_KV_WIDTH), lambda n: (n, kb)),
                  pl.BlockSpec((BLOCK, A_KV_WIDTH), lambda n: (prev(n), kb)),
                  pl.BlockSpec((BLOCK, A_KV_WIDTH), lambda n: (n, vb)),
                  pl.BlockSpec((BLOCK, A_KV_WIDTH), lambda n: (prev(n), vb)),
                  pl.BlockSpec((A_KV_HEADS, A_GROUP * BLOCK, 2 * BLOCK), lambda n: (0, 0, 0)),
                  pl.BlockSpec((BLOCK, A_WIDTH), lambda n: (n, zb))],
        out_specs=pl.BlockSpec((BLOCK, A_WIDTH), lambda n: (n, 0)),
        out_shape=jax.ShapeDtypeStruct((s, A_WIDTH), MXU_DTYPE),
        compiler_params=_cparams("parallel"),
        name="swa_attention",
    )(sinks, p16, p16, p16, p16, p16, bias.reshape(A_KV_HEADS, A_GROUP * BLOCK, 2 * BLOCK), pf)


def _rms_cast(x, g):
    y = x * lax.rsqrt(jnp.mean(x * x, axis=-1, keepdims=True) + EPS)
    return (y * g).astype(MXU_DTYPE)


def _mla_pre_kernel(cq_ref, ckv_ref, kr_ref, gq_ref, gkv_ref, wq_ref, wkv_ref, rope_ref,
                    qf_ref, kf_ref, v_ref):
    tc = rope_ref[:, 0:128]
    ts = rope_ref[:, 128:256]
    qraw = _dot(_rms_cast(cq_ref[...], gq_ref[...]), wq_ref[...])
    for h in range(B_HEADS):
        b = 3 * 128 * h
        qf_ref[:, h * B_QK_PAD:h * B_QK_PAD + B_NOPE] = qraw[:, b:b + 128].astype(qf_ref.dtype)
        rot = qraw[:, b + 128:b + 256] * tc + qraw[:, b + 256:b + 384] * ts
        qf_ref[:, h * B_QK_PAD + B_NOPE:(h + 1) * B_QK_PAD] = rot.astype(qf_ref.dtype)
    kvraw = _dot(_rms_cast(ckv_ref[...], gkv_ref[...]), wkv_ref[...])
    kr = kr_ref[...]
    krot = (kr[:, 0:128] * tc + kr[:, 128:256] * ts).astype(kf_ref.dtype)
    for h in range(B_HEADS):
        kf_ref[:, h * B_QK_PAD:h * B_QK_PAD + B_NOPE] = (
            kvraw[:, h * B_NOPE:(h + 1) * B_NOPE].astype(kf_ref.dtype))
        kf_ref[:, h * B_QK_PAD + B_NOPE:(h + 1) * B_QK_PAD] = krot
    v_ref[...] = kvraw[:, B_HEADS * B_NOPE:].astype(v_ref.dtype)


def _mla_pre(pf, gq, gkv, wq, wkv, rope, tm=512):
    s = pf.shape[0]
    row = lambda width, off: pl.BlockSpec((tm, width), lambda i: (i, off // width))
    full = lambda a: pl.BlockSpec(a.shape, lambda i: (0, 0))
    gq = gq.reshape(1, B_Q_LORA)
    gkv = gkv.reshape(1, B_KV_LORA)
    out_row = lambda width: pl.BlockSpec((tm, width), lambda i: (i, 0))
    return pl.pallas_call(
        _mla_pre_kernel,
        grid=(s // tm,),
        in_specs=[row(B_Q_LORA, _PF_OFF["cq"]), row(B_KV_LORA, _PF_OFF["ckv"]),
                  row(256, _PF_OFF["kr_a"]), full(gq), full(gkv), full(wq), full(wkv),
                  pl.BlockSpec((tm, 256), lambda i: (i, 0))],
        out_specs=[out_row(B_HEADS * B_QK_PAD), out_row(B_HEADS * B_QK_PAD), out_row(B_WIDTH)],
        out_shape=[jax.ShapeDtypeStruct((s, B_HEADS * B_QK_PAD), MXU_DTYPE),
                   jax.ShapeDtypeStruct((s, B_HEADS * B_QK_PAD), MXU_DTYPE),
                   jax.ShapeDtypeStruct((s, B_WIDTH), MXU_DTYPE)],
        compiler_params=_cparams("parallel"),
        name="mla_pre",
    )(pf, pf, pf, gq, gkv, wq, wkv, rope)


def _run_pipeline(stages, n, n_static, unroll):
    depth = len(stages)
    t0 = n_static + 1
    assert unroll % 2 == 0 and depth >= 2 and t0 >= depth - 1

    def run(k, pos, slot, static_pos=None):
        stages[k](pos, slot, static_pos)

    for tt in range(t0):
        for k in reversed(range(depth)):
            pos = tt - k
            if pos < 0:
                continue
            if pos < n_static:
                run(k, pos, pos % 2, pos)
            else:
                pl.when(n > pos)(functools.partial(run, k, pos, pos % 2))

    def steady(tt, slot):
        for k in reversed(range(depth)):
            run(k, tt - k, (slot + k) % 2)

    def body(u, carry):
        for j in range(unroll):
            steady(t0 + unroll * u + j, (t0 + j) % 2)
        return carry

    trips = jnp.maximum(n - t0, 0) // unroll
    lax.fori_loop(0, trips, body, 0)
    start = t0 + unroll * trips
    left = jnp.maximum(n - start, 0)
    width = unroll // 2
    while width >= 1:

        def block(start=start, width=width):
            for j in range(width):
                steady(start + j, (t0 + j) % 2)

        pl.when((left & width) != 0)(block)
        start = start + (left & width)
        width //= 2

    def drain(first):
        for e in range(0 if first else 1, depth - 1):
            for k in reversed(range(e + 1, depth)):
                run(k, n + e - k, (n + e - k) % 2)

    pl.when(n >= t0)(functools.partial(drain, True))
    pl.when(n < t0)(functools.partial(drain, False))


MLA_UNROLL = 4


def _mla_kernel(q_ref, k_ref, v_ref, z_ref, o_ref,
                y_scr, p_scr, alpha_scr, m_scr, l_scr, acc_scr, *, tq, tk):
    n_diag = tq // tk
    i = pl.program_id(1)
    q = q_ref[...]
    c = (B_QK ** -0.5) * LOG2E

    def key_rows(pos):
        kb = jnp.where(pos < n_diag, n_diag * i + pos, pos - n_diag)
        return pl.ds(pl.multiple_of(kb * tk, tk), tk)

    def scores(pos, slot, static_pos):
        y_scr[slot] = _dot_nt(q, k_ref[key_rows(pos), :]) * c

    def softmax(pos, slot, static_pos):
        y = y_scr[slot]
        if static_pos is not None:
            row = lax.broadcasted_iota(jnp.int32, (tq, tk), 0)
            col = lax.broadcasted_iota(jnp.int32, (tq, tk), 1) + static_pos * tk
            y = jnp.where(col <= row, y, -jnp.inf)
        m = m_scr[...]
        m_new = jnp.maximum(m, jnp.max(y, axis=-1, keepdims=True))
        alpha = jnp.exp2(m - m_new)
        p = jnp.exp2(y - jnp.concatenate([m_new] * (tk // LANES), axis=-1))
        l_scr[...] = alpha * l_scr[...] + jnp.sum(p, axis=-1, keepdims=True)
        m_scr[...] = m_new
        alpha_scr[slot] = alpha
        p_scr[slot] = p.astype(MXU_DTYPE)

    def accumulate(pos, slot, static_pos):
        acc_scr[...] = alpha_scr[slot] * acc_scr[...] + _dot(p_scr[slot], v_ref[key_rows(pos), :])

    m_scr[...] = jnp.full((tq, LANES), -jnp.inf, jnp.float32)
    l_scr[...] = jnp.zeros((tq, LANES), jnp.float32)
    acc_scr[...] = jnp.zeros((tq, B_V), jnp.float32)
    _run_pipeline([scores, softmax, accumulate], n_diag * (i + 1), n_diag, MLA_UNROLL)
    o_ref[...] = ((acc_scr[...] / l_scr[...]) * _silu(z_ref[...])).astype(o_ref.dtype)


def _mla(qf, kf, v, pf, tq=1024, tk=512):
    s = qf.shape[0]
    zb = _PF_OFF["zb"] // B_V
    return pl.pallas_call(
        functools.partial(_mla_kernel, tq=tq, tk=tk),
        grid=(B_HEADS, s // tq),
        in_specs=[pl.BlockSpec((tq, B_QK_PAD), lambda h, i: (i, h)),
                  pl.BlockSpec((s, B_QK_PAD), lambda h, i: (0, h)),
                  pl.BlockSpec((s, B_V), lambda h, i: (0, h)),
                  pl.BlockSpec((tq, B_V), lambda h, i: (i, zb + h))],
        out_specs=pl.BlockSpec((tq, B_V), lambda h, i: (i, h)),
        out_shape=jax.ShapeDtypeStruct((s, B_WIDTH), MXU_DTYPE),
        scratch_shapes=[pltpu.VMEM((2, tq, tk), jnp.float32),
                        pltpu.VMEM((2, tq, tk), MXU_DTYPE),
                        pltpu.VMEM((2, tq, LANES), jnp.float32),
                        pltpu.VMEM((tq, LANES), jnp.float32),
                        pltpu.VMEM((tq, LANES), jnp.float32),
                        pltpu.VMEM((tq, B_V), jnp.float32)],
        compiler_params=_cparams("parallel", "arbitrary"),
        name="mla_attention",
    )(qf, kf, v, pf)


SB_UNROLL = 4


def _sb_kernel(q_ref, k_ref, v_ref, z_ref, uu_ref, o_ref,
               s_scr, hl_scr, lb_scr, off_scr, off_run_scr, w_scr, acc_scr, *, tq, tk):
    n_diag = tq // tk
    i = pl.program_id(1)
    q = q_ref[...]
    scale = HEAD_DIM ** -0.5
    n = n_diag * (i + 1)

    def key_rows(pos):
        return pl.ds(pl.multiple_of((n - 1 - pos) * tk, tk), tk)

    def score(pos, slot, static_pos):
        s_scr[slot] = _dot_nt(q, k_ref[key_rows(pos), :])

    def park(pos, slot, static_pos):
        y = s_scr[slot] * (scale * LOG2E)
        l1p = jnp.log2(1.0 + jnp.exp2(-jnp.abs(y)))
        drop = jnp.maximum(y, 0.0) + l1p
        log_beta = y - drop
        if static_pos is not None:
            row = lax.broadcasted_iota(jnp.int32, (tq, tk), 0)
            col = lax.broadcasted_iota(jnp.int32, (tq, tk), 1) + (n_diag - 1 - static_pos) * tk
            past = col < row
            drop = jnp.where(past, drop, 0.0)
            log_beta = jnp.where(past, log_beta, -jnp.inf)
        hi = drop.astype(MXU_DTYPE)
        lo = (drop - hi.astype(jnp.float32)).astype(MXU_DTYPE)
        hl_scr[slot] = jnp.concatenate([hi, lo], axis=-1)
        lb_scr[slot] = log_beta
        off = off_run_scr[...]
        off_scr[slot] = off
        off_run_scr[...] = off + jnp.sum(drop, axis=-1, keepdims=True)

    def weigh(pos, slot, static_pos):
        within = _dot(hl_scr[slot], uu_ref[...])
        off = jnp.concatenate([off_scr[slot]] * (tk // LANES), axis=-1)
        w_scr[slot] = jnp.exp2(lb_scr[slot] - (within + off)).astype(MXU_DTYPE)

    def accumulate(pos, slot, static_pos):
        acc_scr[...] += _dot(w_scr[slot], v_ref[key_rows(pos), :])

    off_run_scr[...] = jnp.zeros((tq, LANES), jnp.float32)
    acc_scr[...] = jnp.zeros((tq, HEAD_DIM), jnp.float32)
    _run_pipeline([score, park, weigh, accumulate], n, n_diag, SB_UNROLL)
    o_ref[...] = (acc_scr[...] * _silu(z_ref[...])).astype(o_ref.dtype)


def _sb(p16, pf, tq=1024, tk=256):
    s = p16.shape[0]
    qb = _P16_OFF["qc"] // HEAD_DIM
    kb = _P16_OFF["kc"] // HEAD_DIM
    vb = _P16_OFF["vc"] // HEAD_DIM
    zb = _PF_OFF["zc"] // HEAD_DIM
    j = lax.broadcasted_iota(jnp.int32, (tk, tk), 0)
    c = lax.broadcasted_iota(jnp.int32, (tk, tk), 1)
    u = (j > c).astype(MXU_DTYPE)
    uu = jnp.concatenate([u, u], axis=0)
    return pl.pallas_call(
        functools.partial(_sb_kernel, tq=tq, tk=tk),
        grid=(C_HEADS, s // tq),
        in_specs=[pl.BlockSpec((tq, HEAD_DIM), lambda h, i: (i, qb + h)),
                  pl.BlockSpec((s, HEAD_DIM), lambda h, i: (0, kb + h)),
                  pl.BlockSpec((s, HEAD_DIM), lambda h, i: (0, vb + h)),
                  pl.BlockSpec((tq, HEAD_DIM), lambda h, i: (i, zb + h)),
                  pl.BlockSpec((2 * tk, tk), lambda h, i: (0, 0))],
        out_specs=pl.BlockSpec((tq, HEAD_DIM), lambda h, i: (i, h)),
        out_shape=jax.ShapeDtypeStruct((s, C_WIDTH), MXU_DTYPE),
        scratch_shapes=[pltpu.VMEM((2, tq, tk), jnp.float32),
                        pltpu.VMEM((2, tq, 2 * tk), MXU_DTYPE),
                        pltpu.VMEM((2, tq, tk), jnp.float32),
                        pltpu.VMEM((2, tq, LANES), jnp.float32),
                        pltpu.VMEM((tq, LANES), jnp.float32),
                        pltpu.VMEM((2, tq, tk), MXU_DTYPE),
                        pltpu.VMEM((tq, HEAD_DIM), jnp.float32)],
        compiler_params=_cparams("parallel", "arbitrary"),
        name="stick_breaking_attention",
    )(p16, p16, p16, pf, uu)


def _swap_halves(w):
    half = w.shape[-1] // 2
    return jnp.concatenate([w[..., half:], w[..., :half]], axis=-1)


def _pad_cols(w, width):
    return jnp.pad(w, ((0, 0), (0, width - w.shape[-1])))


_GATE_COLS = (("ga", D_MODEL), ("gb", D_MODEL), ("gc", D_MODEL))
W_ALL_WIDTH = GATE_OFF + 3 * D_MODEL


def _w_in_layout_kernel(w_ref, o_ref):
    x = w_ref[0]
    rows = x.shape[0]

    def col(name):
        a, width = _IN_OFF[name]
        return x[:, a:a + width]

    kr = col("kr")
    half = B_ROPE // 2
    pad = jnp.zeros((rows, LANES - B_ROPE), x.dtype)
    special = {"kr_a": lambda: jnp.concatenate([kr, pad], axis=1),
               "kr_b": lambda: jnp.concatenate([kr[:, half:], kr[:, :half], pad], axis=1)}
    dst = 0
    for name, width in _P16_COLS + _PF_COLS + _GATE_COLS:
        piece = special[name]() if name in special else col(name)
        o_ref[0, :, dst:dst + width] = piece.astype(o_ref.dtype)
        dst += width


def _layout_w_in(w_in, tr=128):
    depth, d, n_in = w_in.shape
    return pl.pallas_call(
        _w_in_layout_kernel,
        grid=(depth, d // tr),
        in_specs=[pl.BlockSpec((1, tr, n_in), lambda l, r: (l, r, 0))],
        out_specs=pl.BlockSpec((1, tr, W_ALL_WIDTH), lambda l, r: (l, r, 0)),
        out_shape=jax.ShapeDtypeStruct((depth, d, W_ALL_WIDTH), MXU_DTYPE),
        compiler_params=_cparams("parallel", "parallel"),
        name="w_in_layout",
    )(w_in)


def _cast_kernel(w_ref, o_ref):
    o_ref[...] = w_ref[...].astype(o_ref.dtype)


def _cast_weights(w, tr=512):
    depth, rows, cols = w.shape
    tr = min(tr, rows)
    spec = pl.BlockSpec((1, tr, cols), lambda l, r: (l, r, 0))
    return pl.pallas_call(
        _cast_kernel,
        grid=(depth, rows // tr),
        in_specs=[spec],
        out_specs=spec,
        out_shape=jax.ShapeDtypeStruct(w.shape, MXU_DTYPE),
        compiler_params=_cparams("parallel", "parallel"),
        name="cast_weights",
    )(w)


def _layout_w_q_up(w):
    parts = []
    for h in range(B_HEADS):
        nope = w[:, h * B_QK:h * B_QK + B_NOPE]
        rope = w[:, h * B_QK + B_NOPE:(h + 1) * B_QK]
        parts += [nope, _pad_cols(rope, 128), _pad_cols(_swap_halves(rope), 128)]
    return jnp.concatenate(parts, axis=1).astype(MXU_DTYPE)


def _layout_w_kv_up(w):
    per = B_NOPE + B_V
    ks = [w[:, h * per:h * per + B_NOPE] for h in range(B_HEADS)]
    vs = [w[:, h * per + B_NOPE:(h + 1) * per] for h in range(B_HEADS)]
    return jnp.concatenate(ks + vs, axis=1).astype(MXU_DTYPE)


def _rope_table(seq):
    pos = jnp.arange(seq, dtype=jnp.float32)
    inv = ROPE_THETA ** (-jnp.arange(0, B_ROPE, 2, dtype=jnp.float32) / B_ROPE)
    ang = pos[:, None] * inv[None, :]
    cos, sin = jnp.cos(ang), jnp.sin(ang)
    zeros = jnp.zeros((seq, 128 - B_ROPE), jnp.float32)
    return jnp.concatenate([cos, cos, zeros, -sin, sin, zeros], axis=1)


def kernel(x, norm_g, w_in, attn_sinks, rel_bias, g_q_lora, w_q_up, g_kv_lora, w_kv_up,
           w_proj_a, w_proj_b, w_proj_c, w_out, final_g):
    batch, seq, d = x.shape
    assert batch == 1 and d == D_MODEL and seq % 1024 == 0
    depth = w_in.shape[0]
    xs = x.reshape(seq, d)
    bias = _swa_bias(rel_bias)
    rope = _rope_table(seq)
    w_all = _layout_w_in(w_in)
    wa, wb, wc, wo = (_cast_weights(w) for w in (w_proj_a, w_proj_b, w_proj_c, w_out))

    for l in range(depth):
        h = _rmsnorm(xs, norm_g[l], MXU_DTYPE)
        p16 = _matmul(h, w_all, l, 0, P16_WIDTH, MXU_DTYPE, tm=1024, tn=768, name="in_proj_bf16")
        pf = _matmul(h, w_all, l, P16_WIDTH, PF_WIDTH, jnp.float32, tm=1024, tn=768,
                     name="in_proj_f32")

        ya = _swa(p16, pf, attn_sinks[l], bias)
        qf, kf, vb = _mla_pre(pf, g_q_lora[l], g_kv_lora[l], _layout_w_q_up(w_q_up[l]),
                              _layout_w_kv_up(w_kv_up[l]), rope)
        yb = _mla(qf, kf, vb, pf)
        yc = _sb(p16, pf)

        merged = _merge(h, w_all, l, ya, yb, yc, wa, wb, wc)
        xs = _matmul(merged, wo, l, 0, D_MODEL, jnp.float32, tm=1024, tn=512, residual=xs,
                     name="out_proj")

    out = _rmsnorm(xs, final_g, jnp.float32)
    return out.reshape(batch, seq, d)
```

```python
import functools
import math

import jax
import jax.numpy as jnp
from jax import lax
from jax.experimental import pallas as pl
from jax.experimental.pallas import tpu as pltpu

D_MODEL = 4096
HEAD_DIM = 128
LANES = 128
BLOCK = 128
EPS = 1e-6
A_HEADS = 8
A_KV_HEADS = 2
A_GROUP = A_HEADS // A_KV_HEADS
A_WIDTH = A_HEADS * HEAD_DIM
A_KV_WIDTH = A_KV_HEADS * HEAD_DIM
NUM_BUCKETS = 32
MAX_DISTANCE = 128
B_HEADS = 4
B_Q_LORA = 1024
B_KV_LORA = 512
B_NOPE = 128
B_ROPE = 64
B_V = 128
B_QK = B_NOPE + B_ROPE
B_QK_PAD = 256
B_WIDTH = B_HEADS * B_V
ROPE_THETA = 10000.0
C_HEADS = 4
C_WIDTH = C_HEADS * HEAD_DIM

MXU_DTYPE = jnp.bfloat16
LOG2E = math.log2(math.e)
VMEM_LIMIT_BYTES = 56 * 1024 * 1024

_IN_SIZES = (A_WIDTH, A_KV_WIDTH, A_KV_WIDTH, A_WIDTH, B_Q_LORA, B_KV_LORA, B_ROPE, B_WIDTH,
             C_WIDTH, C_WIDTH, C_WIDTH, C_WIDTH, D_MODEL, D_MODEL, D_MODEL)
_IN_NAMES = ("qa", "ka", "va", "za", "cq", "ckv", "kr", "zb", "qc", "kc", "vc", "zc", "ga", "gb", "gc")
_IN_OFF = {}
_o = 0
for _n, _s in zip(_IN_NAMES, _IN_SIZES):
    _IN_OFF[_n] = (_o, _s)
    _o += _s

_P16_COLS = (("qa", A_WIDTH), ("ka", A_KV_WIDTH), ("va", A_KV_WIDTH),
             ("qc", C_WIDTH), ("kc", C_WIDTH), ("vc", C_WIDTH))
_PF_COLS = (("za", A_WIDTH), ("cq", B_Q_LORA), ("ckv", B_KV_LORA), ("kr_a", 128), ("kr_b", 128),
            ("zb", B_WIDTH), ("zc", C_WIDTH))


def _offsets(cols):
    out, o = {}, 0
    for n, s in cols:
        out[n] = o
        o += s
    return out, o


_P16_OFF, P16_WIDTH = _offsets(_P16_COLS)
_PF_OFF, PF_WIDTH = _offsets(_PF_COLS)
GATE_OFF = P16_WIDTH + PF_WIDTH


def _cparams(*sem):
    return pltpu.CompilerParams(dimension_semantics=sem, vmem_limit_bytes=VMEM_LIMIT_BYTES)


def _dot(a, b):
    return jnp.dot(a, b, preferred_element_type=jnp.float32)


def _dot_nt(a, b):
    return lax.dot_general(a, b, (((1,), (1,)), ((), ())), preferred_element_type=jnp.float32)


def _silu(z):
    return z * jax.nn.sigmoid(z)


def _rmsnorm_kernel(x_ref, g_ref, o_ref):
    x = x_ref[...]
    y = x * lax.rsqrt(jnp.mean(x * x, axis=-1, keepdims=True) + EPS)
    o_ref[...] = (y * g_ref[...]).astype(o_ref.dtype)


def _rmsnorm(x, g, out_dtype, tm=512):
    s, d = x.shape
    return pl.pallas_call(
        _rmsnorm_kernel,
        grid=(s // tm,),
        in_specs=[pl.BlockSpec((tm, d), lambda i: (i, 0)),
                  pl.BlockSpec((1, d), lambda i: (0, 0))],
        out_specs=pl.BlockSpec((tm, d), lambda i: (i, 0)),
        out_shape=jax.ShapeDtypeStruct((s, d), out_dtype),
        compiler_params=_cparams("parallel"),
        name="rmsnorm",
    )(x, g.reshape(1, d))


def _matmul_kernel(a_ref, w_ref, o_ref):
    o_ref[...] = _dot(a_ref[...], w_ref[...]).astype(o_ref.dtype)


def _matmul_nt_kernel(a_ref, wt_ref, o_ref):
    o_ref[...] = _dot_nt(a_ref[...], wt_ref[...]).astype(o_ref.dtype)


def _matmul_residual_kernel(a_ref, w_ref, r_ref, o_ref):
    o_ref[...] = r_ref[...] + _dot(a_ref[...], w_ref[...])


def _matmul(a, w, layer, col_off, n_cols, out_dtype, tm, tn, residual=None, transposed_w=False,
            name="matmul"):
    s, k = a.shape
    assert s % tm == 0 and n_cols % tn == 0 and col_off % tn == 0
    assert not (transposed_w and residual is not None)
    jo = col_off // tn
    if transposed_w:
        w_spec = pl.BlockSpec((None, tn, k), lambda i, j: (layer, j + jo, 0))
    else:
        w_spec = pl.BlockSpec((None, k, tn), lambda i, j: (layer, 0, j + jo))
    in_specs = [pl.BlockSpec((tm, k), lambda i, j: (i, 0)), w_spec]
    args = [a, w]
    body = _matmul_nt_kernel if transposed_w else _matmul_kernel
    if residual is not None:
        in_specs.append(pl.BlockSpec((tm, tn), lambda i, j: (i, j)))
        args.append(residual)
        body = _matmul_residual_kernel
    return pl.pallas_call(
        body,
        grid=(s // tm, n_cols // tn),
        in_specs=in_specs,
        out_specs=pl.BlockSpec((tm, tn), lambda i, j: (i, j)),
        out_shape=jax.ShapeDtypeStruct((s, n_cols), out_dtype),
        compiler_params=_cparams("parallel", "arbitrary"),
        name=name,
    )(*args)


def _merge_kernel(h_ref, wga_ref, wgb_ref, wgc_ref, ya_ref, yb_ref, yc_ref,
                  wa_ref, wb_ref, wc_ref, o_ref):
    h = h_ref[...]

    def branch(wg_ref, y_ref, w_ref):
        return jax.nn.sigmoid(_dot_nt(h, wg_ref[...])) * _dot(y_ref[...], w_ref[...])

    merged = (branch(wga_ref, ya_ref, wa_ref) + branch(wgb_ref, yb_ref, wb_ref)
              + branch(wgc_ref, yc_ref, wc_ref))
    o_ref[...] = merged.astype(o_ref.dtype)


def _merge(h, w_all, layer, ya, yb, yc, wa, wb, wc, tm=1024, tn=256):
    s = h.shape[0]
    nj = D_MODEL // tn
    g0 = GATE_OFF // tn

    def gate_spec(k):
        return pl.BlockSpec((None, tn, D_MODEL), lambda i, j: (layer, g0 + k * nj + j, 0))

    def row_spec(width):
        return pl.BlockSpec((tm, width), lambda i, j: (i, 0))

    def w_spec(width):
        return pl.BlockSpec((None, width, tn), lambda i, j: (layer, 0, j))

    return pl.pallas_call(
        _merge_kernel,
        grid=(s // tm, nj),
        in_specs=[row_spec(D_MODEL), gate_spec(0), gate_spec(1), gate_spec(2),
                  row_spec(A_WIDTH), row_spec(B_WIDTH), row_spec(C_WIDTH),
                  w_spec(A_WIDTH), w_spec(B_WIDTH), w_spec(C_WIDTH)],
        out_specs=pl.BlockSpec((tm, tn), lambda i, j: (i, j)),
        out_shape=jax.ShapeDtypeStruct((s, D_MODEL), MXU_DTYPE),
        compiler_params=_cparams("parallel", "arbitrary"),
        name="gated_merge",
    )(h, w_all, w_all, w_all, ya, yb, yc, wa, wb, wc)


def _t5_bucket(rel):
    n = jnp.maximum(rel, 0)
    max_exact = NUM_BUCKETS // 2
    logn = jnp.log(jnp.maximum(n, 1).astype(jnp.float32) / max_exact)
    large = max_exact + (logn / math.log(MAX_DISTANCE / max_exact)
                         * (NUM_BUCKETS - max_exact)).astype(jnp.int32)
    large = jnp.minimum(large, NUM_BUCKETS - 1)
    return jnp.where(n < max_exact, n, large)


def _swa_bias_kernel(rel_bias_ref, bucket_ref, o_ref):
    bucket = bucket_ref[...]
    for h in range(A_HEADS):
        acc = jnp.zeros(bucket.shape, jnp.float32)
        for b in range(NUM_BUCKETS):
            acc = jnp.where(bucket == b, rel_bias_ref[b, h], acc)
        o_ref[h] = acc


def _swa_bias(rel_bias):
    t = jnp.arange(BLOCK)[:, None]
    s = jnp.arange(2 * BLOCK)[None, :]
    bucket = _t5_bucket(BLOCK + t - s).astype(jnp.int32)
    return pl.pallas_call(
        _swa_bias_kernel,
        in_specs=[pl.BlockSpec(memory_space=pltpu.SMEM),
                  pl.BlockSpec((BLOCK, 2 * BLOCK), lambda: (0, 0))],
        out_specs=pl.BlockSpec((A_HEADS, BLOCK, 2 * BLOCK), lambda: (0, 0, 0)),
        out_shape=jax.ShapeDtypeStruct((A_HEADS, BLOCK, 2 * BLOCK), jnp.float32),
        name="swa_bias_table",
    )(rel_bias, bucket)


def _swa_kernel(sinks_ref, q_ref, kc_ref, kp_ref, vc_ref, vp_ref, bias_ref, z_ref, o_ref):
    n = pl.program_id(0)
    rows = A_GROUP * BLOCK
    t = lax.broadcasted_iota(jnp.int32, (rows, 2 * BLOCK), 0) & (BLOCK - 1)
    s = lax.broadcasted_iota(jnp.int32, (rows, 2 * BLOCK), 1)
    ok = (s > jnp.maximum(t, jnp.where(n > 0, -1, BLOCK - 1))) & (s <= t + BLOCK)
    scale = HEAD_DIM ** -0.5

    def head_cols(h):
        return slice(h * HEAD_DIM, (h + 1) * HEAD_DIM)

    scores = []
    for hk in range(A_KV_HEADS):
        q = jnp.concatenate([q_ref[:, head_cols(hk * A_GROUP + g)] for g in range(A_GROUP)], axis=0)
        kk = jnp.concatenate([kp_ref[:, head_cols(hk)], kc_ref[:, head_cols(hk)]], axis=0)
        scores.append(_dot_nt(q, kk))
    probs = []
    for hk in range(A_KV_HEADS):
        sc = jnp.where(ok, scores[hk] * scale + bias_ref[hk], -jnp.inf)
        sink = jnp.concatenate([jnp.full((BLOCK, LANES), sinks_ref[hk * A_GROUP + g], jnp.float32)
                                for g in range(A_GROUP)], axis=0)
        m = jnp.maximum(jnp.max(sc, axis=-1, keepdims=True), sink)
        e = jnp.exp(sc - jnp.concatenate([m, m], axis=-1))
        inv = 1.0 / (jnp.sum(e, axis=-1, keepdims=True) + jnp.exp(sink - m))
        probs.append((e * jnp.concatenate([inv, inv], axis=-1)).astype(MXU_DTYPE))
    for hk in range(A_KV_HEADS):
        vv = jnp.concatenate([vp_ref[:, head_cols(hk)], vc_ref[:, head_cols(hk)]], axis=0)
        out = _dot(probs[hk], vv)
        for g in range(A_GROUP):
            cols = head_cols(hk * A_GROUP + g)
            o_ref[:, cols] = (out[g * BLOCK:(g + 1) * BLOCK] * _silu(z_ref[:, cols])).astype(o_ref.dtype)


def _swa(p16, pf, sinks, bias):
    s = p16.shape[0]
    qb = _P16_OFF["qa"] // A_WIDTH
    kb = _P16_OFF["ka"] // A_KV_WIDTH
    vb = _P16_OFF["va"] // A_KV_WIDTH
    zb = _PF_OFF["za"] // A_WIDTH
    prev = lambda n: jnp.maximum(n - 1, 0)
    return pl.pallas_call(
        _swa_kernel,
        grid=(s // BLOCK,),
        in_specs=[pl.BlockSpec(memory_space=pltpu.SMEM),
                  pl.BlockSpec((BLOCK, A_WIDTH), lambda n: (n, qb)),
                  pl.BlockSpec((BLOCK, A_KV_WIDTH), lambda n: (n, kb)),
                  pl.BlockSpec((BLOCK, A_KV_WIDTH), lambda n: (prev(n), kb)),
                  pl.BlockSpec((BLOCK, A_KV_WIDTH), lambda n: (n, vb)),
                  pl.BlockSpec((BLOCK, A_KV_WIDTH), lambda n: (prev(n), vb)),
                  pl.BlockSpec((A_KV_HEADS, A_GROUP * BLOCK, 2 * BLOCK), lambda n: (0, 0, 0)),
                  pl.BlockSpec((BLOCK, A_WIDTH), lambda n: (n, zb))],
        out_specs=pl.BlockSpec((BLOCK, A_WIDTH), lambda n: (n, 0)),
        out_shape=jax.ShapeDtypeStruct((s, A_WIDTH), MXU_DTYPE),
        compiler_params=_cparams("parallel"),
        name="swa_attention",
    )(sinks, p16, p16, p16, p16, p16, bias.reshape(A_KV_HEADS, A_GROUP * BLOCK, 2 * BLOCK), pf)


def _rms_cast(x, g):
    y = x * lax.rsqrt(jnp.mean(x * x, axis=-1, keepdims=True) + EPS)
    return (y * g).astype(MXU_DTYPE)


def _mla_pre_kernel(cq_ref, ckv_ref, kr_ref, gq_ref, gkv_ref, wq_ref, wkv_ref, rope_ref,
                    qf_ref, kf_ref, v_ref):
    tc = rope_ref[:, 0:128]
    ts = rope_ref[:, 128:256]
    qraw = _dot(_rms_cast(cq_ref[...], gq_ref[...]), wq_ref[...])
    for h in range(B_HEADS):
        b = 3 * 128 * h
        qf_ref[:, h * B_QK_PAD:h * B_QK_PAD + B_NOPE] = qraw[:, b:b + 128].astype(qf_ref.dtype)
        rot = qraw[:, b + 128:b + 256] * tc + qraw[:, b + 256:b + 384] * ts
        qf_ref[:, h * B_QK_PAD + B_NOPE:(h + 1) * B_QK_PAD] = rot.astype(qf_ref.dtype)
    kvraw = _dot(_rms_cast(ckv_ref[...], gkv_ref[...]), wkv_ref[...])
    kr = kr_ref[...]
    krot = (kr[:, 0:128] * tc + kr[:, 128:256] * ts).astype(kf_ref.dtype)
    for h in range(B_HEADS):
        kf_ref[:, h * B_QK_PAD:h * B_QK_PAD + B_NOPE] = (
            kvraw[:, h * B_NOPE:(h + 1) * B_NOPE].astype(kf_ref.dtype))
        kf_ref[:, h * B_QK_PAD + B_NOPE:(h + 1) * B_QK_PAD] = krot
    v_ref[...] = kvraw[:, B_HEADS * B_NOPE:].astype(v_ref.dtype)


def _mla_pre(pf, gq, gkv, wq, wkv, rope, tm=512):
    s = pf.shape[0]
    row = lambda width, off: pl.BlockSpec((tm, width), lambda i: (i, off // width))
    full = lambda a: pl.BlockSpec(a.shape, lambda i: (0, 0))
    gq = gq.reshape(1, B_Q_LORA)
    gkv = gkv.reshape(1, B_KV_LORA)
    out_row = lambda width: pl.BlockSpec((tm, width), lambda i: (i, 0))
    return pl.pallas_call(
        _mla_pre_kernel,
        grid=(s // tm,),
        in_specs=[row(B_Q_LORA, _PF_OFF["cq"]), row(B_KV_LORA, _PF_OFF["ckv"]),
                  row(256, _PF_OFF["kr_a"]), full(gq), full(gkv), full(wq), full(wkv),
                  pl.BlockSpec((tm, 256), lambda i: (i, 0))],
        out_specs=[out_row(B_HEADS * B_QK_PAD), out_row(B_HEADS * B_QK_PAD), out_row(B_WIDTH)],
        out_shape=[jax.ShapeDtypeStruct((s, B_HEADS * B_QK_PAD), MXU_DTYPE),
                   jax.ShapeDtypeStruct((s, B_HEADS * B_QK_PAD), MXU_DTYPE),
                   jax.ShapeDtypeStruct((s, B_WIDTH), MXU_DTYPE)],
        compiler_params=_cparams("parallel"),
        name="mla_pre",
    )(pf, pf, pf, gq, gkv, wq, wkv, rope)


def _run_pipeline(stages, n, n_static, unroll):
    depth = len(stages)
    t0 = n_static + 1
    assert unroll % 2 == 0 and depth >= 2 and t0 >= depth - 1

    def run(k, pos, slot, static_pos=None):
        stages[k](pos, slot, static_pos)

    for tt in range(t0):
        for k in reversed(range(depth)):
            pos = tt - k
            if pos < 0:
                continue
            if pos < n_static:
                run(k, pos, pos % 2, pos)
            else:
                pl.when(n > pos)(functools.partial(run, k, pos, pos % 2))

    def steady(tt, slot):
        for k in reversed(range(depth)):
            run(k, tt - k, (slot + k) % 2)

    def body(u, carry):
        for j in range(unroll):
            steady(t0 + unroll * u + j, (t0 + j) % 2)
        return carry

    trips = jnp.maximum(n - t0, 0) // unroll
    lax.fori_loop(0, trips, body, 0)
    start = t0 + unroll * trips
    left = jnp.maximum(n - start, 0)
    width = unroll // 2
    while width >= 1:

        def block(start=start, width=width):
            for j in range(width):
                steady(start + j, (t0 + j) % 2)

        pl.when((left & width) != 0)(block)
        start = start + (left & width)
        width //= 2

    def drain(first):
        for e in range(0 if first else 1, depth - 1):
            for k in reversed(range(e + 1, depth)):
                run(k, n + e - k, (n + e - k) % 2)

    pl.when(n >= t0)(functools.partial(drain, True))
    pl.when(n < t0)(functools.partial(drain, False))


MLA_UNROLL = 4


def _mla_kernel(q_ref, k_ref, v_ref, z_ref, o_ref,
                y_scr, p_scr, alpha_scr, m_scr, l_scr, acc_scr, *, tq, tk):
    n_diag = tq // tk
    i = pl.program_id(1)
    q = q_ref[...]
    c = (B_QK ** -0.5) * LOG2E

    def key_rows(pos):
        kb = jnp.where(pos < n_diag, n_diag * i + pos, pos - n_diag)
        return pl.ds(pl.multiple_of(kb * tk, tk), tk)

    def scores(pos, slot, static_pos):
        y_scr[slot] = _dot_nt(q, k_ref[key_rows(pos), :]) * c

    def softmax(pos, slot, static_pos):
        y = y_scr[slot]
        if static_pos is not None:
            row = lax.broadcasted_iota(jnp.int32, (tq, tk), 0)
            col = lax.broadcasted_iota(jnp.int32, (tq, tk), 1) + static_pos * tk
            y = jnp.where(col <= row, y, -jnp.inf)
        m = m_scr[...]
        m_new = jnp.maximum(m, jnp.max(y, axis=-1, keepdims=True))
        alpha = jnp.exp2(m - m_new)
        p = jnp.exp2(y - jnp.concatenate([m_new] * (tk // LANES), axis=-1))
        l_scr[...] = alpha * l_scr[...] + jnp.sum(p, axis=-1, keepdims=True)
        m_scr[...] = m_new
        alpha_scr[slot] = alpha
        p_scr[slot] = p.astype(MXU_DTYPE)

    def accumulate(pos, slot, static_pos):
        acc_scr[...] = alpha_scr[slot] * acc_scr[...] + _dot(p_scr[slot], v_ref[key_rows(pos), :])

    m_scr[...] = jnp.full((tq, LANES), -jnp.inf, jnp.float32)
    l_scr[...] = jnp.zeros((tq, LANES), jnp.float32)
    acc_scr[...] = jnp.zeros((tq, B_V), jnp.float32)
    _run_pipeline([scores, softmax, accumulate], n_diag * (i + 1), n_diag, MLA_UNROLL)
    o_ref[...] = ((acc_scr[...] / l_scr[...]) * _silu(z_ref[...])).astype(o_ref.dtype)


def _mla(qf, kf, v, pf, tq=1024, tk=512):
    s = qf.shape[0]
    zb = _PF_OFF["zb"] // B_V
    return pl.pallas_call(
        functools.partial(_mla_kernel, tq=tq, tk=tk),
        grid=(B_HEADS, s // tq),
        in_specs=[pl.BlockSpec((tq, B_QK_PAD), lambda h, i: (i, h)),
                  pl.BlockSpec((s, B_QK_PAD), lambda h, i: (0, h)),
                  pl.BlockSpec((s, B_V), lambda h, i: (0, h)),
                  pl.BlockSpec((tq, B_V), lambda h, i: (i, zb + h))],
        out_specs=pl.BlockSpec((tq, B_V), lambda h, i: (i, h)),
        out_shape=jax.ShapeDtypeStruct((s, B_WIDTH), MXU_DTYPE),
        scratch_shapes=[pltpu.VMEM((2, tq, tk), jnp.float32),
                        pltpu.VMEM((2, tq, tk), MXU_DTYPE),
                        pltpu.VMEM((2, tq, LANES), jnp.float32),
                        pltpu.VMEM((tq, LANES), jnp.float32),
                        pltpu.VMEM((tq, LANES), jnp.float32),
                        pltpu.VMEM((tq, B_V), jnp.float32)],
        compiler_params=_cparams("parallel", "arbitrary"),
        name="mla_attention",
    )(qf, kf, v, pf)


SB_UNROLL = 4


def _sb_kernel(q_ref, k_ref, v_ref, z_ref, uu_ref, o_ref,
               s_scr, hl_scr, lb_scr, off_scr, off_run_scr, w_scr, acc_scr, *, tq, tk):
    n_diag = tq // tk
    i = pl.program_id(1)
    q = q_ref[...]
    scale = HEAD_DIM ** -0.5
    n = n_diag * (i + 1)

    def key_rows(pos):
        return pl.ds(pl.multiple_of((n - 1 - pos) * tk, tk), tk)

    def score(pos, slot, static_pos):
        s_scr[slot] = _dot_nt(q, k_ref[key_rows(pos), :])

    def park(pos, slot, static_pos):
        y = s_scr[slot] * (scale * LOG2E)
        l1p = jnp.log2(1.0 + jnp.exp2(-jnp.abs(y)))
        drop = jnp.maximum(y, 0.0) + l1p
        log_beta = y - drop
        if static_pos is not None:
            row = lax.broadcasted_iota(jnp.int32, (tq, tk), 0)
            col = lax.broadcasted_iota(jnp.int32, (tq, tk), 1) + (n_diag - 1 - static_pos) * tk
            past = col < row
            drop = jnp.where(past, drop, 0.0)
            log_beta = jnp.where(past, log_beta, -jnp.inf)
        hi = drop.astype(MXU_DTYPE)
        lo = (drop - hi.astype(jnp.float32)).astype(MXU_DTYPE)
        hl_scr[slot] = jnp.concatenate([hi, lo], axis=-1)
        lb_scr[slot] = log_beta
        off = off_run_scr[...]
        off_scr[slot] = off
        off_run_scr[...] = off + jnp.sum(drop, axis=-1, keepdims=True)

    def weigh(pos, slot, static_pos):
        within = _dot(hl_scr[slot], uu_ref[...])
        off = jnp.concatenate([off_scr[slot]] * (tk // LANES), axis=-1)
        w_scr[slot] = jnp.exp2(lb_scr[slot] - (within + off)).astype(MXU_DTYPE)

    def accumulate(pos, slot, static_pos):
        acc_scr[...] += _dot(w_scr[slot], v_ref[key_rows(pos), :])

    off_run_scr[...] = jnp.zeros((tq, LANES), jnp.float32)
    acc_scr[...] = jnp.zeros((tq, HEAD_DIM), jnp.float32)
    _run_pipeline([score, park, weigh, accumulate], n, n_diag, SB_UNROLL)
    o_ref[...] = (acc_scr[...] * _silu(z_ref[...])).astype(o_ref.dtype)


def _sb(p16, pf, tq=1024, tk=256):
    s = p16.shape[0]
    qb = _P16_OFF["qc"] // HEAD_DIM
    kb = _P16_OFF["kc"] // HEAD_DIM
    vb = _P16_OFF["vc"] // HEAD_DIM
    zb = _PF_OFF["zc"] // HEAD_DIM
    j = lax.broadcasted_iota(jnp.int32, (tk, tk), 0)
    c = lax.broadcasted_iota(jnp.int32, (tk, tk), 1)
    u = (j > c).astype(MXU_DTYPE)
    uu = jnp.concatenate([u, u], axis=0)
    return pl.pallas_call(
        functools.partial(_sb_kernel, tq=tq, tk=tk),
        grid=(C_HEADS, s // tq),
        in_specs=[pl.BlockSpec((tq, HEAD_DIM), lambda h, i: (i, qb + h)),
                  pl.BlockSpec((s, HEAD_DIM), lambda h, i: (0, kb + h)),
                  pl.BlockSpec((s, HEAD_DIM), lambda h, i: (0, vb + h)),
                  pl.BlockSpec((tq, HEAD_DIM), lambda h, i: (i, zb + h)),
                  pl.BlockSpec((2 * tk, tk), lambda h, i: (0, 0))],
        out_specs=pl.BlockSpec((tq, HEAD_DIM), lambda h, i: (i, h)),
        out_shape=jax.ShapeDtypeStruct((s, C_WIDTH), MXU_DTYPE),
        scratch_shapes=[pltpu.VMEM((2, tq, tk), jnp.float32),
                        pltpu.VMEM((2, tq, 2 * tk), MXU_DTYPE),
                        pltpu.VMEM((2, tq, tk), jnp.float32),
                        pltpu.VMEM((2, tq, LANES), jnp.float32),
                        pltpu.VMEM((tq, LANES), jnp.float32),
                        pltpu.VMEM((2, tq, tk), MXU_DTYPE),
                        pltpu.VMEM((tq, HEAD_DIM), jnp.float32)],
        compiler_params=_cparams("parallel", "arbitrary"),
        name="stick_breaking_attention",
    )(p16, p16, p16, pf, uu)


def _swap_halves(w):
    half = w.shape[-1] // 2
    return jnp.concatenate([w[..., half:], w[..., :half]], axis=-1)


def _pad_cols(w, width):
    return jnp.pad(w, ((0, 0), (0, width - w.shape[-1])))


_GATE_COLS = (("ga", D_MODEL), ("gb", D_MODEL), ("gc", D_MODEL))
W_ALL_WIDTH = GATE_OFF + 3 * D_MODEL


W_IN_ROWS = B_ROPE
_COPY, _SWAP_HALVES, _ZEROS = 0, 1, 2


def _w_in_layout_table():
    src, code = [], []
    kr_block = _IN_OFF["kr"][0] // W_IN_ROWS
    for name, width in _P16_COLS + _PF_COLS + _GATE_COLS:
        if name in ("kr_a", "kr_b"):
            src += [kr_block, kr_block]
            code += [_COPY if name == "kr_a" else _SWAP_HALVES, _ZEROS]
        else:
            a = _IN_OFF[name][0]
            assert a % W_IN_ROWS == 0 and width % W_IN_ROWS == 0
            src += [a // W_IN_ROWS + t for t in range(width // W_IN_ROWS)]
            code += [_COPY] * (width // W_IN_ROWS)
    return jnp.asarray(src, jnp.int32), jnp.asarray(code, jnp.int32)


def _w_in_layout_kernel(src_ref, code_ref, w_ref, o_ref):
    code = code_ref[pl.program_id(1)]
    x = w_ref[...]
    half = W_IN_ROWS // 2
    x = jnp.where(code == _SWAP_HALVES, jnp.concatenate([x[half:], x[:half]], axis=0), x)
    x = jnp.where(code == _ZEROS, 0.0, x)
    o_ref[...] = x.astype(o_ref.dtype)


def _layout_w_in(w_in_t):
    depth, _, d = w_in_t.shape
    src, code = _w_in_layout_table()
    return pl.pallas_call(
        _w_in_layout_kernel,
        grid_spec=pltpu.PrefetchScalarGridSpec(
            num_scalar_prefetch=2,
            grid=(depth, W_ALL_WIDTH // W_IN_ROWS),
            in_specs=[pl.BlockSpec((None, W_IN_ROWS, d), lambda l, r, src, code: (l, src[r], 0))],
            out_specs=pl.BlockSpec((None, W_IN_ROWS, d), lambda l, r, src, code: (l, r, 0))),
        out_shape=jax.ShapeDtypeStruct((depth, W_ALL_WIDTH, d), MXU_DTYPE),
        compiler_params=_cparams("parallel", "arbitrary"),
        name="w_in_layout",
    )(src, code, w_in_t)


def _cast_kernel(w_ref, o_ref):
    o_ref[...] = w_ref[...].astype(o_ref.dtype)


def _cast_weights(w, tr=512):
    depth, rows, cols = w.shape
    tr = min(tr, rows)
    spec = pl.BlockSpec((1, tr, cols), lambda l, r: (l, r, 0))
    return pl.pallas_call(
        _cast_kernel,
        grid=(depth, rows // tr),
        in_specs=[spec],
        out_specs=spec,
        out_shape=jax.ShapeDtypeStruct(w.shape, MXU_DTYPE),
        compiler_params=_cparams("parallel", "parallel"),
        name="cast_weights",
    )(w)


def _layout_w_q_up(w):
    parts = []
    for h in range(B_HEADS):
        nope = w[:, h * B_QK:h * B_QK + B_NOPE]
        rope = w[:, h * B_QK + B_NOPE:(h + 1) * B_QK]
        parts += [nope, _pad_cols(rope, 128), _pad_cols(_swap_halves(rope), 128)]
    return jnp.concatenate(parts, axis=1).astype(MXU_DTYPE)


def _layout_w_kv_up(w):
    per = B_NOPE + B_V
    ks = [w[:, h * per:h * per + B_NOPE] for h in range(B_HEADS)]
    vs = [w[:, h * per + B_NOPE:(h + 1) * per] for h in range(B_HEADS)]
    return jnp.concatenate(ks + vs, axis=1).astype(MXU_DTYPE)


def _rope_table(seq):
    pos = jnp.arange(seq, dtype=jnp.float32)
    inv = ROPE_THETA ** (-jnp.arange(0, B_ROPE, 2, dtype=jnp.float32) / B_ROPE)
    ang = pos[:, None] * inv[None, :]
    cos, sin = jnp.cos(ang), jnp.sin(ang)
    zeros = jnp.zeros((seq, 128 - B_ROPE), jnp.float32)
    return jnp.concatenate([cos, cos, zeros, -sin, sin, zeros], axis=1)


def kernel(x, norm_g, w_in, attn_sinks, rel_bias, g_q_lora, w_q_up, g_kv_lora, w_kv_up,
           w_proj_a, w_proj_b, w_proj_c, w_out, final_g):
    batch, seq, d = x.shape
    assert batch == 1 and d == D_MODEL and seq % 1024 == 0
    depth = w_in.shape[0]
    xs = x.reshape(seq, d)
    bias = _swa_bias(rel_bias)
    rope = _rope_table(seq)
    w_all = _layout_w_in(jnp.swapaxes(w_in, 1, 2))
    wa, wb, wc, wo = (_cast_weights(w) for w in (w_proj_a, w_proj_b, w_proj_c, w_out))

    for l in range(depth):
        h = _rmsnorm(xs, norm_g[l], MXU_DTYPE)
        p16 = _matmul(h, w_all, l, 0, P16_WIDTH, MXU_DTYPE, tm=1024, tn=768, transposed_w=True,
                      name="in_proj_bf16")
        pf = _matmul(h, w_all, l, P16_WIDTH, PF_WIDTH, jnp.float32, tm=1024, tn=768,
                     transposed_w=True, name="in_proj_f32")

        ya = _swa(p16, pf, attn_sinks[l], bias)
        qf, kf, vb = _mla_pre(pf, g_q_lora[l], g_kv_lora[l], _layout_w_q_up(w_q_up[l]),
                              _layout_w_kv_up(w_kv_up[l]), rope)
        yb = _mla(qf, kf, vb, pf)
        yc = _sb(p16, pf)

        merged = _merge(h, w_all, l, ya, yb, yc, wa, wb, wc)
        xs = _matmul(merged, wo, l, 0, D_MODEL, jnp.float32, tm=1024, tn=512, residual=xs,
                     name="out_proj")

    out = _rmsnorm(xs, final_g, jnp.float32)
    return out.reshape(batch, seq, d)
```

```python
import functools
import math

import jax
import jax.numpy as jnp
from jax import lax
from jax.experimental import pallas as pl
from jax.experimental.pallas import tpu as pltpu

D_MODEL = 4096
HEAD_DIM = 128
LANES = 128
BLOCK = 128
EPS = 1e-6
A_HEADS = 8
A_KV_HEADS = 2
A_GROUP = A_HEADS // A_KV_HEADS
A_WIDTH = A_HEADS * HEAD_DIM
A_KV_WIDTH = A_KV_HEADS * HEAD_DIM
NUM_BUCKETS = 32
MAX_DISTANCE = 128
B_HEADS = 4
B_Q_LORA = 1024
B_KV_LORA = 512
B_NOPE = 128
B_ROPE = 64
B_V = 128
B_QK = B_NOPE + B_ROPE
B_QK_PAD = 256
B_WIDTH = B_HEADS * B_V
ROPE_THETA = 10000.0
C_HEADS = 4
C_WIDTH = C_HEADS * HEAD_DIM

MXU_DTYPE = jnp.bfloat16
LOG2E = math.log2(math.e)
VMEM_LIMIT_BYTES = 56 * 1024 * 1024

_IN_SIZES = (A_WIDTH, A_KV_WIDTH, A_KV_WIDTH, A_WIDTH, B_Q_LORA, B_KV_LORA, B_ROPE, B_WIDTH,
             C_WIDTH, C_WIDTH, C_WIDTH, C_WIDTH, D_MODEL, D_MODEL, D_MODEL)
_IN_NAMES = ("qa", "ka", "va", "za", "cq", "ckv", "kr", "zb", "qc", "kc", "vc", "zc", "ga", "gb", "gc")
_IN_OFF = {}
_o = 0
for _n, _s in zip(_IN_NAMES, _IN_SIZES):
    _IN_OFF[_n] = (_o, _s)
    _o += _s

_P16_COLS = (("qa", A_WIDTH), ("ka", A_KV_WIDTH), ("va", A_KV_WIDTH),
             ("qc", C_WIDTH), ("kc", C_WIDTH), ("vc", C_WIDTH))
_PF_COLS = (("za", A_WIDTH), ("cq", B_Q_LORA), ("ckv", B_KV_LORA), ("kr_a", 128), ("kr_b", 128),
            ("zb", B_WIDTH), ("zc", C_WIDTH))


def _offsets(cols):
    out, o = {}, 0
    for n, s in cols:
        out[n] = o
        o += s
    return out, o


_P16_OFF, P16_WIDTH = _offsets(_P16_COLS)
_PF_OFF, PF_WIDTH = _offsets(_PF_COLS)
GATE_OFF = P16_WIDTH + PF_WIDTH


def _cparams(*sem):
    return pltpu.CompilerParams(dimension_semantics=sem, vmem_limit_bytes=VMEM_LIMIT_BYTES)


def _dot(a, b):
    return jnp.dot(a, b, preferred_element_type=jnp.float32)


def _dot_nt(a, b):
    return lax.dot_general(a, b, (((1,), (1,)), ((), ())), preferred_element_type=jnp.float32)


def _silu(z):
    return z * jax.nn.sigmoid(z)


def _rmsnorm_kernel(x_ref, g_ref, o_ref):
    x = x_ref[...]
    y = x * lax.rsqrt(jnp.mean(x * x, axis=-1, keepdims=True) + EPS)
    o_ref[...] = (y * g_ref[...]).astype(o_ref.dtype)


def _rmsnorm(x, g, out_dtype, tm=512):
    s, d = x.shape
    return pl.pallas_call(
        _rmsnorm_kernel,
        grid=(s // tm,),
        in_specs=[pl.BlockSpec((tm, d), lambda i: (i, 0)),
                  pl.BlockSpec((1, d), lambda i: (0, 0))],
        out_specs=pl.BlockSpec((tm, d), lambda i: (i, 0)),
        out_shape=jax.ShapeDtypeStruct((s, d), out_dtype),
        compiler_params=_cparams("parallel"),
        name="rmsnorm",
    )(x, g.reshape(1, d))


def _matmul_kernel(a_ref, w_ref, o_ref):
    o_ref[...] = _dot(a_ref[...], w_ref[...]).astype(o_ref.dtype)


def _matmul_nt_kernel(a_ref, wt_ref, o_ref):
    o_ref[...] = _dot_nt(a_ref[...], wt_ref[...]).astype(o_ref.dtype)


def _matmul_residual_kernel(a_ref, w_ref, r_ref, o_ref):
    o_ref[...] = r_ref[...] + _dot(a_ref[...], w_ref[...])


def _matmul(a, w, layer, col_off, n_cols, out_dtype, tm, tn, residual=None, transposed_w=False,
            name="matmul"):
    s, k = a.shape
    assert s % tm == 0 and n_cols % tn == 0 and col_off % tn == 0
    assert not (transposed_w and residual is not None)
    jo = col_off // tn
    if transposed_w:
        w_spec = pl.BlockSpec((None, tn, k), lambda i, j: (layer, j + jo, 0))
    else:
        w_spec = pl.BlockSpec((None, k, tn), lambda i, j: (layer, 0, j + jo))
    in_specs = [pl.BlockSpec((tm, k), lambda i, j: (i, 0)), w_spec]
    args = [a, w]
    body = _matmul_nt_kernel if transposed_w else _matmul_kernel
    if residual is not None:
        in_specs.append(pl.BlockSpec((tm, tn), lambda i, j: (i, j)))
        args.append(residual)
        body = _matmul_residual_kernel
    return pl.pallas_call(
        body,
        grid=(s // tm, n_cols // tn),
        in_specs=in_specs,
        out_specs=pl.BlockSpec((tm, tn), lambda i, j: (i, j)),
        out_shape=jax.ShapeDtypeStruct((s, n_cols), out_dtype),
        compiler_params=_cparams("parallel", "arbitrary"),
        name=name,
    )(*args)


def _merge_kernel(h_ref, wga_ref, wgb_ref, wgc_ref, ya_ref, yb_ref, yc_ref,
                  wa_ref, wb_ref, wc_ref, o_ref):
    h = h_ref[...]

    def branch(wg_ref, y_ref, w_ref):
        return jax.nn.sigmoid(_dot_nt(h, wg_ref[...])) * _dot(y_ref[...], w_ref[...])

    merged = (branch(wga_ref, ya_ref, wa_ref) + branch(wgb_ref, yb_ref, wb_ref)
              + branch(wgc_ref, yc_ref, wc_ref))
    o_ref[...] = merged.astype(o_ref.dtype)


def _merge(h, w_all, layer, ya, yb, yc, wa, wb, wc, tm=1024, tn=256):
    s = h.shape[0]
    nj = D_MODEL // tn
    g0 = GATE_OFF // tn

    def gate_spec(k):
        return pl.BlockSpec((None, tn, D_MODEL), lambda i, j: (layer, g0 + k * nj + j, 0))

    def row_spec(width):
        return pl.BlockSpec((tm, width), lambda i, j: (i, 0))

    def w_spec(width):
        return pl.BlockSpec((None, width, tn), lambda i, j: (layer, 0, j))

    return pl.pallas_call(
        _merge_kernel,
        grid=(s // tm, nj),
        in_specs=[row_spec(D_MODEL), gate_spec(0), gate_spec(1), gate_spec(2),
                  row_spec(A_WIDTH), row_spec(B_WIDTH), row_spec(C_WIDTH),
                  w_spec(A_WIDTH), w_spec(B_WIDTH), w_spec(C_WIDTH)],
        out_specs=pl.BlockSpec((tm, tn), lambda i, j: (i, j)),
        out_shape=jax.ShapeDtypeStruct((s, D_MODEL), MXU_DTYPE),
        compiler_params=_cparams("parallel", "arbitrary"),
        name="gated_merge",
    )(h, w_all, w_all, w_all, ya, yb, yc, wa, wb, wc)


def _t5_bucket(rel):
    n = jnp.maximum(rel, 0)
    max_exact = NUM_BUCKETS // 2
    logn = jnp.log(jnp.maximum(n, 1).astype(jnp.float32) / max_exact)
    large = max_exact + (logn / math.log(MAX_DISTANCE / max_exact)
                         * (NUM_BUCKETS - max_exact)).astype(jnp.int32)
    large = jnp.minimum(large, NUM_BUCKETS - 1)
    return jnp.where(n < max_exact, n, large)


def _swa_bias_kernel(rel_bias_ref, bucket_ref, o_ref):
    bucket = bucket_ref[...]
    for h in range(A_HEADS):
        acc = jnp.zeros(bucket.shape, jnp.float32)
        for b in range(NUM_BUCKETS):
            acc = jnp.where(bucket == b, rel_bias_ref[b, h], acc)
        o_ref[h] = acc


def _swa_bias(rel_bias):
    t = jnp.arange(BLOCK)[:, None]
    s = jnp.arange(2 * BLOCK)[None, :]
    bucket = _t5_bucket(BLOCK + t - s).astype(jnp.int32)
    return pl.pallas_call(
        _swa_bias_kernel,
        in_specs=[pl.BlockSpec(memory_space=pltpu.SMEM),
                  pl.BlockSpec((BLOCK, 2 * BLOCK), lambda: (0, 0))],
        out_specs=pl.BlockSpec((A_HEADS, BLOCK, 2 * BLOCK), lambda: (0, 0, 0)),
        out_shape=jax.ShapeDtypeStruct((A_HEADS, BLOCK, 2 * BLOCK), jnp.float32),
        name="swa_bias_table",
    )(rel_bias, bucket)


def _swa_kernel(sinks_ref, q_ref, kc_ref, kp_ref, vc_ref, vp_ref, bias_ref, z_ref, o_ref):
    n = pl.program_id(0)
    rows = A_GROUP * BLOCK
    t = lax.broadcasted_iota(jnp.int32, (rows, 2 * BLOCK), 0) & (BLOCK - 1)
    s = lax.broadcasted_iota(jnp.int32, (rows, 2 * BLOCK), 1)
    ok = (s > jnp.maximum(t, jnp.where(n > 0, -1, BLOCK - 1))) & (s <= t + BLOCK)
    scale = HEAD_DIM ** -0.5

    def head_cols(h):
        return slice(h * HEAD_DIM, (h + 1) * HEAD_DIM)

    scores = []
    for hk in range(A_KV_HEADS):
        q = jnp.concatenate([q_ref[:, head_cols(hk * A_GROUP + g)] for g in range(A_GROUP)], axis=0)
        kk = jnp.concatenate([kp_ref[:, head_cols(hk)], kc_ref[:, head_cols(hk)]], axis=0)
        scores.append(_dot_nt(q, kk))
    probs = []
    for hk in range(A_KV_HEADS):
        sc = jnp.where(ok, scores[hk] * scale + bias_ref[hk], -jnp.inf)
        sink = jnp.concatenate([jnp.full((BLOCK, LANES), sinks_ref[hk * A_GROUP + g], jnp.float32)
                                for g in range(A_GROUP)], axis=0)
        m = jnp.maximum(jnp.max(sc, axis=-1, keepdims=True), sink)
        e = jnp.exp(sc - jnp.concatenate([m, m], axis=-1))
        inv = 1.0 / (jnp.sum(e, axis=-1, keepdims=True) + jnp.exp(sink - m))
        probs.append((e * jnp.concatenate([inv, inv], axis=-1)).astype(MXU_DTYPE))
    for hk in range(A_KV_HEADS):
        vv = jnp.concatenate([vp_ref[:, head_cols(hk)], vc_ref[:, head_cols(hk)]], axis=0)
        out = _dot(probs[hk], vv)
        for g in range(A_GROUP):
            cols = head_cols(hk * A_GROUP + g)
            o_ref[:, cols] = (out[g * BLOCK:(g + 1) * BLOCK] * _silu(z_ref[:, cols])).astype(o_ref.dtype)


def _swa(p16, pf, sinks, bias):
    s = p16.shape[0]
    qb = _P16_OFF["qa"] // A_WIDTH
    kb = _P16_OFF["ka"] // A_KV_WIDTH
    vb = _P16_OFF["va"] // A_KV_WIDTH
    zb = _PF_OFF["za"] // A_WIDTH
    prev = lambda n: jnp.maximum(n - 1, 0)
    return pl.pallas_call(
        _swa_kernel,
        grid=(s // BLOCK,),
        in_specs=[pl.BlockSpec(memory_space=pltpu.SMEM),
                  pl.BlockSpec((BLOCK, A_WIDTH), lambda n: (n, qb)),
                  pl.BlockSpec((BLOCK, A_KV_WIDTH), lambda n: (n, kb)),
                  pl.BlockSpec((BLOCK, A_KV_WIDTH), lambda n: (prev(n), kb)),
                  pl.BlockSpec((BLOCK, A_KV_WIDTH), lambda n: (n, vb)),
                  pl.BlockSpec((BLOCK, A_KV_WIDTH), lambda n: (prev(n), vb)),
                  pl.BlockSpec((A_KV_HEADS, A_GROUP * BLOCK, 2 * BLOCK), lambda n: (0, 0, 0)),
                  pl.BlockSpec((BLOCK, A_WIDTH), lambda n: (n, zb))],
        out_specs=pl.BlockSpec((BLOCK, A_WIDTH), lambda n: (n, 0)),
        out_shape=jax.ShapeDtypeStruct((s, A_WIDTH), MXU_DTYPE),
        compiler_params=_cparams("parallel"),
        name="swa_attention",
    )(sinks, p16, p16, p16, p16, p16, bias.reshape(A_KV_HEADS, A_GROUP * BLOCK, 2 * BLOCK), pf)


def _rms_cast(x, g):
    y = x * lax.rsqrt(jnp.mean(x * x, axis=-1, keepdims=True) + EPS)
    return (y * g).astype(MXU_DTYPE)


def _mla_pre_kernel(cq_ref, ckv_ref, kr_ref, gq_ref, gkv_ref, wq_ref, wkv_ref, rope_ref,
                    qf_ref, kf_ref, v_ref):
    tc = rope_ref[:, 0:128]
    ts = rope_ref[:, 128:256]
    qraw = _dot(_rms_cast(cq_ref[...], gq_ref[...]), wq_ref[...])
    for h in range(B_HEADS):
        b = 3 * 128 * h
        qf_ref[:, h * B_QK_PAD:h * B_QK_PAD + B_NOPE] = qraw[:, b:b + 128].astype(qf_ref.dtype)
        rot = qraw[:, b + 128:b + 256] * tc + qraw[:, b + 256:b + 384] * ts
        qf_ref[:, h * B_QK_PAD + B_NOPE:(h + 1) * B_QK_PAD] = rot.astype(qf_ref.dtype)
    kvraw = _dot(_rms_cast(ckv_ref[...], gkv_ref[...]), wkv_ref[...])
    kr = kr_ref[...]
    krot = (kr[:, 0:128] * tc + kr[:, 128:256] * ts).astype(kf_ref.dtype)
    for h in range(B_HEADS):
        kf_ref[:, h * B_QK_PAD:h * B_QK_PAD + B_NOPE] = (
            kvraw[:, h * B_NOPE:(h + 1) * B_NOPE].astype(kf_ref.dtype))
        kf_ref[:, h * B_QK_PAD + B_NOPE:(h + 1) * B_QK_PAD] = krot
    v_ref[...] = kvraw[:, B_HEADS * B_NOPE:].astype(v_ref.dtype)


def _mla_pre(pf, gq, gkv, wq, wkv, rope, tm=512):
    s = pf.shape[0]
    row = lambda width, off: pl.BlockSpec((tm, width), lambda i: (i, off // width))
    full = lambda a: pl.BlockSpec(a.shape, lambda i: (0, 0))
    gq = gq.reshape(1, B_Q_LORA)
    gkv = gkv.reshape(1, B_KV_LORA)
    out_row = lambda width: pl.BlockSpec((tm, width), lambda i: (i, 0))
    return pl.pallas_call(
        _mla_pre_kernel,
        grid=(s // tm,),
        in_specs=[row(B_Q_LORA, _PF_OFF["cq"]), row(B_KV_LORA, _PF_OFF["ckv"]),
                  row(256, _PF_OFF["kr_a"]), full(gq), full(gkv), full(wq), full(wkv),
                  pl.BlockSpec((tm, 256), lambda i: (i, 0))],
        out_specs=[out_row(B_HEADS * B_QK_PAD), out_row(B_HEADS * B_QK_PAD), out_row(B_WIDTH)],
        out_shape=[jax.ShapeDtypeStruct((s, B_HEADS * B_QK_PAD), MXU_DTYPE),
                   jax.ShapeDtypeStruct((s, B_HEADS * B_QK_PAD), MXU_DTYPE),
                   jax.ShapeDtypeStruct((s, B_WIDTH), MXU_DTYPE)],
        compiler_params=_cparams("parallel"),
        name="mla_pre",
    )(pf, pf, pf, gq, gkv, wq, wkv, rope)


def _run_pipeline(stages, n, n_static, unroll):
    depth = len(stages)
    t0 = n_static + 1
    assert unroll % 2 == 0 and depth >= 2 and t0 >= depth - 1

    def run(k, pos, slot, static_pos=None):
        stages[k](pos, slot, static_pos)

    for tt in range(t0):
        for k in reversed(range(depth)):
            pos = tt - k
            if pos < 0:
                continue
            if pos < n_static:
                run(k, pos, pos % 2, pos)
            else:
                pl.when(n > pos)(functools.partial(run, k, pos, pos % 2))

    def steady(tt, slot):
        for k in reversed(range(depth)):
            run(k, tt - k, (slot + k) % 2)

    def body(u, carry):
        for j in range(unroll):
            steady(t0 + unroll * u + j, (t0 + j) % 2)
        return carry

    trips = jnp.maximum(n - t0, 0) // unroll
    lax.fori_loop(0, trips, body, 0)
    start = t0 + unroll * trips
    left = jnp.maximum(n - start, 0)
    width = unroll // 2
    while width >= 1:

        def block(start=start, width=width):
            for j in range(width):
                steady(start + j, (t0 + j) % 2)

        pl.when((left & width) != 0)(block)
        start = start + (left & width)
        width //= 2

    def drain(first):
        for e in range(0 if first else 1, depth - 1):
            for k in reversed(range(e + 1, depth)):
                run(k, n + e - k, (n + e - k) % 2)

    pl.when(n >= t0)(functools.partial(drain, True))
    pl.when(n < t0)(functools.partial(drain, False))


MLA_UNROLL = 4


def _mla_kernel(q_ref, k_ref, v_ref, z_ref, o_ref,
                y_scr, p_scr, alpha_scr, m_scr, l_scr, acc_scr, *, tq, tk):
    n_diag = tq // tk
    i = pl.program_id(1)
    q = q_ref[...]
    c = (B_QK ** -0.5) * LOG2E

    def key_rows(pos):
        kb = jnp.where(pos < n_diag, n_diag * i + pos, pos - n_diag)
        return pl.ds(pl.multiple_of(kb * tk, tk), tk)

    def scores(pos, slot, static_pos):
        y_scr[slot] = _dot_nt(q, k_ref[key_rows(pos), :]) * c

    def softmax(pos, slot, static_pos):
        y = y_scr[slot]
        if static_pos is not None:
            row = lax.broadcasted_iota(jnp.int32, (tq, tk), 0)
            col = lax.broadcasted_iota(jnp.int32, (tq, tk), 1) + static_pos * tk
            y = jnp.where(col <= row, y, -jnp.inf)
        m = m_scr[...]
        m_new = jnp.maximum(m, jnp.max(y, axis=-1, keepdims=True))
        alpha = jnp.exp2(m - m_new)
        p = jnp.exp2(y - jnp.concatenate([m_new] * (tk // LANES), axis=-1))
        l_scr[...] = alpha * l_scr[...] + jnp.sum(p, axis=-1, keepdims=True)
        m_scr[...] = m_new
        alpha_scr[slot] = alpha
        p_scr[slot] = p.astype(MXU_DTYPE)

    def accumulate(pos, slot, static_pos):
        acc_scr[...] = alpha_scr[slot] * acc_scr[...] + _dot(p_scr[slot], v_ref[key_rows(pos), :])

    m_scr[...] = jnp.full((tq, LANES), -jnp.inf, jnp.float32)
    l_scr[...] = jnp.zeros((tq, LANES), jnp.float32)
    acc_scr[...] = jnp.zeros((tq, B_V), jnp.float32)
    _run_pipeline([scores, softmax, accumulate], n_diag * (i + 1), n_diag, MLA_UNROLL)
    o_ref[...] = ((acc_scr[...] / l_scr[...]) * _silu(z_ref[...])).astype(o_ref.dtype)


def _mla(qf, kf, v, pf, tq=1024, tk=512):
    s = qf.shape[0]
    zb = _PF_OFF["zb"] // B_V
    return pl.pallas_call(
        functools.partial(_mla_kernel, tq=tq, tk=tk),
        grid=(B_HEADS, s // tq),
        in_specs=[pl.BlockSpec((tq, B_QK_PAD), lambda h, i: (i, h)),
                  pl.BlockSpec((s, B_QK_PAD), lambda h, i: (0, h)),
                  pl.BlockSpec((s, B_V), lambda h, i: (0, h)),
                  pl.BlockSpec((tq, B_V), lambda h, i: (i, zb + h))],
        out_specs=pl.BlockSpec((tq, B_V), lambda h, i: (i, h)),
        out_shape=jax.ShapeDtypeStruct((s, B_WIDTH), MXU_DTYPE),
        scratch_shapes=[pltpu.VMEM((2, tq, tk), jnp.float32),
                        pltpu.VMEM((2, tq, tk), MXU_DTYPE),
                        pltpu.VMEM((2, tq, LANES), jnp.float32),
                        pltpu.VMEM((tq, LANES), jnp.float32),
                        pltpu.VMEM((tq, LANES), jnp.float32),
                        pltpu.VMEM((tq, B_V), jnp.float32)],
        compiler_params=_cparams("parallel", "arbitrary"),
        name="mla_attention",
    )(qf, kf, v, pf)


SB_UNROLL = 4


def _sb_kernel(q_ref, k_ref, v_ref, z_ref, uu_ref, o_ref,
               s_scr, hl_scr, lb_scr, off_scr, off_run_scr, w_scr, acc_scr, *, tq, tk):
    n_diag = tq // tk
    i = pl.program_id(1)
    q = q_ref[...]
    scale = HEAD_DIM ** -0.5
    n = n_diag * (i + 1)

    def key_rows(pos):
        return pl.ds(pl.multiple_of((n - 1 - pos) * tk, tk), tk)

    def score(pos, slot, static_pos):
        s_scr[slot] = _dot_nt(q, k_ref[key_rows(pos), :])

    def park(pos, slot, static_pos):
        y = s_scr[slot] * (scale * LOG2E)
        l1p = jnp.log2(1.0 + jnp.exp2(-jnp.abs(y)))
        drop = jnp.maximum(y, 0.0) + l1p
        log_beta = y - drop
        if static_pos is not None:
            row = lax.broadcasted_iota(jnp.int32, (tq, tk), 0)
            col = lax.broadcasted_iota(jnp.int32, (tq, tk), 1) + (n_diag - 1 - static_pos) * tk
            past = col < row
            drop = jnp.where(past, drop, 0.0)
            log_beta = jnp.where(past, log_beta, -jnp.inf)
        hi = drop.astype(MXU_DTYPE)
        lo = (drop - hi.astype(jnp.float32)).astype(MXU_DTYPE)
        hl_scr[slot] = jnp.concatenate([hi, lo], axis=-1)
        lb_scr[slot] = log_beta
        off = off_run_scr[...]
        off_scr[slot] = off
        off_run_scr[...] = off + jnp.sum(drop, axis=-1, keepdims=True)

    def weigh(pos, slot, static_pos):
        within = _dot(hl_scr[slot], uu_ref[...])
        off = jnp.concatenate([off_scr[slot]] * (tk // LANES), axis=-1)
        w_scr[slot] = jnp.exp2(lb_scr[slot] - (within + off)).astype(MXU_DTYPE)

    def accumulate(pos, slot, static_pos):
        acc_scr[...] += _dot(w_scr[slot], v_ref[key_rows(pos), :])

    off_run_scr[...] = jnp.zeros((tq, LANES), jnp.float32)
    acc_scr[...] = jnp.zeros((tq, HEAD_DIM), jnp.float32)
    _run_pipeline([score, park, weigh, accumulate], n, n_diag, SB_UNROLL)
    o_ref[...] = (acc_scr[...] * _silu(z_ref[...])).astype(o_ref.dtype)


def _sb(p16, pf, tq=1024, tk=256):
    s = p16.shape[0]
    qb = _P16_OFF["qc"] // HEAD_DIM
    kb = _P16_OFF["kc"] // HEAD_DIM
    vb = _P16_OFF["vc"] // HEAD_DIM
    zb = _PF_OFF["zc"] // HEAD_DIM
    j = lax.broadcasted_iota(jnp.int32, (tk, tk), 0)
    c = lax.broadcasted_iota(jnp.int32, (tk, tk), 1)
    u = (j > c).astype(MXU_DTYPE)
    uu = jnp.concatenate([u, u], axis=0)
    return pl.pallas_call(
        functools.partial(_sb_kernel, tq=tq, tk=tk),
        grid=(C_HEADS, s // tq),
        in_specs=[pl.BlockSpec((tq, HEAD_DIM), lambda h, i: (i, qb + h)),
                  pl.BlockSpec((s, HEAD_DIM), lambda h, i: (0, kb + h)),
                  pl.BlockSpec((s, HEAD_DIM), lambda h, i: (0, vb + h)),
                  pl.BlockSpec((tq, HEAD_DIM), lambda h, i: (i, zb + h)),
                  pl.BlockSpec((2 * tk, tk), lambda h, i: (0, 0))],
        out_specs=pl.BlockSpec((tq, HEAD_DIM), lambda h, i: (i, h)),
        out_shape=jax.ShapeDtypeStruct((s, C_WIDTH), MXU_DTYPE),
        scratch_shapes=[pltpu.VMEM((2, tq, tk), jnp.float32),
                        pltpu.VMEM((2, tq, 2 * tk), MXU_DTYPE),
                        pltpu.VMEM((2, tq, tk), jnp.float32),
                        pltpu.VMEM((2, tq, LANES), jnp.float32),
                        pltpu.VMEM((tq, LANES), jnp.float32),
                        pltpu.VMEM((2, tq, tk), MXU_DTYPE),
                        pltpu.VMEM((tq, HEAD_DIM), jnp.float32)],
        compiler_params=_cparams("parallel", "arbitrary"),
        name="stick_breaking_attention",
    )(p16, p16, p16, pf, uu)


def _swap_halves(w):
    half = w.shape[-1] // 2
    return jnp.concatenate([w[..., half:], w[..., :half]], axis=-1)


def _pad_cols(w, width):
    return jnp.pad(w, ((0, 0), (0, width - w.shape[-1])))


_GATE_COLS = (("ga", D_MODEL), ("gb", D_MODEL), ("gc", D_MODEL))
W_ALL_WIDTH = GATE_OFF + 3 * D_MODEL


W_IN_ROWS = 256


def _w_in_layout_table():
    src, is_kr = [], []
    for name, width in _P16_COLS + _PF_COLS + _GATE_COLS:
        if name == "kr_a":
            src.append(_IN_OFF["kr"][0])
            is_kr.append(1)
        elif name != "kr_b":
            assert width % W_IN_ROWS == 0
            src += [_IN_OFF[name][0] + t * W_IN_ROWS for t in range(width // W_IN_ROWS)]
            is_kr += [0] * (width // W_IN_ROWS)
    return jnp.asarray(src, jnp.int32), jnp.asarray(is_kr, jnp.int32)


def _w_in_layout_kernel(src_ref, is_kr_ref, w_ref, o_ref):
    x = w_ref[0]
    half = B_ROPE // 2
    kr = x[:B_ROPE]
    pad = jnp.zeros((LANES - B_ROPE, x.shape[1]), x.dtype)
    kr_rows = jnp.concatenate([kr, pad, kr[half:], kr[:half], pad], axis=0)
    x = jnp.where(is_kr_ref[pl.program_id(1)] == 1, kr_rows, x)
    o_ref[...] = x.astype(o_ref.dtype)


def _layout_w_in(w_in_t):
    depth, _, d = w_in_t.shape
    src, is_kr = _w_in_layout_table()
    return pl.pallas_call(
        _w_in_layout_kernel,
        grid_spec=pltpu.PrefetchScalarGridSpec(
            num_scalar_prefetch=2,
            grid=(depth, W_ALL_WIDTH // W_IN_ROWS),
            in_specs=[pl.BlockSpec((pl.Element(1), pl.Element(W_IN_ROWS), pl.Element(d)),
                                   lambda l, r, src, is_kr: (l, pl.multiple_of(src[r], B_ROPE), 0))],
            out_specs=pl.BlockSpec((None, W_IN_ROWS, d), lambda l, r, src, is_kr: (l, r, 0))),
        out_shape=jax.ShapeDtypeStruct((depth, W_ALL_WIDTH, d), MXU_DTYPE),
        compiler_params=_cparams("parallel", "arbitrary"),
        name="w_in_layout",
    )(src, is_kr, w_in_t)


def _cast_kernel(w_ref, o_ref):
    o_ref[...] = w_ref[...].astype(o_ref.dtype)


def _cast_weights(w, tr=512):
    depth, rows, cols = w.shape
    tr = min(tr, rows)
    spec = pl.BlockSpec((1, tr, cols), lambda l, r: (l, r, 0))
    return pl.pallas_call(
        _cast_kernel,
        grid=(depth, rows // tr),
        in_specs=[spec],
        out_specs=spec,
        out_shape=jax.ShapeDtypeStruct(w.shape, MXU_DTYPE),
        compiler_params=_cparams("parallel", "parallel"),
        name="cast_weights",
    )(w)


def _layout_w_q_up(w):
    parts = []
    for h in range(B_HEADS):
        nope = w[:, h * B_QK:h * B_QK + B_NOPE]
        rope = w[:, h * B_QK + B_NOPE:(h + 1) * B_QK]
        parts += [nope, _pad_cols(rope, 128), _pad_cols(_swap_halves(rope), 128)]
    return jnp.concatenate(parts, axis=1).astype(MXU_DTYPE)


def _layout_w_kv_up(w):
    per = B_NOPE + B_V
    ks = [w[:, h * per:h * per + B_NOPE] for h in range(B_HEADS)]
    vs = [w[:, h * per + B_NOPE:(h + 1) * per] for h in range(B_HEADS)]
    return jnp.concatenate(ks + vs, axis=1).astype(MXU_DTYPE)


def _rope_table(seq):
    pos = jnp.arange(seq, dtype=jnp.float32)
    inv = ROPE_THETA ** (-jnp.arange(0, B_ROPE, 2, dtype=jnp.float32) / B_ROPE)
    ang = pos[:, None] * inv[None, :]
    cos, sin = jnp.cos(ang), jnp.sin(ang)
    zeros = jnp.zeros((seq, 128 - B_ROPE), jnp.float32)
    return jnp.concatenate([cos, cos, zeros, -sin, sin, zeros], axis=1)


def kernel(x, norm_g, w_in, attn_sinks, rel_bias, g_q_lora, w_q_up, g_kv_lora, w_kv_up,
           w_proj_a, w_proj_b, w_proj_c, w_out, final_g):
    batch, seq, d = x.shape
    assert batch == 1 and d == D_MODEL and seq % 1024 == 0
    depth = w_in.shape[0]
    xs = x.reshape(seq, d)
    bias = _swa_bias(rel_bias)
    rope = _rope_table(seq)
    w_all = _layout_w_in(jnp.swapaxes(w_in, 1, 2))
    wa, wb, wc, wo = (_cast_weights(w) for w in (w_proj_a, w_proj_b, w_proj_c, w_out))

    for l in range(depth):
        h = _rmsnorm(xs, norm_g[l], MXU_DTYPE)
        p16 = _matmul(h, w_all, l, 0, P16_WIDTH, MXU_DTYPE, tm=1024, tn=768, transposed_w=True,
                      name="in_proj_bf16")
        pf = _matmul(h, w_all, l, P16_WIDTH, PF_WIDTH, jnp.float32, tm=1024, tn=768,
                     transposed_w=True, name="in_proj_f32")

        ya = _swa(p16, pf, attn_sinks[l], bias)
        qf, kf, vb = _mla_pre(pf, g_q_lora[l], g_kv_lora[l], _layout_w_q_up(w_q_up[l]),
                              _layout_w_kv_up(w_kv_up[l]), rope)
        yb = _mla(qf, kf, vb, pf)
        yc = _sb(p16, pf)

        merged = _merge(h, w_all, l, ya, yb, yc, wa, wb, wc)
        xs = _matmul(merged, wo, l, 0, D_MODEL, jnp.float32, tm=1024, tn=512, residual=xs,
                     name="out_proj")

    out = _rmsnorm(xs, final_g, jnp.float32)
    return out.reshape(batch, seq, d)
```

```python
import functools
import math

import jax
import jax.numpy as jnp
from jax import lax
from jax.experimental import pallas as pl
from jax.experimental.pallas import tpu as pltpu

D_MODEL = 4096
HEAD_DIM = 128
LANES = 128
BLOCK = 128
EPS = 1e-6
A_HEADS = 8
A_KV_HEADS = 2
A_GROUP = A_HEADS // A_KV_HEADS
A_WIDTH = A_HEADS * HEAD_DIM
A_KV_WIDTH = A_KV_HEADS * HEAD_DIM
NUM_BUCKETS = 32
MAX_DISTANCE = 128
B_HEADS = 4
B_Q_LORA = 1024
B_KV_LORA = 512
B_NOPE = 128
B_ROPE = 64
B_V = 128
B_QK = B_NOPE + B_ROPE
B_QK_PAD = 256
B_WIDTH = B_HEADS * B_V
ROPE_THETA = 10000.0
C_HEADS = 4
C_WIDTH = C_HEADS * HEAD_DIM

MXU_DTYPE = jnp.bfloat16
LOG2E = math.log2(math.e)
VMEM_LIMIT_BYTES = 56 * 1024 * 1024

_IN_SIZES = (A_WIDTH, A_KV_WIDTH, A_KV_WIDTH, A_WIDTH, B_Q_LORA, B_KV_LORA, B_ROPE, B_WIDTH,
             C_WIDTH, C_WIDTH, C_WIDTH, C_WIDTH, D_MODEL, D_MODEL, D_MODEL)
_IN_NAMES = ("qa", "ka", "va", "za", "cq", "ckv", "kr", "zb", "qc", "kc", "vc", "zc", "ga", "gb", "gc")
_IN_OFF = {}
_o = 0
for _n, _s in zip(_IN_NAMES, _IN_SIZES):
    _IN_OFF[_n] = (_o, _s)
    _o += _s

_P16_COLS = (("qa", A_WIDTH), ("ka", A_KV_WIDTH), ("va", A_KV_WIDTH),
             ("qc", C_WIDTH), ("kc", C_WIDTH), ("vc", C_WIDTH))
_PF_COLS = (("za", A_WIDTH), ("cq", B_Q_LORA), ("ckv", B_KV_LORA), ("kr_a", 128), ("kr_b", 128),
            ("zb", B_WIDTH), ("zc", C_WIDTH))


def _offsets(cols):
    out, o = {}, 0
    for n, s in cols:
        out[n] = o
        o += s
    return out, o


_P16_OFF, P16_WIDTH = _offsets(_P16_COLS)
_PF_OFF, PF_WIDTH = _offsets(_PF_COLS)
GATE_OFF = P16_WIDTH + PF_WIDTH


def _cparams(*sem):
    return pltpu.CompilerParams(dimension_semantics=sem, vmem_limit_bytes=VMEM_LIMIT_BYTES)


def _dot(a, b):
    return jnp.dot(a, b, preferred_element_type=jnp.float32)


def _dot_nt(a, b):
    return lax.dot_general(a, b, (((1,), (1,)), ((), ())), preferred_element_type=jnp.float32)


def _silu(z):
    return z * jax.nn.sigmoid(z)


def _neg_abs(x):
    bits = lax.bitcast_convert_type(x, jnp.uint32) | jnp.uint32(0x80000000)
    return lax.bitcast_convert_type(bits, jnp.float32)


def _rmsnorm_kernel(x_ref, g_ref, o_ref):
    x = x_ref[...]
    y = x * lax.rsqrt(jnp.mean(x * x, axis=-1, keepdims=True) + EPS)
    o_ref[...] = (y * g_ref[...]).astype(o_ref.dtype)


def _rmsnorm(x, g, out_dtype, tm=512):
    s, d = x.shape
    return pl.pallas_call(
        _rmsnorm_kernel,
        grid=(s // tm,),
        in_specs=[pl.BlockSpec((tm, d), lambda i: (i, 0)),
                  pl.BlockSpec((1, d), lambda i: (0, 0))],
        out_specs=pl.BlockSpec((tm, d), lambda i: (i, 0)),
        out_shape=jax.ShapeDtypeStruct((s, d), out_dtype),
        compiler_params=_cparams("parallel"),
        name="rmsnorm",
    )(x, g.reshape(1, d))


def _matmul_kernel(a_ref, w_ref, o_ref):
    o_ref[...] = _dot(a_ref[...], w_ref[...]).astype(o_ref.dtype)


def _matmul_nt_kernel(a_ref, wt_ref, o_ref):
    o_ref[...] = _dot_nt(a_ref[...], wt_ref[...]).astype(o_ref.dtype)


def _matmul_residual_kernel(a_ref, w_ref, r_ref, o_ref):
    o_ref[...] = r_ref[...] + _dot(a_ref[...], w_ref[...])


def _matmul(a, w, layer, col_off, n_cols, out_dtype, tm, tn, residual=None, transposed_w=False,
            name="matmul"):
    s, k = a.shape
    assert s % tm == 0 and n_cols % tn == 0 and col_off % tn == 0
    assert not (transposed_w and residual is not None)
    jo = col_off // tn
    if transposed_w:
        w_spec = pl.BlockSpec((None, tn, k), lambda i, j: (layer, j + jo, 0))
    else:
        w_spec = pl.BlockSpec((None, k, tn), lambda i, j: (layer, 0, j + jo))
    in_specs = [pl.BlockSpec((tm, k), lambda i, j: (i, 0)), w_spec]
    args = [a, w]
    body = _matmul_nt_kernel if transposed_w else _matmul_kernel
    if residual is not None:
        in_specs.append(pl.BlockSpec((tm, tn), lambda i, j: (i, j)))
        args.append(residual)
        body = _matmul_residual_kernel
    return pl.pallas_call(
        body,
        grid=(s // tm, n_cols // tn),
        in_specs=in_specs,
        out_specs=pl.BlockSpec((tm, tn), lambda i, j: (i, j)),
        out_shape=jax.ShapeDtypeStruct((s, n_cols), out_dtype),
        compiler_params=_cparams("parallel", "arbitrary"),
        name=name,
    )(*args)


def _merge_kernel(h_ref, wga_ref, wgb_ref, wgc_ref, ya_ref, yb_ref, yc_ref,
                  wa_ref, wb_ref, wc_ref, o_ref):
    h = h_ref[...]

    def branch(wg_ref, y_ref, w_ref):
        return jax.nn.sigmoid(_dot_nt(h, wg_ref[...])) * _dot(y_ref[...], w_ref[...])

    merged = (branch(wga_ref, ya_ref, wa_ref) + branch(wgb_ref, yb_ref, wb_ref)
              + branch(wgc_ref, yc_ref, wc_ref))
    o_ref[...] = merged.astype(o_ref.dtype)


def _merge(h, w_all, layer, ya, yb, yc, wa, wb, wc, tm=1024, tn=256):
    s = h.shape[0]
    nj = D_MODEL // tn
    g0 = GATE_OFF // tn

    def gate_spec(k):
        return pl.BlockSpec((None, tn, D_MODEL), lambda i, j: (layer, g0 + k * nj + j, 0))

    def row_spec(width):
        return pl.BlockSpec((tm, width), lambda i, j: (i, 0))

    def w_spec(width):
        return pl.BlockSpec((None, width, tn), lambda i, j: (layer, 0, j))

    return pl.pallas_call(
        _merge_kernel,
        grid=(s // tm, nj),
        in_specs=[row_spec(D_MODEL), gate_spec(0), gate_spec(1), gate_spec(2),
                  row_spec(A_WIDTH), row_spec(B_WIDTH), row_spec(C_WIDTH),
                  w_spec(A_WIDTH), w_spec(B_WIDTH), w_spec(C_WIDTH)],
        out_specs=pl.BlockSpec((tm, tn), lambda i, j: (i, j)),
        out_shape=jax.ShapeDtypeStruct((s, D_MODEL), MXU_DTYPE),
        compiler_params=_cparams("parallel", "arbitrary"),
        name="gated_merge",
    )(h, w_all, w_all, w_all, ya, yb, yc, wa, wb, wc)


def _t5_bucket(rel):
    n = jnp.maximum(rel, 0)
    max_exact = NUM_BUCKETS // 2
    logn = jnp.log(jnp.maximum(n, 1).astype(jnp.float32) / max_exact)
    large = max_exact + (logn / math.log(MAX_DISTANCE / max_exact)
                         * (NUM_BUCKETS - max_exact)).astype(jnp.int32)
    large = jnp.minimum(large, NUM_BUCKETS - 1)
    return jnp.where(n < max_exact, n, large)


def _swa_bias_kernel(rel_bias_ref, bucket_ref, o_ref):
    bucket = bucket_ref[...]
    for h in range(A_HEADS):
        acc = jnp.zeros(bucket.shape, jnp.float32)
        for b in range(NUM_BUCKETS):
            acc = jnp.where(bucket == b, rel_bias_ref[b, h], acc)
        o_ref[h] = acc


def _swa_bias(rel_bias):
    t = jnp.arange(BLOCK)[:, None]
    s = jnp.arange(2 * BLOCK)[None, :]
    bucket = _t5_bucket(BLOCK + t - s).astype(jnp.int32)
    return pl.pallas_call(
        _swa_bias_kernel,
        in_specs=[pl.BlockSpec(memory_space=pltpu.SMEM),
                  pl.BlockSpec((BLOCK, 2 * BLOCK), lambda: (0, 0))],
        out_specs=pl.BlockSpec((A_HEADS, BLOCK, 2 * BLOCK), lambda: (0, 0, 0)),
        out_shape=jax.ShapeDtypeStruct((A_HEADS, BLOCK, 2 * BLOCK), jnp.float32),
        name="swa_bias_table",
    )(rel_bias, bucket)


def _swa_kernel(sinks_ref, q_ref, kc_ref, kp_ref, vc_ref, vp_ref, bias_ref, z_ref, o_ref):
    n = pl.program_id(0)
    rows = A_GROUP * BLOCK
    t = lax.broadcasted_iota(jnp.int32, (rows, 2 * BLOCK), 0) & (BLOCK - 1)
    s = lax.broadcasted_iota(jnp.int32, (rows, 2 * BLOCK), 1)
    ok = (s > jnp.maximum(t, jnp.where(n > 0, -1, BLOCK - 1))) & (s <= t + BLOCK)
    scale = HEAD_DIM ** -0.5

    def head_cols(h):
        return slice(h * HEAD_DIM, (h + 1) * HEAD_DIM)

    scores = []
    for hk in range(A_KV_HEADS):
        q = jnp.concatenate([q_ref[:, head_cols(hk * A_GROUP + g)] for g in range(A_GROUP)], axis=0)
        kk = jnp.concatenate([kp_ref[:, head_cols(hk)], kc_ref[:, head_cols(hk)]], axis=0)
        scores.append(_dot_nt(q, kk))
    probs = []
    for hk in range(A_KV_HEADS):
        sc = jnp.where(ok, scores[hk] * scale + bias_ref[hk], -jnp.inf)
        sink = jnp.concatenate([jnp.full((BLOCK, LANES), sinks_ref[hk * A_GROUP + g], jnp.float32)
                                for g in range(A_GROUP)], axis=0)
        m = jnp.maximum(jnp.max(sc, axis=-1, keepdims=True), sink)
        e = jnp.exp(sc - jnp.concatenate([m, m], axis=-1))
        inv = 1.0 / (jnp.sum(e, axis=-1, keepdims=True) + jnp.exp(sink - m))
        probs.append((e * jnp.concatenate([inv, inv], axis=-1)).astype(MXU_DTYPE))
    for hk in range(A_KV_HEADS):
        vv = jnp.concatenate([vp_ref[:, head_cols(hk)], vc_ref[:, head_cols(hk)]], axis=0)
        out = _dot(probs[hk], vv)
        for g in range(A_GROUP):
            cols = head_cols(hk * A_GROUP + g)
            o_ref[:, cols] = (out[g * BLOCK:(g + 1) * BLOCK] * _silu(z_ref[:, cols])).astype(o_ref.dtype)


def _swa(p16, pf, sinks, bias):
    s = p16.shape[0]
    qb = _P16_OFF["qa"] // A_WIDTH
    kb = _P16_OFF["ka"] // A_KV_WIDTH
    vb = _P16_OFF["va"] // A_KV_WIDTH
    zb = _PF_OFF["za"] // A_WIDTH
    prev = lambda n: jnp.maximum(n - 1, 0)
    return pl.pallas_call(
        _swa_kernel,
        grid=(s // BLOCK,),
        in_specs=[pl.BlockSpec(memory_space=pltpu.SMEM),
                  pl.BlockSpec((BLOCK, A_WIDTH), lambda n: (n, qb)),
                  pl.BlockSpec((BLOCK, A_KV_WIDTH), lambda n: (n, kb)),
                  pl.BlockSpec((BLOCK, A_KV_WIDTH), lambda n: (prev(n), kb)),
                  pl.BlockSpec((BLOCK, A_KV_WIDTH), lambda n: (n, vb)),
                  pl.BlockSpec((BLOCK, A_KV_WIDTH), lambda n: (prev(n), vb)),
                  pl.BlockSpec((A_KV_HEADS, A_GROUP * BLOCK, 2 * BLOCK), lambda n: (0, 0, 0)),
                  pl.BlockSpec((BLOCK, A_WIDTH), lambda n: (n, zb))],
        out_specs=pl.BlockSpec((BLOCK, A_WIDTH), lambda n: (n, 0)),
        out_shape=jax.ShapeDtypeStruct((s, A_WIDTH), MXU_DTYPE),
        compiler_params=_cparams("parallel"),
        name="swa_attention",
    )(sinks, p16, p16, p16, p16, p16, bias.reshape(A_KV_HEADS, A_GROUP * BLOCK, 2 * BLOCK), pf)


def _rms_cast(x, g):
    y = x * lax.rsqrt(jnp.mean(x * x, axis=-1, keepdims=True) + EPS)
    return (y * g).astype(MXU_DTYPE)


def _mla_pre_kernel(cq_ref, ckv_ref, kr_ref, gq_ref, gkv_ref, wq_ref, wkv_ref, rope_ref,
                    qf_ref, kf_ref, v_ref):
    tc = rope_ref[:, 0:128]
    ts = rope_ref[:, 128:256]
    qraw = _dot(_rms_cast(cq_ref[...], gq_ref[...]), wq_ref[...])
    for h in range(B_HEADS):
        b = 3 * 128 * h
        qf_ref[:, h * B_QK_PAD:h * B_QK_PAD + B_NOPE] = qraw[:, b:b + 128].astype(qf_ref.dtype)
        rot = qraw[:, b + 128:b + 256] * tc + qraw[:, b + 256:b + 384] * ts
        qf_ref[:, h * B_QK_PAD + B_NOPE:(h + 1) * B_QK_PAD] = rot.astype(qf_ref.dtype)
    kvraw = _dot(_rms_cast(ckv_ref[...], gkv_ref[...]), wkv_ref[...])
    kr = kr_ref[...]
    krot = (kr[:, 0:128] * tc + kr[:, 128:256] * ts).astype(kf_ref.dtype)
    for h in range(B_HEADS):
        kf_ref[:, h * B_QK_PAD:h * B_QK_PAD + B_NOPE] = (
            kvraw[:, h * B_NOPE:(h + 1) * B_NOPE].astype(kf_ref.dtype))
        kf_ref[:, h * B_QK_PAD + B_NOPE:(h + 1) * B_QK_PAD] = krot
    v_ref[...] = kvraw[:, B_HEADS * B_NOPE:].astype(v_ref.dtype)


def _mla_pre(pf, gq, gkv, wq, wkv, rope, tm=512):
    s = pf.shape[0]
    row = lambda width, off: pl.BlockSpec((tm, width), lambda i: (i, off // width))
    full = lambda a: pl.BlockSpec(a.shape, lambda i: (0, 0))
    gq = gq.reshape(1, B_Q_LORA)
    gkv = gkv.reshape(1, B_KV_LORA)
    out_row = lambda width: pl.BlockSpec((tm, width), lambda i: (i, 0))
    return pl.pallas_call(
        _mla_pre_kernel,
        grid=(s // tm,),
        in_specs=[row(B_Q_LORA, _PF_OFF["cq"]), row(B_KV_LORA, _PF_OFF["ckv"]),
                  row(256, _PF_OFF["kr_a"]), full(gq), full(gkv), full(wq), full(wkv),
                  pl.BlockSpec((tm, 256), lambda i: (i, 0))],
        out_specs=[out_row(B_HEADS * B_QK_PAD), out_row(B_HEADS * B_QK_PAD), out_row(B_WIDTH)],
        out_shape=[jax.ShapeDtypeStruct((s, B_HEADS * B_QK_PAD), MXU_DTYPE),
                   jax.ShapeDtypeStruct((s, B_HEADS * B_QK_PAD), MXU_DTYPE),
                   jax.ShapeDtypeStruct((s, B_WIDTH), MXU_DTYPE)],
        compiler_params=_cparams("parallel"),
        name="mla_pre",
    )(pf, pf, pf, gq, gkv, wq, wkv, rope)


def _run_pipeline(stages, n, n_static, unroll):
    depth = len(stages)
    t0 = n_static + 1
    assert unroll % 2 == 0 and depth >= 2 and t0 >= depth - 1

    def run(k, pos, slot, static_pos=None):
        stages[k](pos, slot, static_pos)

    for tt in range(t0):
        for k in reversed(range(depth)):
            pos = tt - k
            if pos < 0:
                continue
            if pos < n_static:
                run(k, pos, pos % 2, pos)
            else:
                pl.when(n > pos)(functools.partial(run, k, pos, pos % 2))

    def steady(tt, slot):
        for k in reversed(range(depth)):
            run(k, tt - k, (slot + k) % 2)

    def body(u, carry):
        for j in range(unroll):
            steady(t0 + unroll * u + j, (t0 + j) % 2)
        return carry

    trips = jnp.maximum(n - t0, 0) // unroll
    lax.fori_loop(0, trips, body, 0)
    start = t0 + unroll * trips
    left = jnp.maximum(n - start, 0)
    width = unroll // 2
    while width >= 1:

        def block(start=start, width=width):
            for j in range(width):
                steady(start + j, (t0 + j) % 2)

        pl.when((left & width) != 0)(block)
        start = start + (left & width)
        width //= 2

    def drain(first):
        for e in range(0 if first else 1, depth - 1):
            for k in reversed(range(e + 1, depth)):
                run(k, n + e - k, (n + e - k) % 2)

    pl.when(n >= t0)(functools.partial(drain, True))
    pl.when(n < t0)(functools.partial(drain, False))


MLA_UNROLL = 4


def _mla_kernel(q_ref, k_ref, v_ref, z_ref, o_ref,
                y_scr, p_scr, alpha_scr, m_scr, l_scr, acc_scr, *, tq, tk):
    n_diag = tq // tk
    i = pl.program_id(1)
    q = q_ref[...]
    c = (B_QK ** -0.5) * LOG2E

    def key_rows(pos):
        kb = jnp.where(pos < n_diag, n_diag * i + pos, pos - n_diag)
        return pl.ds(pl.multiple_of(kb * tk, tk), tk)

    def scores(pos, slot, static_pos):
        y_scr[slot] = _dot_nt(q, k_ref[key_rows(pos), :]) * c

    def softmax(pos, slot, static_pos):
        y = y_scr[slot]
        if static_pos is not None:
            row = lax.broadcasted_iota(jnp.int32, (tq, tk), 0)
            col = lax.broadcasted_iota(jnp.int32, (tq, tk), 1) + static_pos * tk
            y = jnp.where(col <= row, y, -jnp.inf)
        m = m_scr[...]
        m_new = jnp.maximum(m, jnp.max(y, axis=-1, keepdims=True))
        alpha = jnp.exp2(m - m_new)
        p = jnp.exp2(y - jnp.concatenate([m_new] * (tk // LANES), axis=-1))
        l_scr[...] = alpha * l_scr[...] + jnp.sum(p, axis=-1, keepdims=True)
        m_scr[...] = m_new
        alpha_scr[slot] = alpha
        p_scr[slot] = p.astype(MXU_DTYPE)

    def accumulate(pos, slot, static_pos):
        acc_scr[...] = alpha_scr[slot] * acc_scr[...] + _dot(p_scr[slot], v_ref[key_rows(pos), :])

    m_scr[...] = jnp.full((tq, LANES), -jnp.inf, jnp.float32)
    l_scr[...] = jnp.zeros((tq, LANES), jnp.float32)
    acc_scr[...] = jnp.zeros((tq, B_V), jnp.float32)
    _run_pipeline([scores, softmax, accumulate], n_diag * (i + 1), n_diag, MLA_UNROLL)
    o_ref[...] = ((acc_scr[...] / l_scr[...]) * _silu(z_ref[...])).astype(o_ref.dtype)


def _mla(qf, kf, v, pf, tq=1024, tk=512):
    s = qf.shape[0]
    zb = _PF_OFF["zb"] // B_V
    return pl.pallas_call(
        functools.partial(_mla_kernel, tq=tq, tk=tk),
        grid=(B_HEADS, s // tq),
        in_specs=[pl.BlockSpec((tq, B_QK_PAD), lambda h, i: (i, h)),
                  pl.BlockSpec((s, B_QK_PAD), lambda h, i: (0, h)),
                  pl.BlockSpec((s, B_V), lambda h, i: (0, h)),
                  pl.BlockSpec((tq, B_V), lambda h, i: (i, zb + h))],
        out_specs=pl.BlockSpec((tq, B_V), lambda h, i: (i, h)),
        out_shape=jax.ShapeDtypeStruct((s, B_WIDTH), MXU_DTYPE),
        scratch_shapes=[pltpu.VMEM((2, tq, tk), jnp.float32),
                        pltpu.VMEM((2, tq, tk), MXU_DTYPE),
                        pltpu.VMEM((2, tq, LANES), jnp.float32),
                        pltpu.VMEM((tq, LANES), jnp.float32),
                        pltpu.VMEM((tq, LANES), jnp.float32),
                        pltpu.VMEM((tq, B_V), jnp.float32)],
        compiler_params=_cparams("parallel", "arbitrary"),
        name="mla_attention",
    )(qf, kf, v, pf)


SB_UNROLL = 4


def _sb_kernel(q_ref, k_ref, v_ref, z_ref, u_ref, o_ref,
               s_scr, drop_scr, lb_scr, off_scr, off_run_scr, w_scr, acc_scr, *, tq, tk):
    n_diag = tq // tk
    i = pl.program_id(1)
    q = q_ref[...]
    scale = HEAD_DIM ** -0.5
    n = n_diag * (i + 1)

    def key_rows(pos):
        return pl.ds(pl.multiple_of((n - 1 - pos) * tk, tk), tk)

    def score(pos, slot, static_pos):
        s_scr[slot] = _dot_nt(q, k_ref[key_rows(pos), :])

    def park(pos, slot, static_pos):
        y = s_scr[slot] * (scale * LOG2E)
        l1p = jnp.log2(1.0 + jnp.exp2(_neg_abs(y)))
        drop = jnp.maximum(y, 0.0) + l1p
        log_beta = y - drop
        if static_pos is not None:
            row = lax.broadcasted_iota(jnp.int32, (tq, tk), 0)
            col = lax.broadcasted_iota(jnp.int32, (tq, tk), 1) + (n_diag - 1 - static_pos) * tk
            past = col < row
            drop = jnp.where(past, drop, 0.0)
            log_beta = jnp.where(past, log_beta, -jnp.inf)
        drop_scr[slot] = drop.astype(MXU_DTYPE)
        lb_scr[slot] = log_beta
        off = off_run_scr[...]
        off_scr[slot] = off
        off_run_scr[...] = off + jnp.sum(drop, axis=-1, keepdims=True)

    def weigh(pos, slot, static_pos):
        within = _dot(drop_scr[slot], u_ref[...])
        off = jnp.concatenate([off_scr[slot]] * (tk // LANES), axis=-1)
        w_scr[slot] = jnp.exp2(lb_scr[slot] - (within + off)).astype(MXU_DTYPE)

    def accumulate(pos, slot, static_pos):
        acc_scr[...] += _dot(w_scr[slot], v_ref[key_rows(pos), :])

    off_run_scr[...] = jnp.zeros((tq, LANES), jnp.float32)
    acc_scr[...] = jnp.zeros((tq, HEAD_DIM), jnp.float32)
    _run_pipeline([score, park, weigh, accumulate], n, n_diag, SB_UNROLL)
    o_ref[...] = (acc_scr[...] * _silu(z_ref[...])).astype(o_ref.dtype)


def _sb(p16, pf, tq=1024, tk=256):
    s = p16.shape[0]
    qb = _P16_OFF["qc"] // HEAD_DIM
    kb = _P16_OFF["kc"] // HEAD_DIM
    vb = _P16_OFF["vc"] // HEAD_DIM
    zb = _PF_OFF["zc"] // HEAD_DIM
    j = lax.broadcasted_iota(jnp.int32, (tk, tk), 0)
    c = lax.broadcasted_iota(jnp.int32, (tk, tk), 1)
    u = (j > c).astype(MXU_DTYPE)
    return pl.pallas_call(
        functools.partial(_sb_kernel, tq=tq, tk=tk),
        grid=(C_HEADS, s // tq),
        in_specs=[pl.BlockSpec((tq, HEAD_DIM), lambda h, i: (i, qb + h)),
                  pl.BlockSpec((s, HEAD_DIM), lambda h, i: (0, kb + h)),
                  pl.BlockSpec((s, HEAD_DIM), lambda h, i: (0, vb + h)),
                  pl.BlockSpec((tq, HEAD_DIM), lambda h, i: (i, zb + h)),
                  pl.BlockSpec((tk, tk), lambda h, i: (0, 0))],
        out_specs=pl.BlockSpec((tq, HEAD_DIM), lambda h, i: (i, h)),
        out_shape=jax.ShapeDtypeStruct((s, C_WIDTH), MXU_DTYPE),
        scratch_shapes=[pltpu.VMEM((2, tq, tk), jnp.float32),
                        pltpu.VMEM((2, tq, tk), MXU_DTYPE),
                        pltpu.VMEM((2, tq, tk), jnp.float32),
                        pltpu.VMEM((2, tq, LANES), jnp.float32),
                        pltpu.VMEM((tq, LANES), jnp.float32),
                        pltpu.VMEM((2, tq, tk), MXU_DTYPE),
                        pltpu.VMEM((tq, HEAD_DIM), jnp.float32)],
        compiler_params=_cparams("parallel", "arbitrary"),
        name="stick_breaking_attention",
    )(p16, p16, p16, pf, u)


def _swap_halves(w):
    half = w.shape[-1] // 2
    return jnp.concatenate([w[..., half:], w[..., :half]], axis=-1)


def _pad_cols(w, width):
    return jnp.pad(w, ((0, 0), (0, width - w.shape[-1])))


_GATE_COLS = (("ga", D_MODEL), ("gb", D_MODEL), ("gc", D_MODEL))
W_ALL_WIDTH = GATE_OFF + 3 * D_MODEL


W_IN_ROWS = 256


def _w_in_layout_table():
    src, is_kr = [], []
    for name, width in _P16_COLS + _PF_COLS + _GATE_COLS:
        if name == "kr_a":
            src.append(_IN_OFF["kr"][0])
            is_kr.append(1)
        elif name != "kr_b":
            assert width % W_IN_ROWS == 0
            src += [_IN_OFF[name][0] + t * W_IN_ROWS for t in range(width // W_IN_ROWS)]
            is_kr += [0] * (width // W_IN_ROWS)
    return jnp.asarray(src, jnp.int32), jnp.asarray(is_kr, jnp.int32)


def _w_in_layout_kernel(src_ref, is_kr_ref, w_ref, o_ref):
    x = w_ref[0]
    half = B_ROPE // 2
    kr = x[:B_ROPE]
    pad = jnp.zeros((LANES - B_ROPE, x.shape[1]), x.dtype)
    kr_rows = jnp.concatenate([kr, pad, kr[half:], kr[:half], pad], axis=0)
    x = jnp.where(is_kr_ref[pl.program_id(1)] == 1, kr_rows, x)
    o_ref[...] = x.astype(o_ref.dtype)


def _layout_w_in(w_in_t):
    depth, _, d = w_in_t.shape
    src, is_kr = _w_in_layout_table()
    return pl.pallas_call(
        _w_in_layout_kernel,
        grid_spec=pltpu.PrefetchScalarGridSpec(
            num_scalar_prefetch=2,
            grid=(depth, W_ALL_WIDTH // W_IN_ROWS),
            in_specs=[pl.BlockSpec((pl.Element(1), pl.Element(W_IN_ROWS), pl.Element(d)),
                                   lambda l, r, src, is_kr: (l, pl.multiple_of(src[r], B_ROPE), 0))],
            out_specs=pl.BlockSpec((None, W_IN_ROWS, d), lambda l, r, src, is_kr: (l, r, 0))),
        out_shape=jax.ShapeDtypeStruct((depth, W_ALL_WIDTH, d), MXU_DTYPE),
        compiler_params=_cparams("parallel", "arbitrary"),
        name="w_in_layout",
    )(src, is_kr, w_in_t)


def _cast_kernel(w_ref, o_ref):
    o_ref[...] = w_ref[...].astype(o_ref.dtype)


def _cast_weights(w, tr=512):
    depth, rows, cols = w.shape
    tr = min(tr, rows)
    spec = pl.BlockSpec((1, tr, cols), lambda l, r: (l, r, 0))
    return pl.pallas_call(
        _cast_kernel,
        grid=(depth, rows // tr),
        in_specs=[spec],
        out_specs=spec,
        out_shape=jax.ShapeDtypeStruct(w.shape, MXU_DTYPE),
        compiler_params=_cparams("parallel", "parallel"),
        name="cast_weights",
    )(w)


def _layout_w_q_up(w):
    parts = []
    for h in range(B_HEADS):
        nope = w[:, h * B_QK:h * B_QK + B_NOPE]
        rope = w[:, h * B_QK + B_NOPE:(h + 1) * B_QK]
        parts += [nope, _pad_cols(rope, 128), _pad_cols(_swap_halves(rope), 128)]
    return jnp.concatenate(parts, axis=1).astype(MXU_DTYPE)


def _layout_w_kv_up(w):
    per = B_NOPE + B_V
    ks = [w[:, h * per:h * per + B_NOPE] for h in range(B_HEADS)]
    vs = [w[:, h * per + B_NOPE:(h + 1) * per] for h in range(B_HEADS)]
    return jnp.concatenate(ks + vs, axis=1).astype(MXU_DTYPE)


def _rope_table(seq):
    pos = jnp.arange(seq, dtype=jnp.float32)
    inv = ROPE_THETA ** (-jnp.arange(0, B_ROPE, 2, dtype=jnp.float32) / B_ROPE)
    ang = pos[:, None] * inv[None, :]
    cos, sin = jnp.cos(ang), jnp.sin(ang)
    zeros = jnp.zeros((seq, 128 - B_ROPE), jnp.float32)
    return jnp.concatenate([cos, cos, zeros, -sin, sin, zeros], axis=1)


def kernel(x, norm_g, w_in, attn_sinks, rel_bias, g_q_lora, w_q_up, g_kv_lora, w_kv_up,
           w_proj_a, w_proj_b, w_proj_c, w_out, final_g):
    batch, seq, d = x.shape
    assert batch == 1 and d == D_MODEL and seq % 1024 == 0
    depth = w_in.shape[0]
    xs = x.reshape(seq, d)
    bias = _swa_bias(rel_bias)
    rope = _rope_table(seq)
    w_all = _layout_w_in(jnp.swapaxes(w_in, 1, 2))
    wa, wb, wc, wo = (_cast_weights(w) for w in (w_proj_a, w_proj_b, w_proj_c, w_out))

    for l in range(depth):
        h = _rmsnorm(xs, norm_g[l], MXU_DTYPE)
        p16 = _matmul(h, w_all, l, 0, P16_WIDTH, MXU_DTYPE, tm=1024, tn=768, transposed_w=True,
                      name="in_proj_bf16")
        pf = _matmul(h, w_all, l, P16_WIDTH, PF_WIDTH, jnp.float32, tm=1024, tn=768,
                     transposed_w=True, name="in_proj_f32")

        ya = _swa(p16, pf, attn_sinks[l], bias)
        qf, kf, vb = _mla_pre(pf, g_q_lora[l], g_kv_lora[l], _layout_w_q_up(w_q_up[l]),
                              _layout_w_kv_up(w_kv_up[l]), rope)
        yb = _mla(qf, kf, vb, pf)
        yc = _sb(p16, pf)

        merged = _merge(h, w_all, l, ya, yb, yc, wa, wb, wc)
        xs = _matmul(merged, wo, l, 0, D_MODEL, jnp.float32, tm=1024, tn=512, residual=xs,
                     name="out_proj")

    out = _rmsnorm(xs, final_g, jnp.float32)
    return out.reshape(batch, seq, d)
```

```python
import functools
import math

import jax
import jax.numpy as jnp
from jax import lax
from jax.experimental import pallas as pl
from jax.experimental.pallas import tpu as pltpu

D_MODEL = 4096
HEAD_DIM = 128
LANES = 128
BLOCK = 128
EPS = 1e-6
A_HEADS = 8
A_KV_HEADS = 2
A_GROUP = A_HEADS // A_KV_HEADS
A_WIDTH = A_HEADS * HEAD_DIM
A_KV_WIDTH = A_KV_HEADS * HEAD_DIM
NUM_BUCKETS = 32
MAX_DISTANCE = 128
B_HEADS = 4
B_Q_LORA = 1024
B_KV_LORA = 512
B_NOPE = 128
B_ROPE = 64
B_V = 128
B_QK = B_NOPE + B_ROPE
B_QK_PAD = 256
B_WIDTH = B_HEADS * B_V
ROPE_THETA = 10000.0
C_HEADS = 4
C_WIDTH = C_HEADS * HEAD_DIM

MXU_DTYPE = jnp.bfloat16
LOG2E = math.log2(math.e)
VMEM_LIMIT_BYTES = 56 * 1024 * 1024

_IN_SIZES = (A_WIDTH, A_KV_WIDTH, A_KV_WIDTH, A_WIDTH, B_Q_LORA, B_KV_LORA, B_ROPE, B_WIDTH,
             C_WIDTH, C_WIDTH, C_WIDTH, C_WIDTH, D_MODEL, D_MODEL, D_MODEL)
_IN_NAMES = ("qa", "ka", "va", "za", "cq", "ckv", "kr", "zb", "qc", "kc", "vc", "zc", "ga", "gb", "gc")
_IN_OFF = {}
_o = 0
for _n, _s in zip(_IN_NAMES, _IN_SIZES):
    _IN_OFF[_n] = (_o, _s)
    _o += _s

_P16_COLS = (("qa", A_WIDTH), ("ka", A_KV_WIDTH), ("va", A_KV_WIDTH),
             ("qc", C_WIDTH), ("kc", C_WIDTH), ("vc", C_WIDTH))
_PF_COLS = (("za", A_WIDTH), ("cq", B_Q_LORA), ("ckv", B_KV_LORA), ("kr_a", LANES), ("kr_b", LANES),
            ("zb", B_WIDTH), ("zc", C_WIDTH))


def _offsets(cols):
    out, o = {}, 0
    for n, s in cols:
        out[n] = o
        o += s
    return out, o


_P16_OFF, P16_WIDTH = _offsets(_P16_COLS)
_PF_OFF, PF_WIDTH = _offsets(_PF_COLS)
GATE_OFF = P16_WIDTH + PF_WIDTH


def _cparams(*sem):
    return pltpu.CompilerParams(dimension_semantics=sem, vmem_limit_bytes=VMEM_LIMIT_BYTES)


def _dot(a, b):
    return jnp.dot(a, b, preferred_element_type=jnp.float32)


def _dot_nt(a, b):
    return lax.dot_general(a, b, (((1,), (1,)), ((), ())), preferred_element_type=jnp.float32)


def _silu(z):
    return z * jax.nn.sigmoid(z)


def _neg_abs(x):
    bits = lax.bitcast_convert_type(x, jnp.uint32) | jnp.uint32(0x80000000)
    return lax.bitcast_convert_type(bits, jnp.float32)


NORM_ROWS = 16


def _rmsnorm_kernel(x_ref, g_ref, o_ref):
    def body(r, carry):
        rows = pl.ds(pl.multiple_of(r * NORM_ROWS, NORM_ROWS), NORM_ROWS)
        x = x_ref[rows, :]
        y = x * lax.rsqrt(jnp.mean(x * x, axis=-1, keepdims=True) + EPS)
        o_ref[rows, :] = (y * g_ref[...]).astype(o_ref.dtype)
        return carry

    lax.fori_loop(0, x_ref.shape[0] // NORM_ROWS, body, 0, unroll=8)


def _rmsnorm(x, g, out_dtype, tm=512):
    s, d = x.shape
    return pl.pallas_call(
        _rmsnorm_kernel,
        grid=(s // tm,),
        in_specs=[pl.BlockSpec((tm, d), lambda i: (i, 0)),
                  pl.BlockSpec((1, d), lambda i: (0, 0))],
        out_specs=pl.BlockSpec((tm, d), lambda i: (i, 0)),
        out_shape=jax.ShapeDtypeStruct((s, d), out_dtype),
        compiler_params=_cparams("parallel"),
        name="rmsnorm",
    )(x, g.reshape(1, d))


def _matmul_kernel(a_ref, w_ref, o_ref):
    o_ref[...] = _dot(a_ref[...], w_ref[...]).astype(o_ref.dtype)


def _matmul_nt_kernel(a_ref, wt_ref, o_ref):
    o_ref[...] = _dot_nt(a_ref[...], wt_ref[...]).astype(o_ref.dtype)


def _matmul_residual_kernel(a_ref, w_ref, r_ref, o_ref):
    o_ref[...] = r_ref[...] + _dot(a_ref[...], w_ref[...])


def _matmul(a, w, layer, col_off, n_cols, out_dtype, tm, tn, residual=None, transposed_w=False,
            name="matmul"):
    s, k = a.shape
    assert s % tm == 0 and n_cols % tn == 0 and col_off % tn == 0
    assert not (transposed_w and residual is not None)
    jo = col_off // tn
    if transposed_w:
        w_spec = pl.BlockSpec((None, tn, k), lambda i, j: (layer, j + jo, 0))
    else:
        w_spec = pl.BlockSpec((None, k, tn), lambda i, j: (layer, 0, j + jo))
    in_specs = [pl.BlockSpec((tm, k), lambda i, j: (i, 0)), w_spec]
    args = [a, w]
    body = _matmul_nt_kernel if transposed_w else _matmul_kernel
    if residual is not None:
        in_specs.append(pl.BlockSpec((tm, tn), lambda i, j: (i, j)))
        args.append(residual)
        body = _matmul_residual_kernel
    return pl.pallas_call(
        body,
        grid=(s // tm, n_cols // tn),
        in_specs=in_specs,
        out_specs=pl.BlockSpec((tm, tn), lambda i, j: (i, j)),
        out_shape=jax.ShapeDtypeStruct((s, n_cols), out_dtype),
        compiler_params=_cparams("parallel", "arbitrary"),
        name=name,
    )(*args)


def _merge_kernel(h_ref, wga_ref, wgb_ref, wgc_ref, ya_ref, yb_ref, yc_ref,
                  wa_ref, wb_ref, wc_ref, o_ref):
    h = h_ref[...]

    def branch(wg_ref, y_ref, w_ref):
        return jax.nn.sigmoid(_dot_nt(h, wg_ref[...])) * _dot(y_ref[...], w_ref[...])

    merged = (branch(wga_ref, ya_ref, wa_ref) + branch(wgb_ref, yb_ref, wb_ref)
              + branch(wgc_ref, yc_ref, wc_ref))
    o_ref[...] = merged.astype(o_ref.dtype)


def _merge(h, w_all, layer, ya, yb, yc, wa, wb, wc, tm=1024, tn=256):
    s = h.shape[0]
    nj = D_MODEL // tn
    g0 = GATE_OFF // tn

    def gate_spec(k):
        return pl.BlockSpec((None, tn, D_MODEL), lambda i, j: (layer, g0 + k * nj + j, 0))

    def row_spec(width):
        return pl.BlockSpec((tm, width), lambda i, j: (i, 0))

    def w_spec(width):
        return pl.BlockSpec((None, width, tn), lambda i, j: (layer, 0, j))

    return pl.pallas_call(
        _merge_kernel,
        grid=(s // tm, nj),
        in_specs=[row_spec(D_MODEL), gate_spec(0), gate_spec(1), gate_spec(2),
                  row_spec(A_WIDTH), row_spec(B_WIDTH), row_spec(C_WIDTH),
                  w_spec(A_WIDTH), w_spec(B_WIDTH), w_spec(C_WIDTH)],
        out_specs=pl.BlockSpec((tm, tn), lambda i, j: (i, j)),
        out_shape=jax.ShapeDtypeStruct((s, D_MODEL), MXU_DTYPE),
        compiler_params=_cparams("parallel", "arbitrary"),
        name="gated_merge",
    )(h, w_all, w_all, w_all, ya, yb, yc, wa, wb, wc)


def _t5_bucket(rel):
    n = jnp.maximum(rel, 0)
    max_exact = NUM_BUCKETS // 2
    logn = jnp.log(jnp.maximum(n, 1).astype(jnp.float32) / max_exact)
    large = max_exact + (logn / math.log(MAX_DISTANCE / max_exact)
                         * (NUM_BUCKETS - max_exact)).astype(jnp.int32)
    large = jnp.minimum(large, NUM_BUCKETS - 1)
    return jnp.where(n < max_exact, n, large)


def _swa_bias_kernel(rel_bias_ref, bucket_ref, o_ref):
    bucket = bucket_ref[...]
    for h in range(A_HEADS):
        acc = jnp.zeros(bucket.shape, jnp.float32)
        for b in range(NUM_BUCKETS):
            acc = jnp.where(bucket == b, rel_bias_ref[b, h], acc)
        o_ref[h] = acc


def _swa_bias(rel_bias):
    t = jnp.arange(BLOCK)[:, None]
    s = jnp.arange(2 * BLOCK)[None, :]
    bucket = _t5_bucket(BLOCK + t - s).astype(jnp.int32)
    return pl.pallas_call(
        _swa_bias_kernel,
        in_specs=[pl.BlockSpec(memory_space=pltpu.SMEM),
                  pl.BlockSpec((BLOCK, 2 * BLOCK), lambda: (0, 0))],
        out_specs=pl.BlockSpec((A_HEADS, BLOCK, 2 * BLOCK), lambda: (0, 0, 0)),
        out_shape=jax.ShapeDtypeStruct((A_HEADS, BLOCK, 2 * BLOCK), jnp.float32),
        name="swa_bias_table",
    )(rel_bias, bucket)


def _swa_kernel(sinks_ref, q_ref, kc_ref, kp_ref, vc_ref, vp_ref, bias_ref, z_ref, o_ref):
    n = pl.program_id(0)
    rows = A_GROUP * BLOCK
    t = lax.broadcasted_iota(jnp.int32, (rows, 2 * BLOCK), 0) & (BLOCK - 1)
    s = lax.broadcasted_iota(jnp.int32, (rows, 2 * BLOCK), 1)
    ok = (s > jnp.maximum(t, jnp.where(n > 0, -1, BLOCK - 1))) & (s <= t + BLOCK)
    scale = HEAD_DIM ** -0.5

    def head_cols(h):
        return slice(h * HEAD_DIM, (h + 1) * HEAD_DIM)

    scores = []
    for hk in range(A_KV_HEADS):
        q = jnp.concatenate([q_ref[:, head_cols(hk * A_GROUP + g)] for g in range(A_GROUP)], axis=0)
        kk = jnp.concatenate([kp_ref[:, head_cols(hk)], kc_ref[:, head_cols(hk)]], axis=0)
        scores.append(_dot_nt(q, kk))
    probs = []
    for hk in range(A_KV_HEADS):
        sc = jnp.where(ok, scores[hk] * scale + bias_ref[hk], -jnp.inf)
        sink = jnp.concatenate([jnp.full((BLOCK, LANES), sinks_ref[hk * A_GROUP + g], jnp.float32)
                                for g in range(A_GROUP)], axis=0)
        m = jnp.maximum(jnp.max(sc, axis=-1, keepdims=True), sink)
        e = jnp.exp(sc - jnp.concatenate([m, m], axis=-1))
        inv = 1.0 / (jnp.sum(e, axis=-1, keepdims=True) + jnp.exp(sink - m))
        probs.append((e * jnp.concatenate([inv, inv], axis=-1)).astype(MXU_DTYPE))
    for hk in range(A_KV_HEADS):
        vv = jnp.concatenate([vp_ref[:, head_cols(hk)], vc_ref[:, head_cols(hk)]], axis=0)
        out = _dot(probs[hk], vv)
        for g in range(A_GROUP):
            cols = head_cols(hk * A_GROUP + g)
            o_ref[:, cols] = (out[g * BLOCK:(g + 1) * BLOCK] * _silu(z_ref[:, cols])).astype(o_ref.dtype)


def _swa(p16, pf, sinks, bias):
    s = p16.shape[0]
    qb = _P16_OFF["qa"] // A_WIDTH
    kb = _P16_OFF["ka"] // A_KV_WIDTH
    vb = _P16_OFF["va"] // A_KV_WIDTH
    zb = _PF_OFF["za"] // A_WIDTH
    prev = lambda n: jnp.maximum(n - 1, 0)
    return pl.pallas_call(
        _swa_kernel,
        grid=(s // BLOCK,),
        in_specs=[pl.BlockSpec(memory_space=pltpu.SMEM),
                  pl.BlockSpec((BLOCK, A_WIDTH), lambda n: (n, qb)),
                  pl.BlockSpec((BLOCK, A_KV_WIDTH), lambda n: (n, kb)),
                  pl.BlockSpec((BLOCK, A_KV_WIDTH), lambda n: (prev(n), kb)),
                  pl.BlockSpec((BLOCK, A_KV_WIDTH), lambda n: (n, vb)),
                  pl.BlockSpec((BLOCK, A_KV_WIDTH), lambda n: (prev(n), vb)),
                  pl.BlockSpec((A_KV_HEADS, A_GROUP * BLOCK, 2 * BLOCK), lambda n: (0, 0, 0)),
                  pl.BlockSpec((BLOCK, A_WIDTH), lambda n: (n, zb))],
        out_specs=pl.BlockSpec((BLOCK, A_WIDTH), lambda n: (n, 0)),
        out_shape=jax.ShapeDtypeStruct((s, A_WIDTH), MXU_DTYPE),
        compiler_params=_cparams("parallel"),
        name="swa_attention",
    )(sinks, p16, p16, p16, p16, p16, bias.reshape(A_KV_HEADS, A_GROUP * BLOCK, 2 * BLOCK), pf)


def _rms_cast(x, g):
    y = x * lax.rsqrt(jnp.mean(x * x, axis=-1, keepdims=True) + EPS)
    return (y * g).astype(MXU_DTYPE)


def _mla_pre_kernel(cq_ref, ckv_ref, kr_ref, gq_ref, gkv_ref, wq_ref, wkv_ref, rope_ref,
                    qf_ref, kf_ref, v_ref):
    tc = rope_ref[:, 0:LANES]
    ts = rope_ref[:, LANES:2 * LANES]
    qraw = _dot(_rms_cast(cq_ref[...], gq_ref[...]), wq_ref[...])
    for h in range(B_HEADS):
        nope, rope, swapped = (qraw[:, (3 * h + t) * LANES:(3 * h + t + 1) * LANES] for t in range(3))
        qf_ref[:, h * B_QK_PAD:h * B_QK_PAD + B_NOPE] = nope.astype(qf_ref.dtype)
        qf_ref[:, h * B_QK_PAD + B_NOPE:(h + 1) * B_QK_PAD] = (rope * tc + swapped * ts).astype(qf_ref.dtype)
    kvraw = _dot(_rms_cast(ckv_ref[...], gkv_ref[...]), wkv_ref[...])
    kr = kr_ref[...]
    krot = (kr[:, 0:LANES] * tc + kr[:, LANES:2 * LANES] * ts).astype(kf_ref.dtype)
    for h in range(B_HEADS):
        kf_ref[:, h * B_QK_PAD:h * B_QK_PAD + B_NOPE] = (
            kvraw[:, h * B_NOPE:(h + 1) * B_NOPE].astype(kf_ref.dtype))
        kf_ref[:, h * B_QK_PAD + B_NOPE:(h + 1) * B_QK_PAD] = krot
    v_ref[...] = kvraw[:, B_HEADS * B_NOPE:].astype(v_ref.dtype)


def _mla_pre(pf, gq, gkv, wq, wkv, rope, tm=512):
    s = pf.shape[0]
    row = lambda width, off: pl.BlockSpec((tm, width), lambda i: (i, off // width))
    full = lambda a: pl.BlockSpec(a.shape, lambda i: (0, 0))
    gq = gq.reshape(1, B_Q_LORA)
    gkv = gkv.reshape(1, B_KV_LORA)
    out_row = lambda width: pl.BlockSpec((tm, width), lambda i: (i, 0))
    return pl.pallas_call(
        _mla_pre_kernel,
        grid=(s // tm,),
        in_specs=[row(B_Q_LORA, _PF_OFF["cq"]), row(B_KV_LORA, _PF_OFF["ckv"]),
                  row(2 * LANES, _PF_OFF["kr_a"]), full(gq), full(gkv), full(wq), full(wkv),
                  pl.BlockSpec((tm, 2 * LANES), lambda i: (i, 0))],
        out_specs=[out_row(B_HEADS * B_QK_PAD), out_row(B_HEADS * B_QK_PAD), out_row(B_WIDTH)],
        out_shape=[jax.ShapeDtypeStruct((s, B_HEADS * B_QK_PAD), MXU_DTYPE),
                   jax.ShapeDtypeStruct((s, B_HEADS * B_QK_PAD), MXU_DTYPE),
                   jax.ShapeDtypeStruct((s, B_WIDTH), MXU_DTYPE)],
        compiler_params=_cparams("parallel"),
        name="mla_pre",
    )(pf, pf, pf, gq, gkv, wq, wkv, rope)


def _run_pipeline(stages, n, n_static, unroll):
    depth = len(stages)
    t0 = n_static + 1
    assert unroll % 2 == 0 and depth >= 2 and t0 >= depth - 1

    def run(k, pos, slot, static_pos=None):
        stages[k](pos, slot, static_pos)

    for tt in range(t0):
        for k in reversed(range(depth)):
            pos = tt - k
            if pos < 0:
                continue
            if pos < n_static:
                run(k, pos, pos % 2, pos)
            else:
                pl.when(n > pos)(functools.partial(run, k, pos, pos % 2))

    def steady(tt, slot):
        for k in reversed(range(depth)):
            run(k, tt - k, (slot + k) % 2)

    def body(u, carry):
        for j in range(unroll):
            steady(t0 + unroll * u + j, (t0 + j) % 2)
        return carry

    trips = jnp.maximum(n - t0, 0) // unroll
    lax.fori_loop(0, trips, body, 0)
    start = t0 + unroll * trips
    left = jnp.maximum(n - start, 0)
    width = unroll // 2
    while width >= 1:

        def block(start=start, width=width):
            for j in range(width):
                steady(start + j, (t0 + j) % 2)

        pl.when((left & width) != 0)(block)
        start = start + (left & width)
        width //= 2

    def drain(first):
        for e in range(0 if first else 1, depth - 1):
            for k in reversed(range(e + 1, depth)):
                run(k, n + e - k, (n + e - k) % 2)

    pl.when(n >= t0)(functools.partial(drain, True))
    pl.when(n < t0)(functools.partial(drain, False))


MLA_UNROLL = 4


def _mla_kernel(q_ref, k_ref, v_ref, z_ref, o_ref,
                y_scr, p_scr, alpha_scr, m_scr, l_scr, acc_scr, *, tq, tk):
    n_diag = tq // tk
    i = pl.program_id(1)
    q = q_ref[...]
    c = (B_QK ** -0.5) * LOG2E

    def key_rows(pos):
        kb = jnp.where(pos < n_diag, n_diag * i + pos, pos - n_diag)
        return pl.ds(pl.multiple_of(kb * tk, tk), tk)

    def scores(pos, slot, static_pos):
        y_scr[slot] = _dot_nt(q, k_ref[key_rows(pos), :]) * c

    def softmax(pos, slot, static_pos):
        y = y_scr[slot]
        if static_pos is not None:
            row = lax.broadcasted_iota(jnp.int32, (tq, tk), 0)
            col = lax.broadcasted_iota(jnp.int32, (tq, tk), 1) + static_pos * tk
            y = jnp.where(col <= row, y, -jnp.inf)
        m = m_scr[...]
        m_new = jnp.maximum(m, jnp.max(y, axis=-1, keepdims=True))
        alpha = jnp.exp2(m - m_new)
        p = jnp.exp2(y - jnp.concatenate([m_new] * (tk // LANES), axis=-1))
        l_scr[...] = alpha * l_scr[...] + jnp.sum(p, axis=-1, keepdims=True)
        m_scr[...] = m_new
        alpha_scr[slot] = alpha
        p_scr[slot] = p.astype(MXU_DTYPE)

    def accumulate(pos, slot, static_pos):
        acc_scr[...] = alpha_scr[slot] * acc_scr[...] + _dot(p_scr[slot], v_ref[key_rows(pos), :])

    m_scr[...] = jnp.full((tq, LANES), -jnp.inf, jnp.float32)
    l_scr[...] = jnp.zeros((tq, LANES), jnp.float32)
    acc_scr[...] = jnp.zeros((tq, B_V), jnp.float32)
    _run_pipeline([scores, softmax, accumulate], n_diag * (i + 1), n_diag, MLA_UNROLL)
    o_ref[...] = ((acc_scr[...] / l_scr[...]) * _silu(z_ref[...])).astype(o_ref.dtype)


def _mla(qf, kf, v, pf, tq=1024, tk=512):
    s = qf.shape[0]
    zb = _PF_OFF["zb"] // B_V
    return pl.pallas_call(
        functools.partial(_mla_kernel, tq=tq, tk=tk),
        grid=(B_HEADS, s // tq),
        in_specs=[pl.BlockSpec((tq, B_QK_PAD), lambda h, i: (i, h)),
                  pl.BlockSpec((s, B_QK_PAD), lambda h, i: (0, h)),
                  pl.BlockSpec((s, B_V), lambda h, i: (0, h)),
                  pl.BlockSpec((tq, B_V), lambda h, i: (i, zb + h))],
        out_specs=pl.BlockSpec((tq, B_V), lambda h, i: (i, h)),
        out_shape=jax.ShapeDtypeStruct((s, B_WIDTH), MXU_DTYPE),
        scratch_shapes=[pltpu.VMEM((2, tq, tk), jnp.float32),
                        pltpu.VMEM((2, tq, tk), MXU_DTYPE),
                        pltpu.VMEM((2, tq, LANES), jnp.float32),
                        pltpu.VMEM((tq, LANES), jnp.float32),
                        pltpu.VMEM((tq, LANES), jnp.float32),
                        pltpu.VMEM((tq, B_V), jnp.float32)],
        compiler_params=_cparams("parallel", "arbitrary"),
        name="mla_attention",
    )(qf, kf, v, pf)


SB_UNROLL = 4


def _sb_kernel(q_ref, k_ref, v_ref, z_ref, u_ref, o_ref,
               s_scr, drop_scr, lb_scr, off_scr, off_run_scr, w_scr, acc_scr, *, tq, tk):
    n_diag = tq // tk
    i = pl.program_id(1)
    q = q_ref[...]
    scale = HEAD_DIM ** -0.5
    n = n_diag * (i + 1)

    def key_rows(pos):
        return pl.ds(pl.multiple_of((n - 1 - pos) * tk, tk), tk)

    def score(pos, slot, static_pos):
        s_scr[slot] = _dot_nt(q, k_ref[key_rows(pos), :])

    def park(pos, slot, static_pos):
        y = s_scr[slot] * (scale * LOG2E)
        l1p = jnp.log2(1.0 + jnp.exp2(_neg_abs(y)))
        drop = jnp.maximum(y, 0.0) + l1p
        log_beta = y - drop
        if static_pos is not None:
            row = lax.broadcasted_iota(jnp.int32, (tq, tk), 0)
            col = lax.broadcasted_iota(jnp.int32, (tq, tk), 1) + (n_diag - 1 - static_pos) * tk
            past = col < row
            drop = jnp.where(past, drop, 0.0)
            log_beta = jnp.where(past, log_beta, -jnp.inf)
        drop_scr[slot] = drop.astype(MXU_DTYPE)
        lb_scr[slot] = log_beta
        off = off_run_scr[...]
        off_scr[slot] = off
        off_run_scr[...] = off + jnp.sum(drop, axis=-1, keepdims=True)

    def weigh(pos, slot, static_pos):
        within = _dot(drop_scr[slot], u_ref[...])
        off = jnp.concatenate([off_scr[slot]] * (tk // LANES), axis=-1)
        w_scr[slot] = jnp.exp2(lb_scr[slot] - (within + off)).astype(MXU_DTYPE)

    def accumulate(pos, slot, static_pos):
        acc_scr[...] += _dot(w_scr[slot], v_ref[key_rows(pos), :])

    off_run_scr[...] = jnp.zeros((tq, LANES), jnp.float32)
    acc_scr[...] = jnp.zeros((tq, HEAD_DIM), jnp.float32)
    _run_pipeline([score, park, weigh, accumulate], n, n_diag, SB_UNROLL)
    o_ref[...] = (acc_scr[...] * _silu(z_ref[...])).astype(o_ref.dtype)


def _sb(p16, pf, tq=1024, tk=256):
    s = p16.shape[0]
    qb = _P16_OFF["qc"] // HEAD_DIM
    kb = _P16_OFF["kc"] // HEAD_DIM
    vb = _P16_OFF["vc"] // HEAD_DIM
    zb = _PF_OFF["zc"] // HEAD_DIM
    j = lax.broadcasted_iota(jnp.int32, (tk, tk), 0)
    c = lax.broadcasted_iota(jnp.int32, (tk, tk), 1)
    u = (j > c).astype(MXU_DTYPE)
    return pl.pallas_call(
        functools.partial(_sb_kernel, tq=tq, tk=tk),
        grid=(C_HEADS, s // tq),
        in_specs=[pl.BlockSpec((tq, HEAD_DIM), lambda h, i: (i, qb + h)),
                  pl.BlockSpec((s, HEAD_DIM), lambda h, i: (0, kb + h)),
                  pl.BlockSpec((s, HEAD_DIM), lambda h, i: (0, vb + h)),
                  pl.BlockSpec((tq, HEAD_DIM), lambda h, i: (i, zb + h)),
                  pl.BlockSpec((tk, tk), lambda h, i: (0, 0))],
        out_specs=pl.BlockSpec((tq, HEAD_DIM), lambda h, i: (i, h)),
        out_shape=jax.ShapeDtypeStruct((s, C_WIDTH), MXU_DTYPE),
        scratch_shapes=[pltpu.VMEM((2, tq, tk), jnp.float32),
                        pltpu.VMEM((2, tq, tk), MXU_DTYPE),
                        pltpu.VMEM((2, tq, tk), jnp.float32),
                        pltpu.VMEM((2, tq, LANES), jnp.float32),
                        pltpu.VMEM((tq, LANES), jnp.float32),
                        pltpu.VMEM((2, tq, tk), MXU_DTYPE),
                        pltpu.VMEM((tq, HEAD_DIM), jnp.float32)],
        compiler_params=_cparams("parallel", "arbitrary"),
        name="stick_breaking_attention",
    )(p16, p16, p16, pf, u)


def _swap_halves(w):
    half = w.shape[-1] // 2
    return jnp.concatenate([w[..., half:], w[..., :half]], axis=-1)


def _pad_cols(w, width):
    return jnp.pad(w, ((0, 0), (0, width - w.shape[-1])))


_GATE_COLS = (("ga", D_MODEL), ("gb", D_MODEL), ("gc", D_MODEL))
W_ALL_WIDTH = GATE_OFF + 3 * D_MODEL


W_IN_ROWS = 256


def _w_in_layout_table():
    src, is_kr = [], []
    for name, width in _P16_COLS + _PF_COLS + _GATE_COLS:
        if name == "kr_a":
            src.append(_IN_OFF["kr"][0])
            is_kr.append(1)
        elif name != "kr_b":
            assert width % W_IN_ROWS == 0
            src += [_IN_OFF[name][0] + t * W_IN_ROWS for t in range(width // W_IN_ROWS)]
            is_kr += [0] * (width // W_IN_ROWS)
    return jnp.asarray(src, jnp.int32), jnp.asarray(is_kr, jnp.int32)


def _w_in_layout_kernel(src_ref, is_kr_ref, w_ref, o_ref):
    x = w_ref[0]
    half = B_ROPE // 2
    kr = x[:B_ROPE]
    pad = jnp.zeros((LANES - B_ROPE, x.shape[1]), x.dtype)
    kr_rows = jnp.concatenate([kr, pad, kr[half:], kr[:half], pad], axis=0)
    x = jnp.where(is_kr_ref[pl.program_id(1)] == 1, kr_rows, x)
    o_ref[...] = x.astype(o_ref.dtype)


def _layout_w_in(w_in_t):
    depth, _, d = w_in_t.shape
    src, is_kr = _w_in_layout_table()
    return pl.pallas_call(
        _w_in_layout_kernel,
        grid_spec=pltpu.PrefetchScalarGridSpec(
            num_scalar_prefetch=2,
            grid=(depth, W_ALL_WIDTH // W_IN_ROWS),
            in_specs=[pl.BlockSpec((pl.Element(1), pl.Element(W_IN_ROWS), pl.Element(d)),
                                   lambda l, r, src, is_kr: (l, pl.multiple_of(src[r], B_ROPE), 0))],
            out_specs=pl.BlockSpec((None, W_IN_ROWS, d), lambda l, r, src, is_kr: (l, r, 0))),
        out_shape=jax.ShapeDtypeStruct((depth, W_ALL_WIDTH, d), MXU_DTYPE),
        compiler_params=_cparams("parallel", "arbitrary"),
        name="w_in_layout",
    )(src, is_kr, w_in_t)


def _cast_kernel(w_ref, o_ref):
    o_ref[...] = w_ref[...].astype(o_ref.dtype)


def _cast_weights(w, tr=512):
    depth, rows, cols = w.shape
    tr = min(tr, rows)
    spec = pl.BlockSpec((1, tr, cols), lambda l, r: (l, r, 0))
    return pl.pallas_call(
        _cast_kernel,
        grid=(depth, rows // tr),
        in_specs=[spec],
        out_specs=spec,
        out_shape=jax.ShapeDtypeStruct(w.shape, MXU_DTYPE),
        compiler_params=_cparams("parallel", "parallel"),
        name="cast_weights",
    )(w)


def _layout_w_q_up(w):
    parts = []
    for h in range(B_HEADS):
        nope = w[:, h * B_QK:h * B_QK + B_NOPE]
        rope = w[:, h * B_QK + B_NOPE:(h + 1) * B_QK]
        parts += [nope, _pad_cols(rope, LANES), _pad_cols(_swap_halves(rope), LANES)]
    return jnp.concatenate(parts, axis=1).astype(MXU_DTYPE)


def _layout_w_kv_up(w):
    per = B_NOPE + B_V
    ks = [w[:, h * per:h * per + B_NOPE] for h in range(B_HEADS)]
    vs = [w[:, h * per + B_NOPE:(h + 1) * per] for h in range(B_HEADS)]
    return jnp.concatenate(ks + vs, axis=1).astype(MXU_DTYPE)


def _rope_table(seq):
    pos = jnp.arange(seq, dtype=jnp.float32)
    inv = ROPE_THETA ** (-jnp.arange(0, B_ROPE, 2, dtype=jnp.float32) / B_ROPE)
    ang = pos[:, None] * inv[None, :]
    cos, sin = jnp.cos(ang), jnp.sin(ang)
    zeros = jnp.zeros((seq, LANES - B_ROPE), jnp.float32)
    return jnp.concatenate([cos, cos, zeros, -sin, sin, zeros], axis=1)


def kernel(x, norm_g, w_in, attn_sinks, rel_bias, g_q_lora, w_q_up, g_kv_lora, w_kv_up,
           w_proj_a, w_proj_b, w_proj_c, w_out, final_g):
    batch, seq, d = x.shape
    assert batch == 1 and d == D_MODEL and seq % 1024 == 0
    depth = w_in.shape[0]
    xs = x.reshape(seq, d)
    bias = _swa_bias(rel_bias)
    rope = _rope_table(seq)
    w_all = _layout_w_in(jnp.swapaxes(w_in, 1, 2))
    wa, wb, wc, wo = (_cast_weights(w) for w in (w_proj_a, w_proj_b, w_proj_c, w_out))

    for l in range(depth):
        h = _rmsnorm(xs, norm_g[l], MXU_DTYPE)
        p16 = _matmul(h, w_all, l, 0, P16_WIDTH, MXU_DTYPE, tm=1024, tn=768, transposed_w=True,
                      name="in_proj_bf16")
        pf = _matmul(h, w_all, l, P16_WIDTH, PF_WIDTH, jnp.float32, tm=1024, tn=768,
                     transposed_w=True, name="in_proj_f32")

        ya = _swa(p16, pf, attn_sinks[l], bias)
        qf, kf, vb = _mla_pre(pf, g_q_lora[l], g_kv_lora[l], _layout_w_q_up(w_q_up[l]),
                              _layout_w_kv_up(w_kv_up[l]), rope)
        yb = _mla(qf, kf, vb, pf)
        yc = _sb(p16, pf)

        merged = _merge(h, w_all, l, ya, yb, yc, wa, wb, wc)
        xs = _matmul(merged, wo, l, 0, D_MODEL, jnp.float32, tm=1024, tn=512, residual=xs,
                     name="out_proj")

    out = _rmsnorm(xs, final_g, jnp.float32)
    return out.reshape(batch, seq, d)
```

```python
import functools
import math

import jax
import jax.numpy as jnp
from jax import lax
from jax.experimental import pallas as pl
from jax.experimental.pallas import tpu as pltpu

D_MODEL = 4096
HEAD_DIM = 128
LANES = 128
BLOCK = 128
EPS = 1e-6
A_HEADS = 8
A_KV_HEADS = 2
A_GROUP = A_HEADS // A_KV_HEADS
A_WIDTH = A_HEADS * HEAD_DIM
A_KV_WIDTH = A_KV_HEADS * HEAD_DIM
NUM_BUCKETS = 32
MAX_DISTANCE = 128
B_HEADS = 4
B_Q_LORA = 1024
B_KV_LORA = 512
B_NOPE = 128
B_ROPE = 64
B_V = 128
B_QK = B_NOPE + B_ROPE
B_QK_PAD = 256
B_WIDTH = B_HEADS * B_V
ROPE_THETA = 10000.0
C_HEADS = 4
C_WIDTH = C_HEADS * HEAD_DIM

MXU_DTYPE = jnp.bfloat16
LOG2E = math.log2(math.e)
VMEM_LIMIT_BYTES = 56 * 1024 * 1024

_IN_SIZES = (A_WIDTH, A_KV_WIDTH, A_KV_WIDTH, A_WIDTH, B_Q_LORA, B_KV_LORA, B_ROPE, B_WIDTH,
             C_WIDTH, C_WIDTH, C_WIDTH, C_WIDTH, D_MODEL, D_MODEL, D_MODEL)
_IN_NAMES = ("qa", "ka", "va", "za", "cq", "ckv", "kr", "zb", "qc", "kc", "vc", "zc", "ga", "gb", "gc")
_IN_OFF = {}
_o = 0
for _n, _s in zip(_IN_NAMES, _IN_SIZES):
    _IN_OFF[_n] = (_o, _s)
    _o += _s

_P16_COLS = (("qa", A_WIDTH), ("ka", A_KV_WIDTH), ("va", A_KV_WIDTH),
             ("qc", C_WIDTH), ("kc", C_WIDTH), ("vc", C_WIDTH))
_PF_COLS = (("za", A_WIDTH), ("cq", B_Q_LORA), ("ckv", B_KV_LORA), ("kr_a", LANES), ("kr_b", LANES),
            ("zb", B_WIDTH), ("zc", C_WIDTH))


def _offsets(cols):
    out, o = {}, 0
    for n, s in cols:
        out[n] = o
        o += s
    return out, o


_P16_OFF, P16_WIDTH = _offsets(_P16_COLS)
_PF_OFF, PF_WIDTH = _offsets(_PF_COLS)
GATE_OFF = P16_WIDTH + PF_WIDTH


def _cparams(*sem):
    return pltpu.CompilerParams(dimension_semantics=sem, vmem_limit_bytes=VMEM_LIMIT_BYTES)


def _dot(a, b):
    return jnp.dot(a, b, preferred_element_type=jnp.float32)


def _dot_nt(a, b):
    return lax.dot_general(a, b, (((1,), (1,)), ((), ())), preferred_element_type=jnp.float32)


def _silu(z):
    return z * jax.nn.sigmoid(z)


def _neg_abs(x):
    bits = lax.bitcast_convert_type(x, jnp.uint32) | jnp.uint32(0x80000000)
    return lax.bitcast_convert_type(bits, jnp.float32)


NORM_ROWS = 16


def _rmsnorm_kernel(x_ref, g_ref, o_ref):
    def body(r, carry):
        rows = pl.ds(pl.multiple_of(r * NORM_ROWS, NORM_ROWS), NORM_ROWS)
        x = x_ref[rows, :]
        y = x * lax.rsqrt(jnp.mean(x * x, axis=-1, keepdims=True) + EPS)
        o_ref[rows, :] = (y * g_ref[...]).astype(o_ref.dtype)
        return carry

    lax.fori_loop(0, x_ref.shape[0] // NORM_ROWS, body, 0, unroll=8)


def _rmsnorm(x, g, out_dtype, tm=512):
    s, d = x.shape
    return pl.pallas_call(
        _rmsnorm_kernel,
        grid=(s // tm,),
        in_specs=[pl.BlockSpec((tm, d), lambda i: (i, 0)),
                  pl.BlockSpec((1, d), lambda i: (0, 0))],
        out_specs=pl.BlockSpec((tm, d), lambda i: (i, 0)),
        out_shape=jax.ShapeDtypeStruct((s, d), out_dtype),
        compiler_params=_cparams("parallel"),
        name="rmsnorm",
    )(x, g.reshape(1, d))


def _matmul_kernel(a_ref, w_ref, o_ref):
    o_ref[...] = _dot(a_ref[...], w_ref[...]).astype(o_ref.dtype)


def _matmul_nt_kernel(a_ref, wt_ref, o_ref):
    o_ref[...] = _dot_nt(a_ref[...], wt_ref[...]).astype(o_ref.dtype)


def _matmul_residual_kernel(a_ref, w_ref, r_ref, o_ref):
    o_ref[...] = r_ref[...] + _dot(a_ref[...], w_ref[...])


def _matmul(a, w, layer, col_off, n_cols, out_dtype, tm, tn, residual=None, transposed_w=False,
            name="matmul"):
    s, k = a.shape
    assert s % tm == 0 and n_cols % tn == 0 and col_off % tn == 0
    assert not (transposed_w and residual is not None)
    jo = col_off // tn
    if transposed_w:
        w_spec = pl.BlockSpec((None, tn, k), lambda i, j: (layer, j + jo, 0))
    else:
        w_spec = pl.BlockSpec((None, k, tn), lambda i, j: (layer, 0, j + jo))
    in_specs = [pl.BlockSpec((tm, k), lambda i, j: (i, 0)), w_spec]
    args = [a, w]
    body = _matmul_nt_kernel if transposed_w else _matmul_kernel
    if residual is not None:
        in_specs.append(pl.BlockSpec((tm, tn), lambda i, j: (i, j)))
        args.append(residual)
        body = _matmul_residual_kernel
    return pl.pallas_call(
        body,
        grid=(s // tm, n_cols // tn),
        in_specs=in_specs,
        out_specs=pl.BlockSpec((tm, tn), lambda i, j: (i, j)),
        out_shape=jax.ShapeDtypeStruct((s, n_cols), out_dtype),
        compiler_params=_cparams("parallel", "arbitrary"),
        name=name,
    )(*args)


def _merge_kernel(h_ref, wga_ref, wgb_ref, wgc_ref, ya_ref, yb_ref, yc_ref,
                  wa_ref, wb_ref, wc_ref, o_ref):
    h = h_ref[...]

    def branch(wg_ref, y_ref, w_ref):
        return jax.nn.sigmoid(_dot_nt(h, wg_ref[...])) * _dot(y_ref[...], w_ref[...])

    merged = (branch(wga_ref, ya_ref, wa_ref) + branch(wgb_ref, yb_ref, wb_ref)
              + branch(wgc_ref, yc_ref, wc_ref))
    o_ref[...] = merged.astype(o_ref.dtype)


def _merge(h, w_all, layer, ya, yb, yc, wa, wb, wc, tm=1024, tn=256):
    s = h.shape[0]
    nj = D_MODEL // tn
    g0 = GATE_OFF // tn

    def gate_spec(k):
        return pl.BlockSpec((None, tn, D_MODEL), lambda i, j: (layer, g0 + k * nj + j, 0))

    def row_spec(width):
        return pl.BlockSpec((tm, width), lambda i, j: (i, 0))

    def w_spec(width):
        return pl.BlockSpec((None, width, tn), lambda i, j: (layer, 0, j))

    return pl.pallas_call(
        _merge_kernel,
        grid=(s // tm, nj),
        in_specs=[row_spec(D_MODEL), gate_spec(0), gate_spec(1), gate_spec(2),
                  row_spec(A_WIDTH), row_spec(B_WIDTH), row_spec(C_WIDTH),
                  w_spec(A_WIDTH), w_spec(B_WIDTH), w_spec(C_WIDTH)],
        out_specs=pl.BlockSpec((tm, tn), lambda i, j: (i, j)),
        out_shape=jax.ShapeDtypeStruct((s, D_MODEL), MXU_DTYPE),
        compiler_params=_cparams("parallel", "arbitrary"),
        name="gated_merge",
    )(h, w_all, w_all, w_all, ya, yb, yc, wa, wb, wc)


def _t5_bucket(rel):
    n = jnp.maximum(rel, 0)
    max_exact = NUM_BUCKETS // 2
    logn = jnp.log(jnp.maximum(n, 1).astype(jnp.float32) / max_exact)
    large = max_exact + (logn / math.log(MAX_DISTANCE / max_exact)
                         * (NUM_BUCKETS - max_exact)).astype(jnp.int32)
    large = jnp.minimum(large, NUM_BUCKETS - 1)
    return jnp.where(n < max_exact, n, large)


def _swa_bias_kernel(rel_bias_ref, bucket_ref, o_ref):
    bucket = bucket_ref[...]
    for h in range(A_HEADS):
        acc = jnp.zeros(bucket.shape, jnp.float32)
        for b in range(NUM_BUCKETS):
            acc = jnp.where(bucket == b, rel_bias_ref[b, h], acc)
        o_ref[h] = acc


def _swa_bias(rel_bias):
    t = jnp.arange(BLOCK)[:, None]
    s = jnp.arange(2 * BLOCK)[None, :]
    bucket = _t5_bucket(BLOCK + t - s).astype(jnp.int32)
    return pl.pallas_call(
        _swa_bias_kernel,
        in_specs=[pl.BlockSpec(memory_space=pltpu.SMEM),
                  pl.BlockSpec((BLOCK, 2 * BLOCK), lambda: (0, 0))],
        out_specs=pl.BlockSpec((A_HEADS, BLOCK, 2 * BLOCK), lambda: (0, 0, 0)),
        out_shape=jax.ShapeDtypeStruct((A_HEADS, BLOCK, 2 * BLOCK), jnp.float32),
        name="swa_bias_table",
    )(rel_bias, bucket)


def _swa_kernel(sinks_ref, q_ref, kc_ref, kp_ref, vc_ref, vp_ref, bias_ref, z_ref, o_ref):
    n = pl.program_id(0)
    rows = A_GROUP * BLOCK
    t = lax.broadcasted_iota(jnp.int32, (rows, 2 * BLOCK), 0) & (BLOCK - 1)
    s = lax.broadcasted_iota(jnp.int32, (rows, 2 * BLOCK), 1)
    ok = (s > jnp.maximum(t, jnp.where(n > 0, -1, BLOCK - 1))) & (s <= t + BLOCK)
    scale = HEAD_DIM ** -0.5

    def head_cols(h):
        return slice(h * HEAD_DIM, (h + 1) * HEAD_DIM)

    scores = []
    for hk in range(A_KV_HEADS):
        q = jnp.concatenate([q_ref[:, head_cols(hk * A_GROUP + g)] for g in range(A_GROUP)], axis=0)
        kk = jnp.concatenate([kp_ref[:, head_cols(hk)], kc_ref[:, head_cols(hk)]], axis=0)
        scores.append(_dot_nt(q, kk))
    probs = []
    for hk in range(A_KV_HEADS):
        sc = jnp.where(ok, scores[hk] * scale + bias_ref[hk], -jnp.inf)
        sink = jnp.concatenate([jnp.full((BLOCK, LANES), sinks_ref[hk * A_GROUP + g], jnp.float32)
                                for g in range(A_GROUP)], axis=0)
        m = jnp.maximum(jnp.max(sc, axis=-1, keepdims=True), sink)
        e = jnp.exp(sc - jnp.concatenate([m, m], axis=-1))
        inv = 1.0 / (jnp.sum(e, axis=-1, keepdims=True) + jnp.exp(sink - m))
        probs.append((e * jnp.concatenate([inv, inv], axis=-1)).astype(MXU_DTYPE))
    for hk in range(A_KV_HEADS):
        vv = jnp.concatenate([vp_ref[:, head_cols(hk)], vc_ref[:, head_cols(hk)]], axis=0)
        out = _dot(probs[hk], vv)
        for g in range(A_GROUP):
            cols = head_cols(hk * A_GROUP + g)
            o_ref[:, cols] = (out[g * BLOCK:(g + 1) * BLOCK] * _silu(z_ref[:, cols])).astype(o_ref.dtype)


def _swa(p16, pf, sinks, bias):
    s = p16.shape[0]
    qb = _P16_OFF["qa"] // A_WIDTH
    kb = _P16_OFF["ka"] // A_KV_WIDTH
    vb = _P16_OFF["va"] // A_KV_WIDTH
    zb = _PF_OFF["za"] // A_WIDTH
    prev = lambda n: jnp.maximum(n - 1, 0)
    return pl.pallas_call(
        _swa_kernel,
        grid=(s // BLOCK,),
        in_specs=[pl.BlockSpec(memory_space=pltpu.SMEM),
                  pl.BlockSpec((BLOCK, A_WIDTH), lambda n: (n, qb)),
                  pl.BlockSpec((BLOCK, A_KV_WIDTH), lambda n: (n, kb)),
                  pl.BlockSpec((BLOCK, A_KV_WIDTH), lambda n: (prev(n), kb)),
                  pl.BlockSpec((BLOCK, A_KV_WIDTH), lambda n: (n, vb)),
                  pl.BlockSpec((BLOCK, A_KV_WIDTH), lambda n: (prev(n), vb)),
                  pl.BlockSpec((A_KV_HEADS, A_GROUP * BLOCK, 2 * BLOCK), lambda n: (0, 0, 0)),
                  pl.BlockSpec((BLOCK, A_WIDTH), lambda n: (n, zb))],
        out_specs=pl.BlockSpec((BLOCK, A_WIDTH), lambda n: (n, 0)),
        out_shape=jax.ShapeDtypeStruct((s, A_WIDTH), MXU_DTYPE),
        compiler_params=_cparams("parallel"),
        name="swa_attention",
    )(sinks, p16, p16, p16, p16, p16, bias.reshape(A_KV_HEADS, A_GROUP * BLOCK, 2 * BLOCK), pf)


def _rms_cast(x, g):
    y = x * lax.rsqrt(jnp.mean(x * x, axis=-1, keepdims=True) + EPS)
    return (y * g).astype(MXU_DTYPE)


def _mla_pre_kernel(cq_ref, ckv_ref, kr_ref, gq_ref, gkv_ref, wq_ref, wkv_ref, rope_ref,
                    qf_ref, kf_ref, v_ref):
    tc = rope_ref[:, 0:LANES]
    ts = rope_ref[:, LANES:2 * LANES]
    qraw = _dot(_rms_cast(cq_ref[...], gq_ref[...]), wq_ref[...])
    for h in range(B_HEADS):
        nope, rope, swapped = (qraw[:, (3 * h + t) * LANES:(3 * h + t + 1) * LANES] for t in range(3))
        qf_ref[:, h * B_QK_PAD:h * B_QK_PAD + B_NOPE] = nope.astype(qf_ref.dtype)
        qf_ref[:, h * B_QK_PAD + B_NOPE:(h + 1) * B_QK_PAD] = (rope * tc + swapped * ts).astype(qf_ref.dtype)
    kvraw = _dot(_rms_cast(ckv_ref[...], gkv_ref[...]), wkv_ref[...])
    kr = kr_ref[...]
    krot = (kr[:, 0:LANES] * tc + kr[:, LANES:2 * LANES] * ts).astype(kf_ref.dtype)
    for h in range(B_HEADS):
        kf_ref[:, h * B_QK_PAD:h * B_QK_PAD + B_NOPE] = (
            kvraw[:, h * B_NOPE:(h + 1) * B_NOPE].astype(kf_ref.dtype))
        kf_ref[:, h * B_QK_PAD + B_NOPE:(h + 1) * B_QK_PAD] = krot
    v_ref[...] = kvraw[:, B_HEADS * B_NOPE:].astype(v_ref.dtype)


def _mla_pre(pf, gq, gkv, wq, wkv, rope, tm=512):
    s = pf.shape[0]
    row = lambda width, off: pl.BlockSpec((tm, width), lambda i: (i, off // width))
    full = lambda a: pl.BlockSpec(a.shape, lambda i: (0, 0))
    gq = gq.reshape(1, B_Q_LORA)
    gkv = gkv.reshape(1, B_KV_LORA)
    out_row = lambda width: pl.BlockSpec((tm, width), lambda i: (i, 0))
    return pl.pallas_call(
        _mla_pre_kernel,
        grid=(s // tm,),
        in_specs=[row(B_Q_LORA, _PF_OFF["cq"]), row(B_KV_LORA, _PF_OFF["ckv"]),
                  row(2 * LANES, _PF_OFF["kr_a"]), full(gq), full(gkv), full(wq), full(wkv),
                  pl.BlockSpec((tm, 2 * LANES), lambda i: (i, 0))],
        out_specs=[out_row(B_HEADS * B_QK_PAD), out_row(B_HEADS * B_QK_PAD), out_row(B_WIDTH)],
        out_shape=[jax.ShapeDtypeStruct((s, B_HEADS * B_QK_PAD), MXU_DTYPE),
                   jax.ShapeDtypeStruct((s, B_HEADS * B_QK_PAD), MXU_DTYPE),
                   jax.ShapeDtypeStruct((s, B_WIDTH), MXU_DTYPE)],
        compiler_params=_cparams("parallel"),
        name="mla_pre",
    )(pf, pf, pf, gq, gkv, wq, wkv, rope)


def _run_pipeline(stages, n, n_static, unroll, aligned=False):
    depth = len(stages)
    t0 = n_static + 1
    assert unroll % 2 == 0 and depth >= 2 and t0 >= depth - 1

    def run(k, pos, slot, static_pos=None):
        stages[k](pos, slot, static_pos)

    def prologue(length, guarded):
        for tt in range(length):
            for k in reversed(range(depth)):
                pos = tt - k
                if pos < 0:
                    continue
                if pos < n_static:
                    run(k, pos, pos % 2, pos)
                elif guarded:
                    pl.when(n > pos)(functools.partial(run, k, pos, pos % 2))
                else:
                    run(k, pos, pos % 2)

    def steady(tt, slot):
        for k in reversed(range(depth)):
            run(k, tt - k, (slot + k) % 2)

    def main_loop(first):
        def body(u, carry):
            for j in range(unroll):
                steady(first + unroll * u + j, (first + j) % 2)
            return carry

        trips = jnp.maximum(n - first, 0) // unroll
        lax.fori_loop(0, trips, body, 0)
        return first + unroll * trips

    def drain(first):
        for e in range(0 if first else 1, depth - 1):
            for k in reversed(range(e + 1, depth)):
                run(k, n + e - k, (n + e - k) % 2)

    if not aligned:
        prologue(t0, guarded=True)
        start = main_loop(t0)
        left = jnp.maximum(n - start, 0)
        width = unroll // 2
        while width >= 1:

            def block(start=start, width=width):
                for j in range(width):
                    steady(start + j, (t0 + j) % 2)

            pl.when((left & width) != 0)(block)
            start = start + (left & width)
            width //= 2
        pl.when(n >= t0)(functools.partial(drain, True))
        pl.when(n < t0)(functools.partial(drain, False))
    else:
        step = math.gcd(n_static, unroll)
        aligned_start = [p for p in range(t0, t0 + unroll) if p % step == 0]
        for m in range(2 * n_static, t0 + 2 * unroll, n_static):
            assert sum(1 for p in aligned_start if m >= p and (m - p) % unroll == 0) == 1, m

        def short():
            prologue(t0, guarded=True)
            drain(False)

        def long(p):
            prologue(p, guarded=False)
            main_loop(p)
            drain(True)

        pl.when(n == n_static)(short)
        for p in aligned_start:
            pl.when((n >= p) & ((n - p) % unroll == 0))(functools.partial(long, p))


MLA_UNROLL = 4


def _mla_kernel(q_ref, k_ref, v_ref, z_ref, o_ref,
                y_scr, p_scr, alpha_scr, m_scr, l_scr, acc_scr, *, tq, tk):
    n_diag = tq // tk
    i = pl.program_id(1)
    q = q_ref[...]
    c = (B_QK ** -0.5) * LOG2E

    def key_rows(pos):
        kb = jnp.where(pos < n_diag, n_diag * i + pos, pos - n_diag)
        return pl.ds(pl.multiple_of(kb * tk, tk), tk)

    def scores(pos, slot, static_pos):
        y_scr[slot] = _dot_nt(q, k_ref[key_rows(pos), :]) * c

    def softmax(pos, slot, static_pos):
        y = y_scr[slot]
        if static_pos is not None:
            row = lax.broadcasted_iota(jnp.int32, (tq, tk), 0)
            col = lax.broadcasted_iota(jnp.int32, (tq, tk), 1) + static_pos * tk
            y = jnp.where(col <= row, y, -jnp.inf)
        m = m_scr[...]
        m_new = jnp.maximum(m, jnp.max(y, axis=-1, keepdims=True))
        alpha = jnp.exp2(m - m_new)
        p = jnp.exp2(y - jnp.concatenate([m_new] * (tk // LANES), axis=-1))
        l_scr[...] = alpha * l_scr[...] + jnp.sum(p, axis=-1, keepdims=True)
        m_scr[...] = m_new
        alpha_scr[slot] = alpha
        p_scr[slot] = p.astype(MXU_DTYPE)

    def accumulate(pos, slot, static_pos):
        acc_scr[...] = alpha_scr[slot] * acc_scr[...] + _dot(p_scr[slot], v_ref[key_rows(pos), :])

    m_scr[...] = jnp.full((tq, LANES), -jnp.inf, jnp.float32)
    l_scr[...] = jnp.zeros((tq, LANES), jnp.float32)
    acc_scr[...] = jnp.zeros((tq, B_V), jnp.float32)
    _run_pipeline([scores, softmax, accumulate], n_diag * (i + 1), n_diag, MLA_UNROLL, aligned=True)
    o_ref[...] = ((acc_scr[...] / l_scr[...]) * _silu(z_ref[...])).astype(o_ref.dtype)


def _mla(qf, kf, v, pf, tq=1024, tk=512):
    s = qf.shape[0]
    zb = _PF_OFF["zb"] // B_V
    return pl.pallas_call(
        functools.partial(_mla_kernel, tq=tq, tk=tk),
        grid=(B_HEADS, s // tq),
        in_specs=[pl.BlockSpec((tq, B_QK_PAD), lambda h, i: (i, h)),
                  pl.BlockSpec((s, B_QK_PAD), lambda h, i: (0, h)),
                  pl.BlockSpec((s, B_V), lambda h, i: (0, h)),
                  pl.BlockSpec((tq, B_V), lambda h, i: (i, zb + h))],
        out_specs=pl.BlockSpec((tq, B_V), lambda h, i: (i, h)),
        out_shape=jax.ShapeDtypeStruct((s, B_WIDTH), MXU_DTYPE),
        scratch_shapes=[pltpu.VMEM((2, tq, tk), jnp.float32),
                        pltpu.VMEM((2, tq, tk), MXU_DTYPE),
                        pltpu.VMEM((2, tq, LANES), jnp.float32),
                        pltpu.VMEM((tq, LANES), jnp.float32),
                        pltpu.VMEM((tq, LANES), jnp.float32),
                        pltpu.VMEM((tq, B_V), jnp.float32)],
        compiler_params=_cparams("parallel", "arbitrary"),
        name="mla_attention",
    )(qf, kf, v, pf)


SB_UNROLL = 4


def _sb_kernel(q_ref, k_ref, v_ref, z_ref, u_ref, o_ref,
               s_scr, drop_scr, lb_scr, off_scr, off_run_scr, w_scr, acc_scr, *, tq, tk):
    n_diag = tq // tk
    i = pl.program_id(1)
    q = q_ref[...]
    scale = HEAD_DIM ** -0.5
    n = n_diag * (i + 1)

    def key_rows(pos):
        return pl.ds(pl.multiple_of((n - 1 - pos) * tk, tk), tk)

    def score(pos, slot, static_pos):
        s_scr[slot] = _dot_nt(q, k_ref[key_rows(pos), :])

    def park(pos, slot, static_pos):
        y = s_scr[slot] * (scale * LOG2E)
        l1p = jnp.log2(1.0 + jnp.exp2(_neg_abs(y)))
        drop = jnp.maximum(y, 0.0) + l1p
        log_beta = y - drop
        if static_pos is not None:
            row = lax.broadcasted_iota(jnp.int32, (tq, tk), 0)
            col = lax.broadcasted_iota(jnp.int32, (tq, tk), 1) + (n_diag - 1 - static_pos) * tk
            past = col < row
            drop = jnp.where(past, drop, 0.0)
            log_beta = jnp.where(past, log_beta, -jnp.inf)
        drop_scr[slot] = drop.astype(MXU_DTYPE)
        lb_scr[slot] = log_beta
        off = off_run_scr[...]
        off_scr[slot] = off
        off_run_scr[...] = off + jnp.sum(drop, axis=-1, keepdims=True)

    def weigh(pos, slot, static_pos):
        within = _dot(drop_scr[slot], u_ref[...])
        off = jnp.concatenate([off_scr[slot]] * (tk // LANES), axis=-1)
        w_scr[slot] = jnp.exp2(lb_scr[slot] - (within + off)).astype(MXU_DTYPE)

    def accumulate(pos, slot, static_pos):
        acc_scr[...] += _dot(w_scr[slot], v_ref[key_rows(pos), :])

    off_run_scr[...] = jnp.zeros((tq, LANES), jnp.float32)
    acc_scr[...] = jnp.zeros((tq, HEAD_DIM), jnp.float32)
    _run_pipeline([score, park, weigh, accumulate], n, n_diag, SB_UNROLL, aligned=True)
    o_ref[...] = (acc_scr[...] * _silu(z_ref[...])).astype(o_ref.dtype)


def _sb(p16, pf, tq=1024, tk=256):
    s = p16.shape[0]
    qb = _P16_OFF["qc"] // HEAD_DIM
    kb = _P16_OFF["kc"] // HEAD_DIM
    vb = _P16_OFF["vc"] // HEAD_DIM
    zb = _PF_OFF["zc"] // HEAD_DIM
    j = lax.broadcasted_iota(jnp.int32, (tk, tk), 0)
    c = lax.broadcasted_iota(jnp.int32, (tk, tk), 1)
    u = (j > c).astype(MXU_DTYPE)
    return pl.pallas_call(
        functools.partial(_sb_kernel, tq=tq, tk=tk),
        grid=(C_HEADS, s // tq),
        in_specs=[pl.BlockSpec((tq, HEAD_DIM), lambda h, i: (i, qb + h)),
                  pl.BlockSpec((s, HEAD_DIM), lambda h, i: (0, kb + h)),
                  pl.BlockSpec((s, HEAD_DIM), lambda h, i: (0, vb + h)),
                  pl.BlockSpec((tq, HEAD_DIM), lambda h, i: (i, zb + h)),
                  pl.BlockSpec((tk, tk), lambda h, i: (0, 0))],
        out_specs=pl.BlockSpec((tq, HEAD_DIM), lambda h, i: (i, h)),
        out_shape=jax.ShapeDtypeStruct((s, C_WIDTH), MXU_DTYPE),
        scratch_shapes=[pltpu.VMEM((2, tq, tk), jnp.float32),
                        pltpu.VMEM((2, tq, tk), MXU_DTYPE),
                        pltpu.VMEM((2, tq, tk), jnp.float32),
                        pltpu.VMEM((2, tq, LANES), jnp.float32),
                        pltpu.VMEM((tq, LANES), jnp.float32),
                        pltpu.VMEM((2, tq, tk), MXU_DTYPE),
                        pltpu.VMEM((tq, HEAD_DIM), jnp.float32)],
        compiler_params=_cparams("parallel", "arbitrary"),
        name="stick_breaking_attention",
    )(p16, p16, p16, pf, u)


def _swap_halves(w):
    half = w.shape[-1] // 2
    return jnp.concatenate([w[..., half:], w[..., :half]], axis=-1)


def _pad_cols(w, width):
    return jnp.pad(w, ((0, 0), (0, width - w.shape[-1])))


_GATE_COLS = (("ga", D_MODEL), ("gb", D_MODEL), ("gc", D_MODEL))
W_ALL_WIDTH = GATE_OFF + 3 * D_MODEL


W_IN_ROWS = 256


def _w_in_layout_table():
    src, is_kr = [], []
    for name, width in _P16_COLS + _PF_COLS + _GATE_COLS:
        if name == "kr_a":
            src.append(_IN_OFF["kr"][0])
            is_kr.append(1)
        elif name != "kr_b":
            assert width % W_IN_ROWS == 0
            src += [_IN_OFF[name][0] + t * W_IN_ROWS for t in range(width // W_IN_ROWS)]
            is_kr += [0] * (width // W_IN_ROWS)
    return jnp.asarray(src, jnp.int32), jnp.asarray(is_kr, jnp.int32)


def _w_in_layout_kernel(src_ref, is_kr_ref, w_ref, o_ref):
    x = w_ref[0]
    half = B_ROPE // 2
    kr = x[:B_ROPE]
    pad = jnp.zeros((LANES - B_ROPE, x.shape[1]), x.dtype)
    kr_rows = jnp.concatenate([kr, pad, kr[half:], kr[:half], pad], axis=0)
    x = jnp.where(is_kr_ref[pl.program_id(1)] == 1, kr_rows, x)
    o_ref[...] = x.astype(o_ref.dtype)


def _layout_w_in(w_in_t):
    depth, _, d = w_in_t.shape
    src, is_kr = _w_in_layout_table()
    return pl.pallas_call(
        _w_in_layout_kernel,
        grid_spec=pltpu.PrefetchScalarGridSpec(
            num_scalar_prefetch=2,
            grid=(depth, W_ALL_WIDTH // W_IN_ROWS),
            in_specs=[pl.BlockSpec((pl.Element(1), pl.Element(W_IN_ROWS), pl.Element(d)),
                                   lambda l, r, src, is_kr: (l, pl.multiple_of(src[r], B_ROPE), 0))],
            out_specs=pl.BlockSpec((None, W_IN_ROWS, d), lambda l, r, src, is_kr: (l, r, 0))),
        out_shape=jax.ShapeDtypeStruct((depth, W_ALL_WIDTH, d), MXU_DTYPE),
        compiler_params=_cparams("parallel", "arbitrary"),
        name="w_in_layout",
    )(src, is_kr, w_in_t)


def _cast_kernel(w_ref, o_ref):
    o_ref[...] = w_ref[...].astype(o_ref.dtype)


def _cast_weights(w, tr=512):
    depth, rows, cols = w.shape
    tr = min(tr, rows)
    spec = pl.BlockSpec((1, tr, cols), lambda l, r: (l, r, 0))
    return pl.pallas_call(
        _cast_kernel,
        grid=(depth, rows // tr),
        in_specs=[spec],
        out_specs=spec,
        out_shape=jax.ShapeDtypeStruct(w.shape, MXU_DTYPE),
        compiler_params=_cparams("parallel", "parallel"),
        name="cast_weights",
    )(w)


def _layout_w_q_up(w):
    parts = []
    for h in range(B_HEADS):
        nope = w[:, h * B_QK:h * B_QK + B_NOPE]
        rope = w[:, h * B_QK + B_NOPE:(h + 1) * B_QK]
        parts += [nope, _pad_cols(rope, LANES), _pad_cols(_swap_halves(rope), LANES)]
    return jnp.concatenate(parts, axis=1).astype(MXU_DTYPE)


def _layout_w_kv_up(w):
    per = B_NOPE + B_V
    ks = [w[:, h * per:h * per + B_NOPE] for h in range(B_HEADS)]
    vs = [w[:, h * per + B_NOPE:(h + 1) * per] for h in range(B_HEADS)]
    return jnp.concatenate(ks + vs, axis=1).astype(MXU_DTYPE)


def _rope_table(seq):
    pos = jnp.arange(seq, dtype=jnp.float32)
    inv = ROPE_THETA ** (-jnp.arange(0, B_ROPE, 2, dtype=jnp.float32) / B_ROPE)
    ang = pos[:, None] * inv[None, :]
    cos, sin = jnp.cos(ang), jnp.sin(ang)
    zeros = jnp.zeros((seq, LANES - B_ROPE), jnp.float32)
    return jnp.concatenate([cos, cos, zeros, -sin, sin, zeros], axis=1)


def kernel(x, norm_g, w_in, attn_sinks, rel_bias, g_q_lora, w_q_up, g_kv_lora, w_kv_up,
           w_proj_a, w_proj_b, w_proj_c, w_out, final_g):
    batch, seq, d = x.shape
    assert batch == 1 and d == D_MODEL and seq % 1024 == 0
    depth = w_in.shape[0]
    xs = x.reshape(seq, d)
    bias = _swa_bias(rel_bias)
    rope = _rope_table(seq)
    w_all = _layout_w_in(jnp.swapaxes(w_in, 1, 2))
    wa, wb, wc, wo = (_cast_weights(w) for w in (w_proj_a, w_proj_b, w_proj_c, w_out))

    for l in range(depth):
        h = _rmsnorm(xs, norm_g[l], MXU_DTYPE)
        p16 = _matmul(h, w_all, l, 0, P16_WIDTH, MXU_DTYPE, tm=1024, tn=768, transposed_w=True,
                      name="in_proj_bf16")
        pf = _matmul(h, w_all, l, P16_WIDTH, PF_WIDTH, jnp.float32, tm=1024, tn=768,
                     transposed_w=True, name="in_proj_f32")

        ya = _swa(p16, pf, attn_sinks[l], bias)
        qf, kf, vb = _mla_pre(pf, g_q_lora[l], g_kv_lora[l], _layout_w_q_up(w_q_up[l]),
                              _layout_w_kv_up(w_kv_up[l]), rope)
        yb = _mla(qf, kf, vb, pf)
        yc = _sb(p16, pf)

        merged = _merge(h, w_all, l, ya, yb, yc, wa, wb, wc)
        xs = _matmul(merged, wo, l, 0, D_MODEL, jnp.float32, tm=1024, tn=512, residual=xs,
                     name="out_proj")

    out = _rmsnorm(xs, final_g, jnp.float32)
    return out.reshape(batch, seq, d)
```

```python
import functools
import math

import jax
import jax.numpy as jnp
from jax import lax
from jax.experimental import pallas as pl
from jax.experimental.pallas import tpu as pltpu

D_MODEL = 4096
HEAD_DIM = 128
LANES = 128
BLOCK = 128
EPS = 1e-6
A_HEADS = 8
A_KV_HEADS = 2
A_GROUP = A_HEADS // A_KV_HEADS
A_WIDTH = A_HEADS * HEAD_DIM
A_KV_WIDTH = A_KV_HEADS * HEAD_DIM
NUM_BUCKETS = 32
MAX_DISTANCE = 128
B_HEADS = 4
B_Q_LORA = 1024
B_KV_LORA = 512
B_NOPE = 128
B_ROPE = 64
B_V = 128
B_QK = B_NOPE + B_ROPE
B_QK_PAD = 256
B_WIDTH = B_HEADS * B_V
ROPE_THETA = 10000.0
C_HEADS = 4
C_WIDTH = C_HEADS * HEAD_DIM

MXU_DTYPE = jnp.bfloat16
LOG2E = math.log2(math.e)
VMEM_LIMIT_BYTES = 56 * 1024 * 1024

_IN_SIZES = (A_WIDTH, A_KV_WIDTH, A_KV_WIDTH, A_WIDTH, B_Q_LORA, B_KV_LORA, B_ROPE, B_WIDTH,
             C_WIDTH, C_WIDTH, C_WIDTH, C_WIDTH, D_MODEL, D_MODEL, D_MODEL)
_IN_NAMES = ("qa", "ka", "va", "za", "cq", "ckv", "kr", "zb", "qc", "kc", "vc", "zc", "ga", "gb", "gc")
_IN_OFF = {}
_o = 0
for _n, _s in zip(_IN_NAMES, _IN_SIZES):
    _IN_OFF[_n] = (_o, _s)
    _o += _s

_P16_COLS = (("qa", A_WIDTH), ("ka", A_KV_WIDTH), ("va", A_KV_WIDTH),
             ("qc", C_WIDTH), ("kc", C_WIDTH), ("vc", C_WIDTH))
_PF_COLS = (("za", A_WIDTH), ("cq", B_Q_LORA), ("ckv", B_KV_LORA), ("kr_a", LANES), ("kr_b", LANES),
            ("zb", B_WIDTH), ("zc", C_WIDTH))


def _offsets(cols):
    out, o = {}, 0
    for n, s in cols:
        out[n] = o
        o += s
    return out, o


_P16_OFF, P16_WIDTH = _offsets(_P16_COLS)
_PF_OFF, PF_WIDTH = _offsets(_PF_COLS)
GATE_OFF = P16_WIDTH + PF_WIDTH


def _cparams(*sem):
    return pltpu.CompilerParams(dimension_semantics=sem, vmem_limit_bytes=VMEM_LIMIT_BYTES)


def _dot(a, b):
    return jnp.dot(a, b, preferred_element_type=jnp.float32)


def _dot_nt(a, b):
    return lax.dot_general(a, b, (((1,), (1,)), ((), ())), preferred_element_type=jnp.float32)


def _silu(z):
    return z * jax.nn.sigmoid(z)


def _neg_abs(x):
    bits = lax.bitcast_convert_type(x, jnp.uint32) | jnp.uint32(0x80000000)
    return lax.bitcast_convert_type(bits, jnp.float32)


NORM_ROWS = 16


def _rmsnorm_kernel(x_ref, g_ref, o_ref):
    def body(r, carry):
        rows = pl.ds(pl.multiple_of(r * NORM_ROWS, NORM_ROWS), NORM_ROWS)
        x = x_ref[rows, :]
        y = x * lax.rsqrt(jnp.mean(x * x, axis=-1, keepdims=True) + EPS)
        o_ref[rows, :] = (y * g_ref[...]).astype(o_ref.dtype)
        return carry

    lax.fori_loop(0, x_ref.shape[0] // NORM_ROWS, body, 0, unroll=8)


def _rmsnorm(x, g, out_dtype, tm=512):
    s, d = x.shape
    return pl.pallas_call(
        _rmsnorm_kernel,
        grid=(s // tm,),
        in_specs=[pl.BlockSpec((tm, d), lambda i: (i, 0)),
                  pl.BlockSpec((1, d), lambda i: (0, 0))],
        out_specs=pl.BlockSpec((tm, d), lambda i: (i, 0)),
        out_shape=jax.ShapeDtypeStruct((s, d), out_dtype),
        compiler_params=_cparams("parallel"),
        name="rmsnorm",
    )(x, g.reshape(1, d))


def _matmul_kernel(a_ref, w_ref, o_ref):
    o_ref[...] = _dot(a_ref[...], w_ref[...]).astype(o_ref.dtype)


def _matmul_nt_kernel(a_ref, wt_ref, o_ref):
    o_ref[...] = _dot_nt(a_ref[...], wt_ref[...]).astype(o_ref.dtype)


def _matmul_residual_kernel(a_ref, w_ref, r_ref, o_ref):
    o_ref[...] = r_ref[...] + _dot(a_ref[...], w_ref[...])


def _matmul(a, w, layer, col_off, n_cols, out_dtype, tm, tn, residual=None, transposed_w=False,
            name="matmul"):
    s, k = a.shape
    assert s % tm == 0 and n_cols % tn == 0 and col_off % tn == 0
    assert not (transposed_w and residual is not None)
    jo = col_off // tn
    if transposed_w:
        w_spec = pl.BlockSpec((None, tn, k), lambda i, j: (layer, j + jo, 0))
    else:
        w_spec = pl.BlockSpec((None, k, tn), lambda i, j: (layer, 0, j + jo))
    in_specs = [pl.BlockSpec((tm, k), lambda i, j: (i, 0)), w_spec]
    args = [a, w]
    body = _matmul_nt_kernel if transposed_w else _matmul_kernel
    if residual is not None:
        in_specs.append(pl.BlockSpec((tm, tn), lambda i, j: (i, j)))
        args.append(residual)
        body = _matmul_residual_kernel
    return pl.pallas_call(
        body,
        grid=(s // tm, n_cols // tn),
        in_specs=in_specs,
        out_specs=pl.BlockSpec((tm, tn), lambda i, j: (i, j)),
        out_shape=jax.ShapeDtypeStruct((s, n_cols), out_dtype),
        compiler_params=_cparams("parallel", "arbitrary"),
        name=name,
    )(*args)


def _merge_kernel(h_ref, wga_ref, wgb_ref, wgc_ref, ya_ref, yb_ref, yc_ref,
                  wa_ref, wb_ref, wc_ref, o_ref):
    h = h_ref[...]

    def branch(wg_ref, y_ref, w_ref):
        return jax.nn.sigmoid(_dot_nt(h, wg_ref[...])) * _dot(y_ref[...], w_ref[...])

    merged = (branch(wga_ref, ya_ref, wa_ref) + branch(wgb_ref, yb_ref, wb_ref)
              + branch(wgc_ref, yc_ref, wc_ref))
    o_ref[...] = merged.astype(o_ref.dtype)


def _merge(h, w_all, layer, ya, yb, yc, wa, wb, wc, tm=1024, tn=256):
    s = h.shape[0]
    nj = D_MODEL // tn
    g0 = GATE_OFF // tn

    def gate_spec(k):
        return pl.BlockSpec((None, tn, D_MODEL), lambda i, j: (layer, g0 + k * nj + j, 0))

    def row_spec(width):
        return pl.BlockSpec((tm, width), lambda i, j: (i, 0))

    def w_spec(width):
        return pl.BlockSpec((None, width, tn), lambda i, j: (layer, 0, j))

    return pl.pallas_call(
        _merge_kernel,
        grid=(s // tm, nj),
        in_specs=[row_spec(D_MODEL), gate_spec(0), gate_spec(1), gate_spec(2),
                  row_spec(A_WIDTH), row_spec(B_WIDTH), row_spec(C_WIDTH),
                  w_spec(A_WIDTH), w_spec(B_WIDTH), w_spec(C_WIDTH)],
        out_specs=pl.BlockSpec((tm, tn), lambda i, j: (i, j)),
        out_shape=jax.ShapeDtypeStruct((s, D_MODEL), MXU_DTYPE),
        compiler_params=_cparams("parallel", "arbitrary"),
        name="gated_merge",
    )(h, w_all, w_all, w_all, ya, yb, yc, wa, wb, wc)


def _t5_bucket(rel):
    n = jnp.maximum(rel, 0)
    max_exact = NUM_BUCKETS // 2
    logn = jnp.log(jnp.maximum(n, 1).astype(jnp.float32) / max_exact)
    large = max_exact + (logn / math.log(MAX_DISTANCE / max_exact)
                         * (NUM_BUCKETS - max_exact)).astype(jnp.int32)
    large = jnp.minimum(large, NUM_BUCKETS - 1)
    return jnp.where(n < max_exact, n, large)


def _swa_bias_kernel(rel_bias_ref, bucket_ref, o_ref):
    bucket = bucket_ref[...]
    for h in range(A_HEADS):
        acc = jnp.zeros(bucket.shape, jnp.float32)
        for b in range(NUM_BUCKETS):
            acc = jnp.where(bucket == b, rel_bias_ref[b, h], acc)
        o_ref[h] = acc


def _swa_bias(rel_bias):
    t = jnp.arange(BLOCK)[:, None]
    s = jnp.arange(2 * BLOCK)[None, :]
    bucket = _t5_bucket(BLOCK + t - s).astype(jnp.int32)
    return pl.pallas_call(
        _swa_bias_kernel,
        in_specs=[pl.BlockSpec(memory_space=pltpu.SMEM),
                  pl.BlockSpec((BLOCK, 2 * BLOCK), lambda: (0, 0))],
        out_specs=pl.BlockSpec((A_HEADS, BLOCK, 2 * BLOCK), lambda: (0, 0, 0)),
        out_shape=jax.ShapeDtypeStruct((A_HEADS, BLOCK, 2 * BLOCK), jnp.float32),
        name="swa_bias_table",
    )(rel_bias, bucket)


SWA_BLOCKS = 4


def _swa_kernel(sinks_ref, q_ref, kc_ref, kp_ref, vc_ref, vp_ref, bias_ref, z_ref, o_ref):
    n = pl.program_id(0)
    rows = A_GROUP * BLOCK
    t = lax.broadcasted_iota(jnp.int32, (rows, 2 * BLOCK), 0) & (BLOCK - 1)
    s = lax.broadcasted_iota(jnp.int32, (rows, 2 * BLOCK), 1)
    ok = (s > t) & (s <= t + BLOCK)
    ok_first = (s > jnp.maximum(t, jnp.where(n > 0, -1, BLOCK - 1))) & (s <= t + BLOCK)
    scale = HEAD_DIM ** -0.5

    def head_cols(h):
        return slice(h * HEAD_DIM, (h + 1) * HEAD_DIM)

    def block_rows(b):
        return slice(b * BLOCK, (b + 1) * BLOCK)

    def band(cur_ref, prev_ref, b, hk):
        prev = prev_ref[:, head_cols(hk)] if b == 0 else cur_ref[block_rows(b - 1), head_cols(hk)]
        return jnp.concatenate([prev, cur_ref[block_rows(b), head_cols(hk)]], axis=0)

    units = [(b, hk) for b in range(SWA_BLOCKS) for hk in range(A_KV_HEADS)]
    scores = []
    for b, hk in units:
        q = jnp.concatenate([q_ref[block_rows(b), head_cols(hk * A_GROUP + g)] for g in range(A_GROUP)],
                            axis=0)
        scores.append(_dot_nt(q, band(kc_ref, kp_ref, b, hk)))
    probs = []
    for (b, hk), sc in zip(units, scores):
        sc = jnp.where(ok_first if b == 0 else ok, sc * scale + bias_ref[hk], -jnp.inf)
        sink = jnp.concatenate([jnp.full((BLOCK, LANES), sinks_ref[hk * A_GROUP + g], jnp.float32)
                                for g in range(A_GROUP)], axis=0)
        m = jnp.maximum(jnp.max(sc, axis=-1, keepdims=True), sink)
        e = jnp.exp(sc - jnp.concatenate([m, m], axis=-1))
        inv = 1.0 / (jnp.sum(e, axis=-1, keepdims=True) + jnp.exp(sink - m))
        probs.append((e * jnp.concatenate([inv, inv], axis=-1)).astype(MXU_DTYPE))
    for (b, hk), p in zip(units, probs):
        out = _dot(p, band(vc_ref, vp_ref, b, hk))
        for g in range(A_GROUP):
            cols = head_cols(hk * A_GROUP + g)
            gate = _silu(z_ref[block_rows(b), cols])
            o_ref[block_rows(b), cols] = (out[g * BLOCK:(g + 1) * BLOCK] * gate).astype(o_ref.dtype)


def _swa(p16, pf, sinks, bias):
    s = p16.shape[0]
    qb = _P16_OFF["qa"] // A_WIDTH
    kb = _P16_OFF["ka"] // A_KV_WIDTH
    vb = _P16_OFF["va"] // A_KV_WIDTH
    zb = _PF_OFF["za"] // A_WIDTH
    tq = SWA_BLOCKS * BLOCK
    prev = lambda n: jnp.maximum(n * SWA_BLOCKS - 1, 0)
    return pl.pallas_call(
        _swa_kernel,
        grid=(s // tq,),
        in_specs=[pl.BlockSpec(memory_space=pltpu.SMEM),
                  pl.BlockSpec((tq, A_WIDTH), lambda n: (n, qb)),
                  pl.BlockSpec((tq, A_KV_WIDTH), lambda n: (n, kb)),
                  pl.BlockSpec((BLOCK, A_KV_WIDTH), lambda n: (prev(n), kb)),
                  pl.BlockSpec((tq, A_KV_WIDTH), lambda n: (n, vb)),
                  pl.BlockSpec((BLOCK, A_KV_WIDTH), lambda n: (prev(n), vb)),
                  pl.BlockSpec((A_KV_HEADS, A_GROUP * BLOCK, 2 * BLOCK), lambda n: (0, 0, 0)),
                  pl.BlockSpec((tq, A_WIDTH), lambda n: (n, zb))],
        out_specs=pl.BlockSpec((tq, A_WIDTH), lambda n: (n, 0)),
        out_shape=jax.ShapeDtypeStruct((s, A_WIDTH), MXU_DTYPE),
        compiler_params=_cparams("parallel"),
        name="swa_attention",
    )(sinks, p16, p16, p16, p16, p16, bias.reshape(A_KV_HEADS, A_GROUP * BLOCK, 2 * BLOCK), pf)


def _rms_cast(x, g):
    y = x * lax.rsqrt(jnp.mean(x * x, axis=-1, keepdims=True) + EPS)
    return (y * g).astype(MXU_DTYPE)


def _mla_pre_kernel(cq_ref, ckv_ref, kr_ref, gq_ref, gkv_ref, wq_ref, wkv_ref, rope_ref,
                    qf_ref, kf_ref, v_ref):
    tc = rope_ref[:, 0:LANES]
    ts = rope_ref[:, LANES:2 * LANES]
    qraw = _dot(_rms_cast(cq_ref[...], gq_ref[...]), wq_ref[...])
    for h in range(B_HEADS):
        nope, rope, swapped = (qraw[:, (3 * h + t) * LANES:(3 * h + t + 1) * LANES] for t in range(3))
        qf_ref[:, h * B_QK_PAD:h * B_QK_PAD + B_NOPE] = nope.astype(qf_ref.dtype)
        qf_ref[:, h * B_QK_PAD + B_NOPE:(h + 1) * B_QK_PAD] = (rope * tc + swapped * ts).astype(qf_ref.dtype)
    kvraw = _dot(_rms_cast(ckv_ref[...], gkv_ref[...]), wkv_ref[...])
    kr = kr_ref[...]
    krot = (kr[:, 0:LANES] * tc + kr[:, LANES:2 * LANES] * ts).astype(kf_ref.dtype)
    for h in range(B_HEADS):
        kf_ref[:, h * B_QK_PAD:h * B_QK_PAD + B_NOPE] = (
            kvraw[:, h * B_NOPE:(h + 1) * B_NOPE].astype(kf_ref.dtype))
        kf_ref[:, h * B_QK_PAD + B_NOPE:(h + 1) * B_QK_PAD] = krot
    v_ref[...] = kvraw[:, B_HEADS * B_NOPE:].astype(v_ref.dtype)


def _mla_pre(pf, gq, gkv, wq, wkv, rope, tm=512):
    s = pf.shape[0]
    row = lambda width, off: pl.BlockSpec((tm, width), lambda i: (i, off // width))
    full = lambda a: pl.BlockSpec(a.shape, lambda i: (0, 0))
    gq = gq.reshape(1, B_Q_LORA)
    gkv = gkv.reshape(1, B_KV_LORA)
    out_row = lambda width: pl.BlockSpec((tm, width), lambda i: (i, 0))
    return pl.pallas_call(
        _mla_pre_kernel,
        grid=(s // tm,),
        in_specs=[row(B_Q_LORA, _PF_OFF["cq"]), row(B_KV_LORA, _PF_OFF["ckv"]),
                  row(2 * LANES, _PF_OFF["kr_a"]), full(gq), full(gkv), full(wq), full(wkv),
                  pl.BlockSpec((tm, 2 * LANES), lambda i: (i, 0))],
        out_specs=[out_row(B_HEADS * B_QK_PAD), out_row(B_HEADS * B_QK_PAD), out_row(B_WIDTH)],
        out_shape=[jax.ShapeDtypeStruct((s, B_HEADS * B_QK_PAD), MXU_DTYPE),
                   jax.ShapeDtypeStruct((s, B_HEADS * B_QK_PAD), MXU_DTYPE),
                   jax.ShapeDtypeStruct((s, B_WIDTH), MXU_DTYPE)],
        compiler_params=_cparams("parallel"),
        name="mla_pre",
    )(pf, pf, pf, gq, gkv, wq, wkv, rope)


def _run_pipeline(stages, n, n_static, unroll, aligned=False):
    depth = len(stages)
    t0 = n_static + 1
    assert unroll % 2 == 0 and depth >= 2 and t0 >= depth - 1

    def run(k, pos, slot, static_pos=None):
        stages[k](pos, slot, static_pos)

    def prologue(length, guarded):
        for tt in range(length):
            for k in reversed(range(depth)):
                pos = tt - k
                if pos < 0:
                    continue
                if pos < n_static:
                    run(k, pos, pos % 2, pos)
                elif guarded:
                    pl.when(n > pos)(functools.partial(run, k, pos, pos % 2))
                else:
                    run(k, pos, pos % 2)

    def steady(tt, slot):
        for k in reversed(range(depth)):
            run(k, tt - k, (slot + k) % 2)

    def main_loop(first):
        def body(u, carry):
            for j in range(unroll):
                steady(first + unroll * u + j, (first + j) % 2)
            return carry

        trips = jnp.maximum(n - first, 0) // unroll
        lax.fori_loop(0, trips, body, 0)
        return first + unroll * trips

    def drain(first):
        for e in range(0 if first else 1, depth - 1):
            for k in reversed(range(e + 1, depth)):
                run(k, n + e - k, (n + e - k) % 2)

    if not aligned:
        prologue(t0, guarded=True)
        start = main_loop(t0)
        left = jnp.maximum(n - start, 0)
        width = unroll // 2
        while width >= 1:

            def block(start=start, width=width):
                for j in range(width):
                    steady(start + j, (t0 + j) % 2)

            pl.when((left & width) != 0)(block)
            start = start + (left & width)
            width //= 2
        pl.when(n >= t0)(functools.partial(drain, True))
        pl.when(n < t0)(functools.partial(drain, False))
    else:
        step = math.gcd(n_static, unroll)
        aligned_start = [p for p in range(t0, t0 + unroll) if p % step == 0]
        for m in range(2 * n_static, t0 + 2 * unroll, n_static):
            assert sum(1 for p in aligned_start if m >= p and (m - p) % unroll == 0) == 1, m

        def short():
            prologue(t0, guarded=True)
            drain(False)

        def long(p):
            prologue(p, guarded=False)
            main_loop(p)
            drain(True)

        pl.when(n == n_static)(short)
        for p in aligned_start:
            pl.when((n >= p) & ((n - p) % unroll == 0))(functools.partial(long, p))


MLA_UNROLL = 4


def _mla_kernel(q_ref, k_ref, v_ref, z_ref, o_ref,
                y_scr, p_scr, alpha_scr, m_scr, l_scr, acc_scr, *, tq, tk):
    n_diag = tq // tk
    i = pl.program_id(1)
    q = q_ref[...]
    c = (B_QK ** -0.5) * LOG2E

    def key_rows(pos):
        kb = jnp.where(pos < n_diag, n_diag * i + pos, pos - n_diag)
        return pl.ds(pl.multiple_of(kb * tk, tk), tk)

    def scores(pos, slot, static_pos):
        y_scr[slot] = _dot_nt(q, k_ref[key_rows(pos), :]) * c

    def softmax(pos, slot, static_pos):
        y = y_scr[slot]
        if static_pos is not None:
            row = lax.broadcasted_iota(jnp.int32, (tq, tk), 0)
            col = lax.broadcasted_iota(jnp.int32, (tq, tk), 1) + static_pos * tk
            y = jnp.where(col <= row, y, -jnp.inf)
        m = m_scr[...]
        m_new = jnp.maximum(m, jnp.max(y, axis=-1, keepdims=True))
        alpha = jnp.exp2(m - m_new)
        p = jnp.exp2(y - jnp.concatenate([m_new] * (tk // LANES), axis=-1))
        l_scr[...] = alpha * l_scr[...] + jnp.sum(p, axis=-1, keepdims=True)
        m_scr[...] = m_new
        alpha_scr[slot] = alpha
        p_scr[slot] = p.astype(MXU_DTYPE)

    def accumulate(pos, slot, static_pos):
        acc_scr[...] = alpha_scr[slot] * acc_scr[...] + _dot(p_scr[slot], v_ref[key_rows(pos), :])

    m_scr[...] = jnp.full((tq, LANES), -jnp.inf, jnp.float32)
    l_scr[...] = jnp.zeros((tq, LANES), jnp.float32)
    acc_scr[...] = jnp.zeros((tq, B_V), jnp.float32)
    _run_pipeline([scores, softmax, accumulate], n_diag * (i + 1), n_diag, MLA_UNROLL, aligned=True)
    o_ref[...] = ((acc_scr[...] / l_scr[...]) * _silu(z_ref[...])).astype(o_ref.dtype)


def _mla(qf, kf, v, pf, tq=1024, tk=512):
    s = qf.shape[0]
    zb = _PF_OFF["zb"] // B_V
    return pl.pallas_call(
        functools.partial(_mla_kernel, tq=tq, tk=tk),
        grid=(B_HEADS, s // tq),
        in_specs=[pl.BlockSpec((tq, B_QK_PAD), lambda h, i: (i, h)),
                  pl.BlockSpec((s, B_QK_PAD), lambda h, i: (0, h)),
                  pl.BlockSpec((s, B_V), lambda h, i: (0, h)),
                  pl.BlockSpec((tq, B_V), lambda h, i: (i, zb + h))],
        out_specs=pl.BlockSpec((tq, B_V), lambda h, i: (i, h)),
        out_shape=jax.ShapeDtypeStruct((s, B_WIDTH), MXU_DTYPE),
        scratch_shapes=[pltpu.VMEM((2, tq, tk), jnp.float32),
                        pltpu.VMEM((2, tq, tk), MXU_DTYPE),
                        pltpu.VMEM((2, tq, LANES), jnp.float32),
                        pltpu.VMEM((tq, LANES), jnp.float32),
                        pltpu.VMEM((tq, LANES), jnp.float32),
                        pltpu.VMEM((tq, B_V), jnp.float32)],
        compiler_params=_cparams("parallel", "arbitrary"),
        name="mla_attention",
    )(qf, kf, v, pf)


SB_UNROLL = 4


def _sb_kernel(q_ref, k_ref, v_ref, z_ref, u_ref, o_ref,
               s_scr, drop_scr, lb_scr, off_scr, off_run_scr, w_scr, acc_scr, *, tq, tk):
    n_diag = tq // tk
    i = pl.program_id(1)
    q = q_ref[...]
    scale = HEAD_DIM ** -0.5
    n = n_diag * (i + 1)

    def key_rows(pos):
        return pl.ds(pl.multiple_of((n - 1 - pos) * tk, tk), tk)

    def score(pos, slot, static_pos):
        s_scr[slot] = _dot_nt(q, k_ref[key_rows(pos), :])

    def park(pos, slot, static_pos):
        y = s_scr[slot] * (scale * LOG2E)
        l1p = jnp.log2(1.0 + jnp.exp2(_neg_abs(y)))
        drop = jnp.maximum(y, 0.0) + l1p
        log_beta = y - drop
        if static_pos is not None:
            row = lax.broadcasted_iota(jnp.int32, (tq, tk), 0)
            col = lax.broadcasted_iota(jnp.int32, (tq, tk), 1) + (n_diag - 1 - static_pos) * tk
            past = col < row
            drop = jnp.where(past, drop, 0.0)
            log_beta = jnp.where(past, log_beta, -jnp.inf)
        drop_scr[slot] = drop.astype(MXU_DTYPE)
        lb_scr[slot] = log_beta
        off = off_run_scr[...]
        off_scr[slot] = off
        off_run_scr[...] = off + jnp.sum(drop, axis=-1, keepdims=True)

    def weigh(pos, slot, static_pos):
        within = _dot(drop_scr[slot], u_ref[...])
        off = jnp.concatenate([off_scr[slot]] * (tk // LANES), axis=-1)
        w_scr[slot] = jnp.exp2(lb_scr[slot] - (within + off)).astype(MXU_DTYPE)

    def accumulate(pos, slot, static_pos):
        acc_scr[...] += _dot(w_scr[slot], v_ref[key_rows(pos), :])

    off_run_scr[...] = jnp.zeros((tq, LANES), jnp.float32)
    acc_scr[...] = jnp.zeros((tq, HEAD_DIM), jnp.float32)
    _run_pipeline([score, park, weigh, accumulate], n, n_diag, SB_UNROLL, aligned=True)
    o_ref[...] = (acc_scr[...] * _silu(z_ref[...])).astype(o_ref.dtype)


def _sb(p16, pf, tq=1024, tk=256):
    s = p16.shape[0]
    qb = _P16_OFF["qc"] // HEAD_DIM
    kb = _P16_OFF["kc"] // HEAD_DIM
    vb = _P16_OFF["vc"] // HEAD_DIM
    zb = _PF_OFF["zc"] // HEAD_DIM
    j = lax.broadcasted_iota(jnp.int32, (tk, tk), 0)
    c = lax.broadcasted_iota(jnp.int32, (tk, tk), 1)
    u = (j > c).astype(MXU_DTYPE)
    return pl.pallas_call(
        functools.partial(_sb_kernel, tq=tq, tk=tk),
        grid=(C_HEADS, s // tq),
        in_specs=[pl.BlockSpec((tq, HEAD_DIM), lambda h, i: (i, qb + h)),
                  pl.BlockSpec((s, HEAD_DIM), lambda h, i: (0, kb + h)),
                  pl.BlockSpec((s, HEAD_DIM), lambda h, i: (0, vb + h)),
                  pl.BlockSpec((tq, HEAD_DIM), lambda h, i: (i, zb + h)),
                  pl.BlockSpec((tk, tk), lambda h, i: (0, 0))],
        out_specs=pl.BlockSpec((tq, HEAD_DIM), lambda h, i: (i, h)),
        out_shape=jax.ShapeDtypeStruct((s, C_WIDTH), MXU_DTYPE),
        scratch_shapes=[pltpu.VMEM((2, tq, tk), jnp.float32),
                        pltpu.VMEM((2, tq, tk), MXU_DTYPE),
                        pltpu.VMEM((2, tq, tk), jnp.float32),
                        pltpu.VMEM((2, tq, LANES), jnp.float32),
                        pltpu.VMEM((tq, LANES), jnp.float32),
                        pltpu.VMEM((2, tq, tk), MXU_DTYPE),
                        pltpu.VMEM((tq, HEAD_DIM), jnp.float32)],
        compiler_params=_cparams("parallel", "arbitrary"),
        name="stick_breaking_attention",
    )(p16, p16, p16, pf, u)


def _swap_halves(w):
    half = w.shape[-1] // 2
    return jnp.concatenate([w[..., half:], w[..., :half]], axis=-1)


def _pad_cols(w, width):
    return jnp.pad(w, ((0, 0), (0, width - w.shape[-1])))


_GATE_COLS = (("ga", D_MODEL), ("gb", D_MODEL), ("gc", D_MODEL))
W_ALL_WIDTH = GATE_OFF + 3 * D_MODEL


W_IN_ROWS = 256


def _w_in_layout_table():
    src, is_kr = [], []
    for name, width in _P16_COLS + _PF_COLS + _GATE_COLS:
        if name == "kr_a":
            src.append(_IN_OFF["kr"][0])
            is_kr.append(1)
        elif name != "kr_b":
            assert width % W_IN_ROWS == 0
            src += [_IN_OFF[name][0] + t * W_IN_ROWS for t in range(width // W_IN_ROWS)]
            is_kr += [0] * (width // W_IN_ROWS)
    return jnp.asarray(src, jnp.int32), jnp.asarray(is_kr, jnp.int32)


def _w_in_layout_kernel(src_ref, is_kr_ref, w_ref, o_ref):
    x = w_ref[0]
    half = B_ROPE // 2
    kr = x[:B_ROPE]
    pad = jnp.zeros((LANES - B_ROPE, x.shape[1]), x.dtype)
    kr_rows = jnp.concatenate([kr, pad, kr[half:], kr[:half], pad], axis=0)
    x = jnp.where(is_kr_ref[pl.program_id(1)] == 1, kr_rows, x)
    o_ref[...] = x.astype(o_ref.dtype)


def _layout_w_in(w_in_t):
    depth, _, d = w_in_t.shape
    src, is_kr = _w_in_layout_table()
    return pl.pallas_call(
        _w_in_layout_kernel,
        grid_spec=pltpu.PrefetchScalarGridSpec(
            num_scalar_prefetch=2,
            grid=(depth, W_ALL_WIDTH // W_IN_ROWS),
            in_specs=[pl.BlockSpec((pl.Element(1), pl.Element(W_IN_ROWS), pl.Element(d)),
                                   lambda l, r, src, is_kr: (l, pl.multiple_of(src[r], B_ROPE), 0))],
            out_specs=pl.BlockSpec((None, W_IN_ROWS, d), lambda l, r, src, is_kr: (l, r, 0))),
        out_shape=jax.ShapeDtypeStruct((depth, W_ALL_WIDTH, d), MXU_DTYPE),
        compiler_params=_cparams("parallel", "arbitrary"),
        name="w_in_layout",
    )(src, is_kr, w_in_t)


def _cast_kernel(w_ref, o_ref):
    o_ref[...] = w_ref[...].astype(o_ref.dtype)


def _cast_weights(w, tr=512):
    depth, rows, cols = w.shape
    tr = min(tr, rows)
    spec = pl.BlockSpec((1, tr, cols), lambda l, r: (l, r, 0))
    return pl.pallas_call(
        _cast_kernel,
        grid=(depth, rows // tr),
        in_specs=[spec],
        out_specs=spec,
        out_shape=jax.ShapeDtypeStruct(w.shape, MXU_DTYPE),
        compiler_params=_cparams("parallel", "parallel"),
        name="cast_weights",
    )(w)


def _layout_w_q_up(w):
    parts = []
    for h in range(B_HEADS):
        nope = w[:, h * B_QK:h * B_QK + B_NOPE]
        rope = w[:, h * B_QK + B_NOPE:(h + 1) * B_QK]
        parts += [nope, _pad_cols(rope, LANES), _pad_cols(_swap_halves(rope), LANES)]
    return jnp.concatenate(parts, axis=1).astype(MXU_DTYPE)


def _layout_w_kv_up(w):
    per = B_NOPE + B_V
    ks = [w[:, h * per:h * per + B_NOPE] for h in range(B_HEADS)]
    vs = [w[:, h * per + B_NOPE:(h + 1) * per] for h in range(B_HEADS)]
    return jnp.concatenate(ks + vs, axis=1).astype(MXU_DTYPE)


def _rope_table(seq):
    pos = jnp.arange(seq, dtype=jnp.float32)
    inv = ROPE_THETA ** (-jnp.arange(0, B_ROPE, 2, dtype=jnp.float32) / B_ROPE)
    ang = pos[:, None] * inv[None, :]
    cos, sin = jnp.cos(ang), jnp.sin(ang)
    zeros = jnp.zeros((seq, LANES - B_ROPE), jnp.float32)
    return jnp.concatenate([cos, cos, zeros, -sin, sin, zeros], axis=1)


def kernel(x, norm_g, w_in, attn_sinks, rel_bias, g_q_lora, w_q_up, g_kv_lora, w_kv_up,
           w_proj_a, w_proj_b, w_proj_c, w_out, final_g):
    batch, seq, d = x.shape
    assert batch == 1 and d == D_MODEL and seq % 1024 == 0
    depth = w_in.shape[0]
    xs = x.reshape(seq, d)
    bias = _swa_bias(rel_bias)
    rope = _rope_table(seq)
    w_all = _layout_w_in(jnp.swapaxes(w_in, 1, 2))
    wa, wb, wc, wo = (_cast_weights(w) for w in (w_proj_a, w_proj_b, w_proj_c, w_out))

    for l in range(depth):
        h = _rmsnorm(xs, norm_g[l], MXU_DTYPE)
        p16 = _matmul(h, w_all, l, 0, P16_WIDTH, MXU_DTYPE, tm=1024, tn=768, transposed_w=True,
                      name="in_proj_bf16")
        pf = _matmul(h, w_all, l, P16_WIDTH, PF_WIDTH, jnp.float32, tm=1024, tn=768,
                     transposed_w=True, name="in_proj_f32")

        ya = _swa(p16, pf, attn_sinks[l], bias)
        qf, kf, vb = _mla_pre(pf, g_q_lora[l], g_kv_lora[l], _layout_w_q_up(w_q_up[l]),
                              _layout_w_kv_up(w_kv_up[l]), rope)
        yb = _mla(qf, kf, vb, pf)
        yc = _sb(p16, pf)

        merged = _merge(h, w_all, l, ya, yb, yc, wa, wb, wc)
        xs = _matmul(merged, wo, l, 0, D_MODEL, jnp.float32, tm=1024, tn=512, residual=xs,
                     name="out_proj")

    out = _rmsnorm(xs, final_g, jnp.float32)
    return out.reshape(batch, seq, d)
```

```python
import functools
import math

import jax
import jax.numpy as jnp
from jax import lax
from jax.experimental import pallas as pl
from jax.experimental.pallas import tpu as pltpu

D_MODEL = 4096
HEAD_DIM = 128
LANES = 128
BLOCK = 128
EPS = 1e-6
A_HEADS = 8
A_KV_HEADS = 2
A_GROUP = A_HEADS // A_KV_HEADS
A_WIDTH = A_HEADS * HEAD_DIM
A_KV_WIDTH = A_KV_HEADS * HEAD_DIM
NUM_BUCKETS = 32
MAX_DISTANCE = 128
B_HEADS = 4
B_Q_LORA = 1024
B_KV_LORA = 512
B_NOPE = 128
B_ROPE = 64
B_V = 128
B_QK = B_NOPE + B_ROPE
B_QK_PAD = 256
B_WIDTH = B_HEADS * B_V
ROPE_THETA = 10000.0
C_HEADS = 4
C_WIDTH = C_HEADS * HEAD_DIM

MXU_DTYPE = jnp.bfloat16
LOG2E = math.log2(math.e)
VMEM_LIMIT_BYTES = 56 * 1024 * 1024

_IN_SIZES = (A_WIDTH, A_KV_WIDTH, A_KV_WIDTH, A_WIDTH, B_Q_LORA, B_KV_LORA, B_ROPE, B_WIDTH,
             C_WIDTH, C_WIDTH, C_WIDTH, C_WIDTH, D_MODEL, D_MODEL, D_MODEL)
_IN_NAMES = ("qa", "ka", "va", "za", "cq", "ckv", "kr", "zb", "qc", "kc", "vc", "zc", "ga", "gb", "gc")
_IN_OFF = {}
_o = 0
for _n, _s in zip(_IN_NAMES, _IN_SIZES):
    _IN_OFF[_n] = (_o, _s)
    _o += _s

_P16_COLS = (("qa", A_WIDTH), ("ka", A_KV_WIDTH), ("va", A_KV_WIDTH),
             ("qc", C_WIDTH), ("kc", C_WIDTH), ("vc", C_WIDTH))
_PF_COLS = (("za", A_WIDTH), ("cq", B_Q_LORA), ("ckv", B_KV_LORA), ("kr_a", LANES), ("kr_b", LANES),
            ("zb", B_WIDTH), ("zc", C_WIDTH))


def _offsets(cols):
    out, o = {}, 0
    for n, s in cols:
        out[n] = o
        o += s
    return out, o


_P16_OFF, P16_WIDTH = _offsets(_P16_COLS)
_PF_OFF, PF_WIDTH = _offsets(_PF_COLS)
GATE_OFF = P16_WIDTH + PF_WIDTH


def _cparams(*sem):
    return pltpu.CompilerParams(dimension_semantics=sem, vmem_limit_bytes=VMEM_LIMIT_BYTES)


def _dot(a, b):
    return jnp.dot(a, b, preferred_element_type=jnp.float32)


def _dot_nt(a, b):
    return lax.dot_general(a, b, (((1,), (1,)), ((), ())), preferred_element_type=jnp.float32)


def _silu(z):
    return z * jax.nn.sigmoid(z)


def _neg_abs(x):
    bits = lax.bitcast_convert_type(x, jnp.uint32) | jnp.uint32(0x80000000)
    return lax.bitcast_convert_type(bits, jnp.float32)


NORM_ROWS = 16


def _rmsnorm_kernel(x_ref, g_ref, o_ref):
    def body(r, carry):
        rows = pl.ds(pl.multiple_of(r * NORM_ROWS, NORM_ROWS), NORM_ROWS)
        x = x_ref[rows, :]
        y = x * lax.rsqrt(jnp.mean(x * x, axis=-1, keepdims=True) + EPS)
        o_ref[rows, :] = (y * g_ref[...]).astype(o_ref.dtype)
        return carry

    lax.fori_loop(0, x_ref.shape[0] // NORM_ROWS, body, 0, unroll=8)


def _rmsnorm(x, g, out_dtype, tm=512):
    s, d = x.shape
    return pl.pallas_call(
        _rmsnorm_kernel,
        grid=(s // tm,),
        in_specs=[pl.BlockSpec((tm, d), lambda i: (i, 0)),
                  pl.BlockSpec((1, d), lambda i: (0, 0))],
        out_specs=pl.BlockSpec((tm, d), lambda i: (i, 0)),
        out_shape=jax.ShapeDtypeStruct((s, d), out_dtype),
        compiler_params=_cparams("parallel"),
        name="rmsnorm",
    )(x, g.reshape(1, d))


def _matmul_kernel(a_ref, w_ref, o_ref):
    o_ref[...] = _dot(a_ref[...], w_ref[...]).astype(o_ref.dtype)


def _matmul_nt_kernel(a_ref, wt_ref, o_ref):
    o_ref[...] = _dot_nt(a_ref[...], wt_ref[...]).astype(o_ref.dtype)


def _matmul_residual_kernel(a_ref, w_ref, r_ref, o_ref):
    o_ref[...] = r_ref[...] + _dot(a_ref[...], w_ref[...])


def _matmul(a, w, layer, col_off, n_cols, out_dtype, tm, tn, residual=None, transposed_w=False,
            name="matmul"):
    s, k = a.shape
    assert s % tm == 0 and n_cols % tn == 0 and col_off % tn == 0
    assert not (transposed_w and residual is not None)
    jo = col_off // tn
    if transposed_w:
        w_spec = pl.BlockSpec((None, tn, k), lambda i, j: (layer, j + jo, 0))
    else:
        w_spec = pl.BlockSpec((None, k, tn), lambda i, j: (layer, 0, j + jo))
    in_specs = [pl.BlockSpec((tm, k), lambda i, j: (i, 0)), w_spec]
    args = [a, w]
    body = _matmul_nt_kernel if transposed_w else _matmul_kernel
    if residual is not None:
        in_specs.append(pl.BlockSpec((tm, tn), lambda i, j: (i, j)))
        args.append(residual)
        body = _matmul_residual_kernel
    return pl.pallas_call(
        body,
        grid=(s // tm, n_cols // tn),
        in_specs=in_specs,
        out_specs=pl.BlockSpec((tm, tn), lambda i, j: (i, j)),
        out_shape=jax.ShapeDtypeStruct((s, n_cols), out_dtype),
        compiler_params=_cparams("parallel", "arbitrary"),
        name=name,
    )(*args)


def _merge_kernel(h_ref, wga_ref, wgb_ref, wgc_ref, ya_ref, yb_ref, yc_ref,
                  wa_ref, wb_ref, wc_ref, o_ref):
    h = h_ref[...]

    def branch(wg_ref, y_ref, w_ref):
        return jax.nn.sigmoid(_dot_nt(h, wg_ref[...])) * _dot(y_ref[...], w_ref[...])

    merged = (branch(wga_ref, ya_ref, wa_ref) + branch(wgb_ref, yb_ref, wb_ref)
              + branch(wgc_ref, yc_ref, wc_ref))
    o_ref[...] = merged.astype(o_ref.dtype)


def _merge(h, w_all, layer, ya, yb, yc, wa, wb, wc, tm=1024, tn=256):
    s = h.shape[0]
    nj = D_MODEL // tn
    g0 = GATE_OFF // tn

    def gate_spec(k):
        return pl.BlockSpec((None, tn, D_MODEL), lambda i, j: (layer, g0 + k * nj + j, 0))

    def row_spec(width):
        return pl.BlockSpec((tm, width), lambda i, j: (i, 0))

    def w_spec(width):
        return pl.BlockSpec((None, width, tn), lambda i, j: (layer, 0, j))

    return pl.pallas_call(
        _merge_kernel,
        grid=(s // tm, nj),
        in_specs=[row_spec(D_MODEL), gate_spec(0), gate_spec(1), gate_spec(2),
                  row_spec(A_WIDTH), row_spec(B_WIDTH), row_spec(C_WIDTH),
                  w_spec(A_WIDTH), w_spec(B_WIDTH), w_spec(C_WIDTH)],
        out_specs=pl.BlockSpec((tm, tn), lambda i, j: (i, j)),
        out_shape=jax.ShapeDtypeStruct((s, D_MODEL), MXU_DTYPE),
        compiler_params=_cparams("parallel", "arbitrary"),
        name="gated_merge",
    )(h, w_all, w_all, w_all, ya, yb, yc, wa, wb, wc)


def _t5_bucket(rel):
    n = jnp.maximum(rel, 0)
    max_exact = NUM_BUCKETS // 2
    logn = jnp.log(jnp.maximum(n, 1).astype(jnp.float32) / max_exact)
    large = max_exact + (logn / math.log(MAX_DISTANCE / max_exact)
                         * (NUM_BUCKETS - max_exact)).astype(jnp.int32)
    large = jnp.minimum(large, NUM_BUCKETS - 1)
    return jnp.where(n < max_exact, n, large)


def _swa_bias_kernel(rel_bias_ref, bucket_ref, o_ref):
    bucket = bucket_ref[...]
    for h in range(A_HEADS):
        acc = jnp.zeros(bucket.shape, jnp.float32)
        for b in range(NUM_BUCKETS):
            acc = jnp.where(bucket == b, rel_bias_ref[b, h], acc)
        o_ref[h] = acc


def _swa_bias(rel_bias):
    t = jnp.arange(BLOCK)[:, None]
    s = jnp.arange(2 * BLOCK)[None, :]
    bucket = _t5_bucket(BLOCK + t - s).astype(jnp.int32)
    return pl.pallas_call(
        _swa_bias_kernel,
        in_specs=[pl.BlockSpec(memory_space=pltpu.SMEM),
                  pl.BlockSpec((BLOCK, 2 * BLOCK), lambda: (0, 0))],
        out_specs=pl.BlockSpec((A_HEADS, BLOCK, 2 * BLOCK), lambda: (0, 0, 0)),
        out_shape=jax.ShapeDtypeStruct((A_HEADS, BLOCK, 2 * BLOCK), jnp.float32),
        name="swa_bias_table",
    )(rel_bias, bucket)


SWA_BLOCKS = 4


def _swa_kernel(sinks_ref, q_ref, kc_ref, kp_ref, vc_ref, vp_ref, bias_ref, z_ref, o_ref):
    n = pl.program_id(0)
    rows = A_GROUP * BLOCK
    t = lax.broadcasted_iota(jnp.int32, (rows, 2 * BLOCK), 0) & (BLOCK - 1)
    s = lax.broadcasted_iota(jnp.int32, (rows, 2 * BLOCK), 1)
    ok = (s > t) & (s <= t + BLOCK)
    ok_first = (s > jnp.maximum(t, jnp.where(n > 0, -1, BLOCK - 1))) & (s <= t + BLOCK)
    scale = HEAD_DIM ** -0.5

    def head_cols(h):
        return slice(h * HEAD_DIM, (h + 1) * HEAD_DIM)

    def block_rows(b):
        return slice(b * BLOCK, (b + 1) * BLOCK)

    def band(cur_ref, prev_ref, b, hk):
        prev = prev_ref[:, head_cols(hk)] if b == 0 else cur_ref[block_rows(b - 1), head_cols(hk)]
        return jnp.concatenate([prev, cur_ref[block_rows(b), head_cols(hk)]], axis=0)

    units = [(b, hk) for b in range(SWA_BLOCKS) for hk in range(A_KV_HEADS)]
    scores = []
    for b, hk in units:
        q = jnp.concatenate([q_ref[block_rows(b), head_cols(hk * A_GROUP + g)] for g in range(A_GROUP)],
                            axis=0)
        scores.append(_dot_nt(q, band(kc_ref, kp_ref, b, hk)))
    probs = []
    for (b, hk), sc in zip(units, scores):
        sc = jnp.where(ok_first if b == 0 else ok, sc * scale + bias_ref[hk], -jnp.inf)
        sink = jnp.concatenate([jnp.full((BLOCK, LANES), sinks_ref[hk * A_GROUP + g], jnp.float32)
                                for g in range(A_GROUP)], axis=0)
        m = jnp.maximum(jnp.max(sc, axis=-1, keepdims=True), sink)
        e = jnp.exp(sc - jnp.concatenate([m, m], axis=-1))
        inv = 1.0 / (jnp.sum(e, axis=-1, keepdims=True) + jnp.exp(sink - m))
        probs.append((e * jnp.concatenate([inv, inv], axis=-1)).astype(MXU_DTYPE))
    for (b, hk), p in zip(units, probs):
        out = _dot(p, band(vc_ref, vp_ref, b, hk))
        for g in range(A_GROUP):
            cols = head_cols(hk * A_GROUP + g)
            gate = _silu(z_ref[block_rows(b), cols])
            o_ref[block_rows(b), cols] = (out[g * BLOCK:(g + 1) * BLOCK] * gate).astype(o_ref.dtype)


def _swa(p16, pf, sinks, bias):
    s = p16.shape[0]
    qb = _P16_OFF["qa"] // A_WIDTH
    kb = _P16_OFF["ka"] // A_KV_WIDTH
    vb = _P16_OFF["va"] // A_KV_WIDTH
    zb = _PF_OFF["za"] // A_WIDTH
    tq = SWA_BLOCKS * BLOCK
    prev = lambda n: jnp.maximum(n * SWA_BLOCKS - 1, 0)
    return pl.pallas_call(
        _swa_kernel,
        grid=(s // tq,),
        in_specs=[pl.BlockSpec(memory_space=pltpu.SMEM),
                  pl.BlockSpec((tq, A_WIDTH), lambda n: (n, qb)),
                  pl.BlockSpec((tq, A_KV_WIDTH), lambda n: (n, kb)),
                  pl.BlockSpec((BLOCK, A_KV_WIDTH), lambda n: (prev(n), kb)),
                  pl.BlockSpec((tq, A_KV_WIDTH), lambda n: (n, vb)),
                  pl.BlockSpec((BLOCK, A_KV_WIDTH), lambda n: (prev(n), vb)),
                  pl.BlockSpec((A_KV_HEADS, A_GROUP * BLOCK, 2 * BLOCK), lambda n: (0, 0, 0)),
                  pl.BlockSpec((tq, A_WIDTH), lambda n: (n, zb))],
        out_specs=pl.BlockSpec((tq, A_WIDTH), lambda n: (n, 0)),
        out_shape=jax.ShapeDtypeStruct((s, A_WIDTH), MXU_DTYPE),
        compiler_params=_cparams("parallel"),
        name="swa_attention",
    )(sinks, p16, p16, p16, p16, p16, bias.reshape(A_KV_HEADS, A_GROUP * BLOCK, 2 * BLOCK), pf)


def _rms_cast(x, g):
    y = x * lax.rsqrt(jnp.mean(x * x, axis=-1, keepdims=True) + EPS)
    return (y * g).astype(MXU_DTYPE)


def _mla_pre_kernel(cq_ref, ckv_ref, kr_ref, gq_ref, gkv_ref, wq_ref, wkv_ref, rope_ref,
                    qf_ref, kf_ref, v_ref):
    tc = rope_ref[:, 0:LANES]
    ts = rope_ref[:, LANES:2 * LANES]
    qraw = _dot(_rms_cast(cq_ref[...], gq_ref[...]), wq_ref[...])
    for h in range(B_HEADS):
        nope, rope, swapped = (qraw[:, (3 * h + t) * LANES:(3 * h + t + 1) * LANES] for t in range(3))
        qf_ref[:, h * B_QK_PAD:h * B_QK_PAD + B_NOPE] = nope.astype(qf_ref.dtype)
        qf_ref[:, h * B_QK_PAD + B_NOPE:(h + 1) * B_QK_PAD] = (rope * tc + swapped * ts).astype(qf_ref.dtype)
    kvraw = _dot(_rms_cast(ckv_ref[...], gkv_ref[...]), wkv_ref[...])
    kr = kr_ref[...]
    krot = (kr[:, 0:LANES] * tc + kr[:, LANES:2 * LANES] * ts).astype(kf_ref.dtype)
    for h in range(B_HEADS):
        kf_ref[:, h * B_QK_PAD:h * B_QK_PAD + B_NOPE] = (
            kvraw[:, h * B_NOPE:(h + 1) * B_NOPE].astype(kf_ref.dtype))
        kf_ref[:, h * B_QK_PAD + B_NOPE:(h + 1) * B_QK_PAD] = krot
    v_ref[...] = kvraw[:, B_HEADS * B_NOPE:].astype(v_ref.dtype)


def _mla_pre(pf, gq, gkv, wq, wkv, rope, tm=512):
    s = pf.shape[0]
    row = lambda width, off: pl.BlockSpec((tm, width), lambda i: (i, off // width))
    full = lambda a: pl.BlockSpec(a.shape, lambda i: (0, 0))
    gq = gq.reshape(1, B_Q_LORA)
    gkv = gkv.reshape(1, B_KV_LORA)
    out_row = lambda width: pl.BlockSpec((tm, width), lambda i: (i, 0))
    return pl.pallas_call(
        _mla_pre_kernel,
        grid=(s // tm,),
        in_specs=[row(B_Q_LORA, _PF_OFF["cq"]), row(B_KV_LORA, _PF_OFF["ckv"]),
                  row(2 * LANES, _PF_OFF["kr_a"]), full(gq), full(gkv), full(wq), full(wkv),
                  pl.BlockSpec((tm, 2 * LANES), lambda i: (i, 0))],
        out_specs=[out_row(B_HEADS * B_QK_PAD), out_row(B_HEADS * B_QK_PAD), out_row(B_WIDTH)],
        out_shape=[jax.ShapeDtypeStruct((s, B_HEADS * B_QK_PAD), MXU_DTYPE),
                   jax.ShapeDtypeStruct((s, B_HEADS * B_QK_PAD), MXU_DTYPE),
                   jax.ShapeDtypeStruct((s, B_WIDTH), MXU_DTYPE)],
        compiler_params=_cparams("parallel"),
        name="mla_pre",
    )(pf, pf, pf, gq, gkv, wq, wkv, rope)


def _run_pipeline(stages, n, n_static, unroll):
    depth = len(stages)
    assert unroll % 2 == 0 and depth >= 2

    def run(k, pos, slot, static_pos=None):
        stages[k](pos, slot, static_pos)

    def prologue(length, limit):
        for tt in range(length):
            for k in reversed(range(depth)):
                pos = tt - k
                if 0 <= pos < limit:
                    run(k, pos, pos % 2, pos if pos < n_static else None)

    def steady(tt, slot):
        for k in reversed(range(depth)):
            run(k, tt - k, (slot + k) % 2)

    def main_loop(first):
        def body(u, carry):
            for j in range(unroll):
                steady(first + unroll * u + j, (first + j) % 2)
            return carry

        lax.fori_loop(0, (n - first) // unroll, body, 0)

    def drain():
        for e in range(depth - 1):
            for k in reversed(range(e + 1, depth)):
                run(k, n + e - k, (n + e - k) % 2)

    step = math.gcd(n_static, unroll)
    starts = [p for p in range(n_static + depth - 1, n_static + depth - 1 + unroll) if p % step == 0]
    for m in range(2 * n_static, n_static + depth + 2 * unroll, n_static):
        assert sum(1 for p in starts if m >= p and (m - p) % unroll == 0) == 1, m

    def long(p):
        prologue(p, limit=p)
        main_loop(p)
        drain()

    pl.when(n == n_static)(functools.partial(prologue, n_static + depth - 1, n_static))
    for p in starts:
        pl.when((n >= p) & ((n - p) % unroll == 0))(functools.partial(long, p))


MLA_UNROLL = 4


def _mla_kernel(q_ref, k_ref, v_ref, z_ref, o_ref,
                y_scr, p_scr, alpha_scr, m_scr, l_scr, acc_scr, *, tq, tk):
    n_diag = tq // tk
    i = pl.program_id(1)
    c = (B_QK ** -0.5) * LOG2E

    def key_rows(pos):
        kb = jnp.where(pos < n_diag, n_diag * i + pos, pos - n_diag)
        return pl.ds(pl.multiple_of(kb * tk, tk), tk)

    def live(static_pos):
        return slice(0 if static_pos is None else static_pos * tk, tq)

    def scores(pos, slot, static_pos):
        rows = live(static_pos)
        y_scr[slot, rows] = _dot_nt(q_ref[rows, :], k_ref[key_rows(pos), :]) * c

    def softmax(pos, slot, static_pos):
        rows = live(static_pos)
        y = y_scr[slot, rows]
        if static_pos is not None:
            y = jnp.where(lax.broadcasted_iota(jnp.int32, y.shape, 1)
                          <= lax.broadcasted_iota(jnp.int32, y.shape, 0), y, -jnp.inf)
        m = m_scr[rows]
        m_new = jnp.maximum(m, jnp.max(y, axis=-1, keepdims=True))
        alpha = jnp.exp2(m - m_new)
        p = jnp.exp2(y - jnp.concatenate([m_new] * (tk // LANES), axis=-1))
        l_scr[rows] = alpha * l_scr[rows] + jnp.sum(p, axis=-1, keepdims=True)
        m_scr[rows] = m_new
        alpha_scr[slot, rows] = alpha
        p_scr[slot, rows] = p.astype(MXU_DTYPE)

    def accumulate(pos, slot, static_pos):
        rows = live(static_pos)
        acc_scr[rows] = (alpha_scr[slot, rows] * acc_scr[rows]
                         + _dot(p_scr[slot, rows], v_ref[key_rows(pos), :]))

    m_scr[...] = jnp.full((tq, LANES), -jnp.inf, jnp.float32)
    l_scr[...] = jnp.zeros((tq, LANES), jnp.float32)
    acc_scr[...] = jnp.zeros((tq, B_V), jnp.float32)
    _run_pipeline([scores, softmax, accumulate], n_diag * (i + 1), n_diag, MLA_UNROLL)
    o_ref[...] = ((acc_scr[...] / l_scr[...]) * _silu(z_ref[...])).astype(o_ref.dtype)


def _mla(qf, kf, v, pf, tq=1024, tk=512):
    s = qf.shape[0]
    zb = _PF_OFF["zb"] // B_V
    return pl.pallas_call(
        functools.partial(_mla_kernel, tq=tq, tk=tk),
        grid=(B_HEADS, s // tq),
        in_specs=[pl.BlockSpec((tq, B_QK_PAD), lambda h, i: (i, h)),
                  pl.BlockSpec((s, B_QK_PAD), lambda h, i: (0, h)),
                  pl.BlockSpec((s, B_V), lambda h, i: (0, h)),
                  pl.BlockSpec((tq, B_V), lambda h, i: (i, zb + h))],
        out_specs=pl.BlockSpec((tq, B_V), lambda h, i: (i, h)),
        out_shape=jax.ShapeDtypeStruct((s, B_WIDTH), MXU_DTYPE),
        scratch_shapes=[pltpu.VMEM((2, tq, tk), jnp.float32),
                        pltpu.VMEM((2, tq, tk), MXU_DTYPE),
                        pltpu.VMEM((2, tq, LANES), jnp.float32),
                        pltpu.VMEM((tq, LANES), jnp.float32),
                        pltpu.VMEM((tq, LANES), jnp.float32),
                        pltpu.VMEM((tq, B_V), jnp.float32)],
        compiler_params=_cparams("parallel", "arbitrary"),
        name="mla_attention",
    )(qf, kf, v, pf)


SB_UNROLL = 4


def _sb_kernel(q_ref, k_ref, v_ref, z_ref, u_ref, o_ref,
               s_scr, drop_scr, lb_scr, off_scr, off_run_scr, w_scr, acc_scr, *, tq, tk):
    n_diag = tq // tk
    i = pl.program_id(1)
    scale = HEAD_DIM ** -0.5
    n = n_diag * (i + 1)

    def key_rows(pos):
        return pl.ds(pl.multiple_of((n - 1 - pos) * tk, tk), tk)

    def live(static_pos):
        return slice(0 if static_pos is None else (n_diag - 1 - static_pos) * tk, tq)

    def score(pos, slot, static_pos):
        rows = live(static_pos)
        s_scr[slot, rows] = _dot_nt(q_ref[rows, :], k_ref[key_rows(pos), :])

    def park(pos, slot, static_pos):
        rows = live(static_pos)
        y = s_scr[slot, rows] * (scale * LOG2E)
        l1p = jnp.log2(1.0 + jnp.exp2(_neg_abs(y)))
        drop = jnp.maximum(y, 0.0) + l1p
        log_beta = y - drop
        if static_pos is not None:
            past = (lax.broadcasted_iota(jnp.int32, y.shape, 1)
                    < lax.broadcasted_iota(jnp.int32, y.shape, 0))
            drop = jnp.where(past, drop, 0.0)
            log_beta = jnp.where(past, log_beta, -jnp.inf)
        drop_scr[slot, rows] = drop.astype(MXU_DTYPE)
        lb_scr[slot, rows] = log_beta
        off = off_run_scr[rows]
        off_scr[slot, rows] = off
        off_run_scr[rows] = off + jnp.sum(drop, axis=-1, keepdims=True)

    def weigh(pos, slot, static_pos):
        rows = live(static_pos)
        within = _dot(drop_scr[slot, rows], u_ref[...])
        off = jnp.concatenate([off_scr[slot, rows]] * (tk // LANES), axis=-1)
        w_scr[slot, rows] = jnp.exp2(lb_scr[slot, rows] - (within + off)).astype(MXU_DTYPE)

    def accumulate(pos, slot, static_pos):
        rows = live(static_pos)
        acc_scr[rows] += _dot(w_scr[slot, rows], v_ref[key_rows(pos), :])

    off_run_scr[...] = jnp.zeros((tq, LANES), jnp.float32)
    acc_scr[...] = jnp.zeros((tq, HEAD_DIM), jnp.float32)
    _run_pipeline([score, park, weigh, accumulate], n, n_diag, SB_UNROLL)
    o_ref[...] = (acc_scr[...] * _silu(z_ref[...])).astype(o_ref.dtype)


def _sb(p16, pf, tq=1024, tk=256):
    s = p16.shape[0]
    qb = _P16_OFF["qc"] // HEAD_DIM
    kb = _P16_OFF["kc"] // HEAD_DIM
    vb = _P16_OFF["vc"] // HEAD_DIM
    zb = _PF_OFF["zc"] // HEAD_DIM
    j = lax.broadcasted_iota(jnp.int32, (tk, tk), 0)
    c = lax.broadcasted_iota(jnp.int32, (tk, tk), 1)
    u = (j > c).astype(MXU_DTYPE)
    return pl.pallas_call(
        functools.partial(_sb_kernel, tq=tq, tk=tk),
        grid=(C_HEADS, s // tq),
        in_specs=[pl.BlockSpec((tq, HEAD_DIM), lambda h, i: (i, qb + h)),
                  pl.BlockSpec((s, HEAD_DIM), lambda h, i: (0, kb + h)),
                  pl.BlockSpec((s, HEAD_DIM), lambda h, i: (0, vb + h)),
                  pl.BlockSpec((tq, HEAD_DIM), lambda h, i: (i, zb + h)),
                  pl.BlockSpec((tk, tk), lambda h, i: (0, 0))],
        out_specs=pl.BlockSpec((tq, HEAD_DIM), lambda h, i: (i, h)),
        out_shape=jax.ShapeDtypeStruct((s, C_WIDTH), MXU_DTYPE),
        scratch_shapes=[pltpu.VMEM((2, tq, tk), jnp.float32),
                        pltpu.VMEM((2, tq, tk), MXU_DTYPE),
                        pltpu.VMEM((2, tq, tk), jnp.float32),
                        pltpu.VMEM((2, tq, LANES), jnp.float32),
                        pltpu.VMEM((tq, LANES), jnp.float32),
                        pltpu.VMEM((2, tq, tk), MXU_DTYPE),
                        pltpu.VMEM((tq, HEAD_DIM), jnp.float32)],
        compiler_params=_cparams("parallel", "arbitrary"),
        name="stick_breaking_attention",
    )(p16, p16, p16, pf, u)


def _swap_halves(w):
    half = w.shape[-1] // 2
    return jnp.concatenate([w[..., half:], w[..., :half]], axis=-1)


def _pad_cols(w, width):
    return jnp.pad(w, ((0, 0), (0, width - w.shape[-1])))


_GATE_COLS = (("ga", D_MODEL), ("gb", D_MODEL), ("gc", D_MODEL))
W_ALL_WIDTH = GATE_OFF + 3 * D_MODEL


W_IN_ROWS = 256


def _w_in_layout_table():
    src, is_kr = [], []
    for name, width in _P16_COLS + _PF_COLS + _GATE_COLS:
        if name == "kr_a":
            src.append(_IN_OFF["kr"][0])
            is_kr.append(1)
        elif name != "kr_b":
            assert width % W_IN_ROWS == 0
            src += [_IN_OFF[name][0] + t * W_IN_ROWS for t in range(width // W_IN_ROWS)]
            is_kr += [0] * (width // W_IN_ROWS)
    return jnp.asarray(src, jnp.int32), jnp.asarray(is_kr, jnp.int32)


def _w_in_layout_kernel(src_ref, is_kr_ref, w_ref, o_ref):
    x = w_ref[0]
    half = B_ROPE // 2
    kr = x[:B_ROPE]
    pad = jnp.zeros((LANES - B_ROPE, x.shape[1]), x.dtype)
    kr_rows = jnp.concatenate([kr, pad, kr[half:], kr[:half], pad], axis=0)
    x = jnp.where(is_kr_ref[pl.program_id(1)] == 1, kr_rows, x)
    o_ref[...] = x.astype(o_ref.dtype)


def _layout_w_in(w_in_t):
    depth, _, d = w_in_t.shape
    src, is_kr = _w_in_layout_table()
    return pl.pallas_call(
        _w_in_layout_kernel,
        grid_spec=pltpu.PrefetchScalarGridSpec(
            num_scalar_prefetch=2,
            grid=(depth, W_ALL_WIDTH // W_IN_ROWS),
            in_specs=[pl.BlockSpec((pl.Element(1), pl.Element(W_IN_ROWS), pl.Element(d)),
                                   lambda l, r, src, is_kr: (l, pl.multiple_of(src[r], B_ROPE), 0))],
            out_specs=pl.BlockSpec((None, W_IN_ROWS, d), lambda l, r, src, is_kr: (l, r, 0))),
        out_shape=jax.ShapeDtypeStruct((depth, W_ALL_WIDTH, d), MXU_DTYPE),
        compiler_params=_cparams("parallel", "arbitrary"),
        name="w_in_layout",
    )(src, is_kr, w_in_t)


def _cast_kernel(w_ref, o_ref):
    o_ref[...] = w_ref[...].astype(o_ref.dtype)


def _cast_weights(w, tr=512):
    depth, rows, cols = w.shape
    tr = min(tr, rows)
    spec = pl.BlockSpec((1, tr, cols), lambda l, r: (l, r, 0))
    return pl.pallas_call(
        _cast_kernel,
        grid=(depth, rows // tr),
        in_specs=[spec],
        out_specs=spec,
        out_shape=jax.ShapeDtypeStruct(w.shape, MXU_DTYPE),
        compiler_params=_cparams("parallel", "parallel"),
        name="cast_weights",
    )(w)


def _layout_w_q_up(w):
    parts = []
    for h in range(B_HEADS):
        nope = w[:, h * B_QK:h * B_QK + B_NOPE]
        rope = w[:, h * B_QK + B_NOPE:(h + 1) * B_QK]
        parts += [nope, _pad_cols(rope, LANES), _pad_cols(_swap_halves(rope), LANES)]
    return jnp.concatenate(parts, axis=1).astype(MXU_DTYPE)


def _layout_w_kv_up(w):
    per = B_NOPE + B_V
    ks = [w[:, h * per:h * per + B_NOPE] for h in range(B_HEADS)]
    vs = [w[:, h * per + B_NOPE:(h + 1) * per] for h in range(B_HEADS)]
    return jnp.concatenate(ks + vs, axis=1).astype(MXU_DTYPE)


def _rope_table(seq):
    pos = jnp.arange(seq, dtype=jnp.float32)
    inv = ROPE_THETA ** (-jnp.arange(0, B_ROPE, 2, dtype=jnp.float32) / B_ROPE)
    ang = pos[:, None] * inv[None, :]
    cos, sin = jnp.cos(ang), jnp.sin(ang)
    zeros = jnp.zeros((seq, LANES - B_ROPE), jnp.float32)
    return jnp.concatenate([cos, cos, zeros, -sin, sin, zeros], axis=1)


def kernel(x, norm_g, w_in, attn_sinks, rel_bias, g_q_lora, w_q_up, g_kv_lora, w_kv_up,
           w_proj_a, w_proj_b, w_proj_c, w_out, final_g):
    batch, seq, d = x.shape
    assert batch == 1 and d == D_MODEL and seq % 1024 == 0
    depth = w_in.shape[0]
    xs = x.reshape(seq, d)
    bias = _swa_bias(rel_bias)
    rope = _rope_table(seq)
    w_all = _layout_w_in(jnp.swapaxes(w_in, 1, 2))
    wa, wb, wc, wo = (_cast_weights(w) for w in (w_proj_a, w_proj_b, w_proj_c, w_out))

    for l in range(depth):
        h = _rmsnorm(xs, norm_g[l], MXU_DTYPE)
        p16 = _matmul(h, w_all, l, 0, P16_WIDTH, MXU_DTYPE, tm=1024, tn=768, transposed_w=True,
                      name="in_proj_bf16")
        pf = _matmul(h, w_all, l, P16_WIDTH, PF_WIDTH, jnp.float32, tm=1024, tn=768,
                     transposed_w=True, name="in_proj_f32")

        ya = _swa(p16, pf, attn_sinks[l], bias)
        qf, kf, vb = _mla_pre(pf, g_q_lora[l], g_kv_lora[l], _layout_w_q_up(w_q_up[l]),
                              _layout_w_kv_up(w_kv_up[l]), rope)
        yb = _mla(qf, kf, vb, pf)
        yc = _sb(p16, pf)

        merged = _merge(h, w_all, l, ya, yb, yc, wa, wb, wc)
        xs = _matmul(merged, wo, l, 0, D_MODEL, jnp.float32, tm=1024, tn=512, residual=xs,
                     name="out_proj")

    out = _rmsnorm(xs, final_g, jnp.float32)
    return out.reshape(batch, seq, d)
```
